```python
import math, functools
import jax, jax.numpy as jnp
from jax import lax
import numpy as np

D_MODEL = 1024
BATCH = 2
SEQ = 16384
DEPTH = 1
DEC_BATCH = 32
DEC_SEQ = 32
PAST_LEN = 4096

CHUNK = 64
H_A = 4
DK_A = 128
DV_A = 128
CONV_W = 4
KEY_DIM_A = H_A * DK_A
VAL_DIM_A = H_A * DV_A
CONV_DIM = 2 * KEY_DIM_A + VAL_DIM_A
H_B = 8
HD_B = 64
ATT_DIM_B = H_B * HD_B
N_BACK = 8
BAND_PAST = N_BACK * CHUNK
BAND = BAND_PAST + CHUNK
REL_MAX = 256
MIX_DIM = VAL_DIM_A + ATT_DIM_B
IN_SPLITS = (CONV_DIM, VAL_DIM_A, H_A, H_A, ATT_DIM_B, ATT_DIM_B, ATT_DIM_B)
IN_DIM = sum(IN_SPLITS)
IN_OFFSETS = tuple(np.cumsum(IN_SPLITS)[:-1].tolist())
N_EXPERTS = 32
TOP_K = 4
D_FF = 1024
SWIGLU_LIMIT = 7.0
SWIGLU_ALPHA = 1.702
MOE_BLOCK = 128
PLE_DIM = 256
RMS_EPS = 1e-6
L2_EPS = 1e-6

kernel_name = 'hybrid_delta_chunkband_moe_stream_step'

F32 = jnp.float32


def rms_norm(x, g):
    xf = x.astype(F32)
    y = xf * lax.rsqrt(jnp.mean(xf * xf, axis=-1, keepdims=True) + RMS_EPS)
    return (y * g.astype(F32)).astype(x.dtype)


def l2_norm(x):
    xf = x.astype(F32)
    return xf * lax.rsqrt(jnp.sum(xf * xf, axis=-1, keepdims=True) + L2_EPS)


def chunk_gated_delta(q, k, v, g, beta, s0):
    bsz, t, h, _ = q.shape
    dv = v.shape[-1]
    cb = min(CHUNK, t)
    n = t // cb

    def blocks(a):
        a = a.reshape((bsz, n, cb) + a.shape[2:])
        return jnp.swapaxes(jnp.moveaxis(a, 1, 0), 2, 3)

    qc, kc, vc = blocks(q), blocks(k), blocks(v)
    gc = jnp.cumsum(blocks(g), axis=-1)
    bc = blocks(beta)
    idx = jnp.arange(cb)
    causal = idx[:, None] >= idx[None, :]
    strict = idx[:, None] > idx[None, :]
    decay = jnp.exp(jnp.where(causal, gc[..., :, None] - gc[..., None, :], -jnp.inf))
    kb = kc * bc[..., None]
    a_mat = jnp.where(strict, jnp.einsum('nbhik,nbhjk->nbhij', kb, kc) * decay, 0.0)
    eye = jnp.eye(cb, dtype=F32)
    rhs = jnp.concatenate([vc * bc[..., None], kb * jnp.exp(gc)[..., None]], axis=-1)
    sol = lax.linalg.triangular_solve(a_mat + eye, rhs, left_side=True, lower=True,
                                      unit_diagonal=True)
    u0, w = sol[..., :dv], sol[..., dv:]
    qk = jnp.einsum('nbhik,nbhjk->nbhij', qc, kc) * decay
    q_dec = qc * jnp.exp(gc)[..., None]
    g_last = gc[..., -1]
    k_dec = kc * jnp.exp(g_last[..., None] - gc)[..., None]

    def step(s, inp):
        qk_i, qd_i, kd_i, u0_i, w_i, gl_i = inp
        u = u0_i - jnp.einsum('bhik,bhkv->bhiv', w_i, s)
        o = jnp.einsum('bhik,bhkv->bhiv', qd_i, s) + jnp.einsum('bhij,bhjv->bhiv', qk_i, u)
        s = s * jnp.exp(gl_i)[..., None, None] + jnp.einsum('bhjk,bhjv->bhkv', kd_i, u)
        return s, o

    s_fin, o = lax.scan(step, s0.astype(F32), (qk, q_dec, k_dec, u0, w, g_last))
    o = jnp.moveaxis(jnp.swapaxes(o, 2, 3), 0, 1).reshape(bsz, t, h, dv)
    return o, s_fin


def gated_delta_mixer(qkv, z, b, a, conv_buf, s0, conv_w, a_log, dt_bias, g_onorm):
    bsz, t, _ = qkv.shape
    xp = jnp.concatenate([conv_buf.astype(qkv.dtype), qkv], axis=1)
    conv = lax.conv_general_dilated(xp, conv_w[:, None, :].astype(qkv.dtype), (1,), 'VALID',
                                    dimension_numbers=('NWC', 'WIO', 'NWC'),
                                    feature_group_count=CONV_DIM)
    conv = jax.nn.silu(conv)
    new_buf = xp[:, xp.shape[1] - (CONV_W - 1):]
    q, k, v = jnp.split(conv, [KEY_DIM_A, 2 * KEY_DIM_A], axis=-1)
    q = l2_norm(q.reshape(bsz, t, H_A, DK_A)) * (DK_A ** -0.5)
    k = l2_norm(k.reshape(bsz, t, H_A, DK_A))
    v = v.reshape(bsz, t, H_A, DV_A).astype(F32)
    beta = jax.nn.sigmoid(b.astype(F32))
    g = -jnp.exp(a_log.astype(F32)) * jax.nn.softplus(a.astype(F32) + dt_bias.astype(F32))
    o, s_new = chunk_gated_delta(q, k, v, g, beta, s0)
    o = rms_norm(o, g_onorm) * jax.nn.silu(z.reshape(bsz, t, H_A, DV_A).astype(F32))
    return o.reshape(bsz, t, VAL_DIM_A).astype(qkv.dtype), new_buf, s_new


def band_attention(q, k, v, q_pos, k_pos, rel_bias):
    s = jnp.einsum('bqhd,bkhd->bhqk', q, k).astype(F32) * (HD_B ** -0.5)
    rel = jnp.clip(q_pos[:, None] - k_pos[None, :], -REL_MAX, REL_MAX) + REL_MAX
    s = s + rel_bias.astype(F32)[:, rel]
    qc = q_pos // CHUNK
    kc = k_pos // CHUNK
    ok = (k_pos[None, :] >= 0) & (kc[None, :] <= qc[:, None]) & (kc[None, :] >= qc[:, None] - N_BACK)
    p = jax.nn.softmax(jnp.where(ok, s, -jnp.inf), axis=-1)
    return jnp.einsum('bhqk,bkhd->bqhd', p.astype(v.dtype), v)


def attend_prompt(q, k, v, rel_bias):
    bsz, t = q.shape[:2]
    n = t // CHUNK
    pad = ((0, 0), (BAND_PAST, 0), (0, 0), (0, 0))
    kp = jnp.pad(k, pad)
    vp = jnp.pad(v, pad)

    def one_chunk(c):
        start = c * CHUNK
        qb = lax.dynamic_slice_in_dim(q, start, CHUNK, axis=1)
        kb = lax.dynamic_slice_in_dim(kp, start, BAND, axis=1)
        vb = lax.dynamic_slice_in_dim(vp, start, BAND, axis=1)
        q_pos = start + jnp.arange(CHUNK)
        k_pos = start - BAND_PAST + jnp.arange(BAND)
        return band_attention(qb, kb, vb, q_pos, k_pos, rel_bias)

    o = lax.map(one_chunk, jnp.arange(n))
    o = jnp.moveaxis(o, 0, 1).reshape(bsz, t, H_B, HD_B)
    keep = min(BAND_PAST, t)
    return o, k[:, t - keep:], v[:, t - keep:]


def attend_sample(q, k, v, rel_bias, cache_k, cache_v):
    ds = q.shape[1]
    past = cache_k.shape[1]
    k_all = jnp.concatenate([cache_k.astype(k.dtype), k], axis=1)
    v_all = jnp.concatenate([cache_v.astype(v.dtype), v], axis=1)
    q_pos = PAST_LEN + jnp.arange(ds)
    k_pos = PAST_LEN - past + jnp.arange(past + ds)
    o = band_attention(q, k_all, v_all, q_pos, k_pos, rel_bias)
    return o, k, v


def moe_ffn(h, w_router, b_router, w_gu, b_gu, w_down, b_down):
    shp = h.shape
    x = h.reshape(-1, shp[-1])
    t = x.shape[0]
    logits = (x @ w_router).astype(F32) + b_router.astype(F32)
    top_val, top_idx = lax.top_k(logits, TOP_K)
    gates = jax.nn.softmax(top_val, axis=-1)
    tk = t * TOP_K
    e_flat = top_idx.reshape(tk)
    tok_flat = jnp.arange(tk, dtype=jnp.int32) // TOP_K
    order = jnp.argsort(e_flat)
    e_sorted = e_flat[order]
    tok_sorted = tok_flat[order]
    counts = jnp.zeros((N_EXPERTS,), jnp.int32).at[e_flat].add(1)
    padded = (counts + MOE_BLOCK - 1) // MOE_BLOCK * MOE_BLOCK
    start = jnp.cumsum(counts) - counts
    pend = jnp.cumsum(padded)
    pstart = pend - padded
    dest = pstart[e_sorted] + jnp.arange(tk, dtype=jnp.int32) - start[e_sorted]
    n_blocks = -(-tk // MOE_BLOCK) + N_EXPERTS
    rows_tok = jnp.zeros((n_blocks * MOE_BLOCK,), jnp.int32).at[dest].set(tok_sorted)
    block_first = jnp.arange(n_blocks, dtype=jnp.int32) * MOE_BLOCK
    block_expert = jnp.minimum(jnp.searchsorted(pend, block_first, side='right'), N_EXPERTS - 1)
    xb = x[rows_tok].reshape(n_blocks, MOE_BLOCK, shp[-1])

    def expert_block(args):
        xblk, e = args
        gu = xblk @ w_gu[e] + b_gu[e]
        gate, up = gu[:, :D_FF], gu[:, D_FF:]
        gate = jnp.minimum(gate, SWIGLU_LIMIT)
        up = jnp.clip(up, -SWIGLU_LIMIT, SWIGLU_LIMIT)
        hid = (up + 1.0) * (gate * jax.nn.sigmoid(gate * SWIGLU_ALPHA))
        return hid @ w_down[e] + b_down[e]

    yb = lax.map(expert_block, (xb, block_expert)).reshape(n_blocks * MOE_BLOCK, shp[-1])
    y_assign = yb[dest] * gates.reshape(tk)[order][:, None].astype(yb.dtype)
    y = jax.ops.segment_sum(y_assign, tok_sorted, num_segments=t)
    return y.reshape(shp)


def layer_forward(x, p_l, conv_buf, s0, attend, g_mix, w_in, conv_w, a_log, dt_bias, g_onorm,
                  g_qnorm, g_knorm, rel_bias, w_out, g_ffn, w_router, b_router, w_gu, b_gu,
                  w_down, b_down, g_ple, w_ple_gate, w_ple_proj):
    bsz, t, _ = x.shape
    h = rms_norm(x, g_mix)
    qkv_a, z_a, beta_a, alpha_a, q_b, k_b, v_b = jnp.split(h @ w_in, IN_OFFSETS, axis=-1)
    o_a, conv_new, s_new = gated_delta_mixer(qkv_a, z_a, beta_a, alpha_a, conv_buf, s0,
                                             conv_w, a_log, dt_bias, g_onorm)
    q_b = rms_norm(q_b.reshape(bsz, t, H_B, HD_B), g_qnorm)
    k_b = rms_norm(k_b.reshape(bsz, t, H_B, HD_B), g_knorm)
    v_b = v_b.reshape(bsz, t, H_B, HD_B)
    o_b, k_rows, v_rows = attend(q_b, k_b, v_b, rel_bias)
    x = x + jnp.concatenate([o_a, o_b.reshape(bsz, t, ATT_DIM_B)], axis=-1) @ w_out
    x = x + moe_ffn(rms_norm(x, g_ffn), w_router, b_router, w_gu, b_gu, w_down, b_down)
    gate = jax.nn.sigmoid(rms_norm(x, g_ple) @ w_ple_gate)
    x = x + gate * (p_l @ w_ple_proj)
    return x, conv_new, s_new, k_rows, v_rows


def setup_inputs(seed: int = 0) -> dict:
    key = jax.random.key(seed)
    ks = iter(jax.random.split(key, 40))

    def nrm(shape, scale=1.0):
        return jax.random.normal(next(ks), shape, jnp.float32) * scale

    def gain(n):
        return 1.0 + nrm((DEPTH, n), 0.05)

    band_len = min(BAND_PAST, PAST_LEN)
    a_init = jax.random.uniform(next(ks), (DEPTH, H_A), jnp.float32, 1.0, 16.0)
    dt = jnp.exp(jax.random.uniform(next(ks), (DEPTH, H_A), jnp.float32,
                                    math.log(1e-3), math.log(1e-1)))
    return {
        'x_prompt': nrm((BATCH, SEQ, D_MODEL)),
        'x_sample': nrm((DEC_BATCH, DEC_SEQ, D_MODEL)),
        'state_delta': nrm((DEPTH, DEC_BATCH, H_A, DK_A, DV_A), 0.1),
        'state_conv': nrm((DEPTH, DEC_BATCH, CONV_W - 1, CONV_DIM)),
        'cache_k': nrm((DEPTH, DEC_BATCH, band_len, H_B, HD_B)),
        'cache_v': nrm((DEPTH, DEC_BATCH, band_len, H_B, HD_B)),
        'p_prompt': nrm((DEPTH, BATCH, SEQ, PLE_DIM)),
        'p_sample': nrm((DEPTH, DEC_BATCH, DEC_SEQ, PLE_DIM)),
        'g_mix': gain(D_MODEL),
        'w_in': nrm((DEPTH, D_MODEL, IN_DIM), D_MODEL ** -0.5),
        'conv_w': nrm((DEPTH, CONV_W, CONV_DIM), CONV_W ** -0.5),
        'a_log': jnp.log(a_init),
        'dt_bias': dt + jnp.log(-jnp.expm1(-dt)),
        'g_onorm': gain(DV_A),
        'g_qnorm': gain(HD_B),
        'g_knorm': gain(HD_B),
        'rel_bias': nrm((DEPTH, H_B, 2 * REL_MAX + 1), 0.1),
        'w_out': nrm((DEPTH, MIX_DIM, D_MODEL), MIX_DIM ** -0.5),
        'g_ffn': gain(D_MODEL),
        'w_router': nrm((DEPTH, D_MODEL, N_EXPERTS), D_MODEL ** -0.5),
        'b_router': nrm((DEPTH, N_EXPERTS), 0.01),
        'w_gu': nrm((DEPTH, N_EXPERTS, D_MODEL, 2 * D_FF), D_MODEL ** -0.5),
        'b_gu': nrm((DEPTH, N_EXPERTS, 2 * D_FF), 0.01),
        'w_down': nrm((DEPTH, N_EXPERTS, D_FF, D_MODEL), D_FF ** -0.5),
        'b_down': nrm((DEPTH, N_EXPERTS, D_MODEL), 0.01),
        'g_ple': gain(D_MODEL),
        'w_ple_gate': nrm((DEPTH, D_MODEL, D_MODEL), D_MODEL ** -0.5),
        'w_ple_proj': nrm((DEPTH, PLE_DIM, D_MODEL), PLE_DIM ** -0.5),
    }


def reference(x_prompt, x_sample, state_delta, state_conv, cache_k, cache_v, p_prompt, p_sample,
              g_mix, w_in, conv_w, a_log, dt_bias, g_onorm, g_qnorm, g_knorm, rel_bias, w_out,
              g_ffn, w_router, b_router, w_gu, b_gu, w_down, b_down, g_ple, w_ple_gate,
              w_ple_proj):
    layer_w = (g_mix, w_in, conv_w, a_log, dt_bias, g_onorm, g_qnorm, g_knorm, rel_bias, w_out,
               g_ffn, w_router, b_router, w_gu, b_gu, w_down, b_down, g_ple, w_ple_gate,
               w_ple_proj)
    xp, xs = x_prompt, x_sample
    p_delta, p_conv, p_k, p_v = [], [], [], []
    s_delta, s_conv, s_k, s_v = [], [], [], []
    conv0 = jnp.zeros((x_prompt.shape[0], CONV_W - 1, CONV_DIM), x_prompt.dtype)
    delta0 = jnp.zeros((x_prompt.shape[0], H_A, DK_A, DV_A), F32)
    for l in range(DEPTH):
        lw = [w[l] for w in layer_w]
        xp, cb_p, sd_p, k_p, v_p = layer_forward(xp, p_prompt[l], conv0, delta0, attend_prompt, *lw)
        attend_s = functools.partial(attend_sample, cache_k=cache_k[l], cache_v=cache_v[l])
        xs, cb_s, sd_s, k_s, v_s = layer_forward(xs, p_sample[l], state_conv[l], state_delta[l],
                                                 attend_s, *lw)
        p_delta.append(sd_p.astype(x_prompt.dtype))
        p_conv.append(cb_p)
        p_k.append(k_p)
        p_v.append(v_p)
        s_delta.append(sd_s.astype(state_delta.dtype))
        s_conv.append(cb_s)
        s_k.append(k_s)
        s_v.append(v_s)
    return (xp, xs, jnp.stack(p_delta), jnp.stack(p_conv), jnp.stack(p_k), jnp.stack(p_v),
            jnp.stack(s_delta), jnp.stack(s_conv), jnp.stack(s_k), jnp.stack(s_v))
```

```python
import functools

import jax
import jax.numpy as jnp
import numpy as np
from jax import lax
from jax.experimental import pallas as pl
from jax.experimental.pallas import tpu as pltpu

F32 = jnp.float32
BF16 = jnp.bfloat16
HIGHEST = lax.Precision.HIGHEST

D_MODEL = 1024
CHUNK = 64
H_A = 4
DK_A = 128
DV_A = 128
CONV_W = 4
KEY_DIM_A = H_A * DK_A
VAL_DIM_A = H_A * DV_A
CONV_DIM = 2 * KEY_DIM_A + VAL_DIM_A
H_B = 8
HD_B = 64
ATT_DIM_B = H_B * HD_B
N_BACK = 8
BAND_PAST = N_BACK * CHUNK
REL_MAX = 256
N_EXPERTS = 32
TOP_K = 4
D_FF = 1024
SWIGLU_LIMIT = 7.0
SWIGLU_ALPHA = 1.702
PLE_DIM = 256
RMS_EPS = 1e-6
L2_EPS = 1e-6
NEG_BIG = -1e30

LANES = 128
SUBLANES = 8
VMEM_LIMIT = 56 * 1024 * 1024

PROJ_TM = 512
ATT_G = 2
DELTA_NCH = 2
MOE_BM = 512
COMB_TM = 256


def _mm(a, b):
    return jnp.dot(a.astype(BF16), b.astype(BF16), preferred_element_type=F32)


def _mm_nt(a, b):
    return lax.dot_general(a.astype(BF16), b.astype(BF16), (((1,), (1,)), ((), ())),
                           preferred_element_type=F32)


def _mm_f32(a, b):
    return jnp.dot(a, b, precision=HIGHEST, preferred_element_type=F32)


def _mm_nt_f32(a, b):
    return lax.dot_general(a, b, (((1,), (1,)), ((), ())), precision=HIGHEST,
                           preferred_element_type=F32)


def _sigmoid(x):
    return 1.0 / (1.0 + jnp.exp(-x))


def _softplus(x):
    return jnp.maximum(x, 0.0) + jnp.log(1.0 + jnp.exp(-jnp.abs(x)))


def _rms(x, g):
    ms = jnp.mean(x * x, axis=-1, keepdims=True)
    return x * lax.rsqrt(ms + RMS_EPS) * g


def _cparams(sem):
    return pltpu.CompilerParams(dimension_semantics=sem, vmem_limit_bytes=VMEM_LIMIT)


def _proj_kernel(x_ref, gmix_ref, wa_ref, wba_ref, wb_ref, bd_ref, gq_ref, gk_ref,
                 qkvz_ref, ba_ref, qkvb_ref):
    h = _rms(x_ref[...], gmix_ref[...]).astype(BF16)
    qkvz_ref[...] = jnp.dot(h, wa_ref[...], preferred_element_type=F32)
    ba_ref[...] = jnp.dot(h, wba_ref[...], preferred_element_type=F32)
    pb = jnp.dot(h, wb_ref[...], preferred_element_type=F32)
    bd = bd_ref[...]

    def head_norm(q, g):
        sq = q * q
        hi = sq.astype(BF16)
        lo = (sq - hi.astype(F32)).astype(BF16)
        ss = (jnp.dot(hi, bd, preferred_element_type=F32)
              + jnp.dot(lo, bd, preferred_element_type=F32))
        return q * lax.rsqrt(ss * (1.0 / HD_B) + RMS_EPS) * g

    qkvb_ref[:, 0:ATT_DIM_B] = head_norm(pb[:, 0:ATT_DIM_B], gq_ref[...])
    qkvb_ref[:, ATT_DIM_B:2 * ATT_DIM_B] = head_norm(pb[:, ATT_DIM_B:2 * ATT_DIM_B], gk_ref[...])
    qkvb_ref[:, 2 * ATT_DIM_B:] = pb[:, 2 * ATT_DIM_B:]


def _project(x2d, gmix, wa, wba, wb, bd, gq, gk):
    t = x2d.shape[0]
    tm = min(PROJ_TM, t)
    const = lambda i: (0, 0)
    return pl.pallas_call(
        _proj_kernel,
        grid=(t // tm,),
        in_specs=[
            pl.BlockSpec((tm, D_MODEL), lambda i: (i, 0)),
            pl.BlockSpec(gmix.shape, const),
            pl.BlockSpec(wa.shape, const),
            pl.BlockSpec(wba.shape, const),
            pl.BlockSpec(wb.shape, const),
            pl.BlockSpec(bd.shape, const),
            pl.BlockSpec(gq.shape, const),
            pl.BlockSpec(gk.shape, const),
        ],
        out_specs=[
            pl.BlockSpec((tm, wa.shape[1]), lambda i: (i, 0)),
            pl.BlockSpec((tm, LANES), lambda i: (i, 0)),
            pl.BlockSpec((tm, wb.shape[1]), lambda i: (i, 0)),
        ],
        out_shape=[
            jax.ShapeDtypeStruct((t, wa.shape[1]), F32),
            jax.ShapeDtypeStruct((t, LANES), F32),
            jax.ShapeDtypeStruct((t, wb.shape[1]), F32),
        ],
        compiler_params=_cparams(("parallel",)),
        name="proj",
    )(x2d, gmix, wa, wba, wb, bd, gq, gk)


def _delta_kernel(qkvz_ref, ba_ref, cbuf_ref, s0_ref, convw_ref, rep_ref, nega_ref, dtb_ref,
                  gon_ref, ltri_ref, ident_ref,
                  o_ref, sfin_ref, cnew_ref, xh_ref, s_ref, *, c, nch):
    step = pl.program_id(1)
    tile = c * nch
    hist = SUBLANES

    @pl.when(step == 0)
    def _():
        xh_ref[0:hist, :] = cbuf_ref[...]
        s_ref[...] = s0_ref[...]

    xh_ref[hist:hist + tile, :] = qkvz_ref[:, 0:CONV_DIM]
    w = convw_ref[...]
    conv = (xh_ref[hist - 3:hist - 3 + tile, :] * w[0:1, :]
            + xh_ref[hist - 2:hist - 2 + tile, :] * w[1:2, :]
            + xh_ref[hist - 1:hist - 1 + tile, :] * w[2:3, :]
            + xh_ref[hist:hist + tile, :] * w[3:4, :])
    conv = conv * _sigmoid(conv)
    last_rows = xh_ref[tile:tile + hist, :]
    cnew_ref[...] = last_rows
    xh_ref[0:hist, :] = last_rows

    rep = _mm_f32(ba_ref[...], rep_ref[...])
    beta_all = _sigmoid(rep[:, 0:KEY_DIM_A])
    g_all = nega_ref[...] * _softplus(rep[:, KEY_DIM_A:] + dtb_ref[...])
    z_all = qkvz_ref[:, CONV_DIM:CONV_DIM + VAL_DIM_A]

    ltri = ltri_ref[...]
    ident = ident_ref[...]
    row = lax.broadcasted_iota(jnp.int32, (c, c), 0)
    col = lax.broadcasted_iota(jnp.int32, (c, c), 1)
    causal = row >= col
    strict = row > col
    avg = jnp.full((c, LANES), 1.0 / LANES, F32)
    nsq = int(np.log2(c))

    pre = []
    for ci in range(nch):
        r0 = ci * c
        gc = _mm_f32(ltri, g_all[r0:r0 + c, :])
        egc = jnp.exp(gc)
        glast = gc[c - 1:c, :]
        eglast = jnp.exp(glast)
        kdec_scale = jnp.exp(glast - gc)
        heads = []
        for h in range(H_A):
            sl = slice(h * DK_A, (h + 1) * DK_A)
            qh = conv[r0:r0 + c, h * DK_A:(h + 1) * DK_A]
            kh = conv[r0:r0 + c, KEY_DIM_A + h * DK_A:KEY_DIM_A + (h + 1) * DK_A]
            vh = conv[r0:r0 + c, 2 * KEY_DIM_A + h * DV_A:2 * KEY_DIM_A + (h + 1) * DV_A]
            qh = qh * lax.rsqrt(jnp.sum(qh * qh, axis=-1, keepdims=True) + L2_EPS) * (DK_A ** -0.5)
            kh = kh * lax.rsqrt(jnp.sum(kh * kh, axis=-1, keepdims=True) + L2_EPS)
            bh = beta_all[r0:r0 + c, sl]
            gch = gc[:, sl]
            grow = _mm_nt_f32(avg, gch)
            gcol = gch[:, 0:c]
            decay = jnp.exp(jnp.where(causal, gcol - grow, -jnp.inf))
            kb = kh * bh
            a_mat = jnp.where(strict, _mm_nt(kb, kh) * decay, 0.0)
            x = jnp.concatenate([vh * bh, kb * egc[:, sl]], axis=-1)
            p = -a_mat
            for s in range(nsq):
                x = x + _mm(p, x)
                if s + 1 < nsq:
                    p = _mm(p, p)
            u0 = x[:, 0:DV_A]
            wmat = x[:, DV_A:]
            qk = _mm_nt(qh, kh) * decay
            q_dec = qh * egc[:, sl]
            k_dec = kh * kdec_scale[:, sl]
            k_dec_t = _mm_nt(ident, k_dec)
            heads.append((u0, wmat, qk, q_dec, k_dec_t, eglast[:, sl]))
        pre.append(heads)

    for ci in range(nch):
        r0 = ci * c
        for h in range(H_A):
            u0, wmat, qk, q_dec, k_dec_t, egl = pre[ci][h]
            s_h = s_ref[h]
            u = u0 - _mm(wmat, s_h)
            o = _mm(q_dec, s_h) + _mm(qk, u)
            s_ref[h] = s_h * egl + _mm(k_dec_t, u)
            z = z_all[r0:r0 + c, h * DV_A:(h + 1) * DV_A]
            o_ref[r0:r0 + c, h * DV_A:(h + 1) * DV_A] = _rms(o, gon_ref[...]) * (z * _sigmoid(z))

    sfin_ref[...] = s_ref[...]


def _delta_mixer(qkvz, ba, cbuf, s0, convw, rep, nega, dtb, gon):
    b, t, _ = qkvz.shape
    c = min(CHUNK, t)
    nch = min(DELTA_NCH, t // c)
    tile = c * nch
    ltri = jnp.tril(jnp.ones((c, c), F32))
    ident = jnp.eye(DK_A, dtype=BF16)
    const2 = lambda i, j: (0, 0)
    kern = functools.partial(_delta_kernel, c=c, nch=nch)
    return pl.pallas_call(
        kern,
        grid=(b, t // tile),
        in_specs=[
            pl.BlockSpec((None, tile, qkvz.shape[2]), lambda i, j: (i, j, 0)),
            pl.BlockSpec((None, tile, LANES), lambda i, j: (i, j, 0)),
            pl.BlockSpec((None, SUBLANES, CONV_DIM), lambda i, j: (i, 0, 0)),
            pl.BlockSpec((None, H_A, DK_A, DV_A), lambda i, j: (i, 0, 0, 0)),
            pl.BlockSpec(convw.shape, const2),
            pl.BlockSpec(rep.shape, const2),
            pl.BlockSpec(nega.shape, const2),
            pl.BlockSpec(dtb.shape, const2),
            pl.BlockSpec(gon.shape, const2),
            pl.BlockSpec(ltri.shape, const2),
            pl.BlockSpec(ident.shape, const2),
        ],
        out_specs=[
            pl.BlockSpec((None, tile, VAL_DIM_A), lambda i, j: (i, j, 0)),
            pl.BlockSpec((None, H_A, DK_A, DV_A), lambda i, j: (i, 0, 0, 0)),
            pl.BlockSpec((None, SUBLANES, CONV_DIM), lambda i, j: (i, 0, 0)),
        ],
        out_shape=[
            jax.ShapeDtypeStruct((b, t, VAL_DIM_A), F32),
            jax.ShapeDtypeStruct((b, H_A, DK_A, DV_A), F32),
            jax.ShapeDtypeStruct((b, SUBLANES, CONV_DIM), F32),
        ],
        scratch_shapes=[
            pltpu.VMEM((SUBLANES + tile, CONV_DIM), F32),
            pltpu.VMEM((H_A, DK_A, DV_A), F32),
        ],
        compiler_params=_cparams(("parallel", "arbitrary")),
        name="delta",
    )(qkvz, ba, cbuf, s0, convw, rep, nega, dtb, gon, ltri, ident)


def _pair_queries(qp):
    lane = lax.broadcasted_iota(jnp.int32, qp.shape, 1)
    q_even = jnp.where(lane < HD_B, qp, 0.0)
    q_odd = jnp.where(lane >= HD_B, qp, 0.0)
    return jnp.concatenate([q_even, q_odd], axis=0).astype(BF16)


def _unpair(o, r):
    lane = lax.broadcasted_iota(jnp.int32, (r, LANES), 1)
    return jnp.where(lane < HD_B, o[0:r, :], o[r:2 * r, :])


def _attn_prompt_kernel(q_ref, k_ref, v_ref, bias_ref, o_ref, kwin_ref, vwin_ref, *, g):
    cstep = pl.program_id(1)
    rows = g * CHUNK
    win = (N_BACK + g) * CHUNK

    @pl.when(cstep == 0)
    def _():
        kwin_ref[...] = jnp.zeros(kwin_ref.shape, BF16)
        vwin_ref[...] = jnp.zeros(vwin_ref.shape, BF16)

    for i in range(N_BACK // g):
        kwin_ref[i * rows:(i + 1) * rows, :] = kwin_ref[(i + 1) * rows:(i + 2) * rows, :]
        vwin_ref[i * rows:(i + 1) * rows, :] = vwin_ref[(i + 1) * rows:(i + 2) * rows, :]
    kwin_ref[win - rows:win, :] = k_ref[...].astype(BF16)
    vwin_ref[win - rows:win, :] = v_ref[...].astype(BF16)

    slot_chunk = lax.broadcasted_iota(jnp.int32, (2 * rows, win), 1) // CHUNK
    in_seq = slot_chunk >= N_BACK - cstep * g
    for p in range(H_B // 2):
        sl = slice(p * LANES, (p + 1) * LANES)
        q2 = _pair_queries(q_ref[:, sl] * (HD_B ** -0.5))
        s = _mm_nt(q2, kwin_ref[:, sl]) + bias_ref[p]
        s = jnp.where(in_seq, s, NEG_BIG)
        m = jnp.max(s, axis=-1, keepdims=True)
        e = jnp.exp(s - m)
        l = jnp.sum(e, axis=-1, keepdims=True)
        o = jnp.dot(e.astype(BF16), vwin_ref[:, sl], preferred_element_type=F32) / l
        o_ref[:, sl] = _unpair(o, rows)


def _attend_prompt(qkvb, bias):
    b, t, _ = qkvb.shape
    g = ATT_G
    rows = g * CHUNK
    win = (N_BACK + g) * CHUNK
    kern = functools.partial(_attn_prompt_kernel, g=g)
    return pl.pallas_call(
        kern,
        grid=(b, t // rows),
        in_specs=[
            pl.BlockSpec((None, rows, ATT_DIM_B), lambda i, j: (i, j, 0)),
            pl.BlockSpec((None, rows, ATT_DIM_B), lambda i, j: (i, j, 1)),
            pl.BlockSpec((None, rows, ATT_DIM_B), lambda i, j: (i, j, 2)),
            pl.BlockSpec(bias.shape, lambda i, j: (0, 0, 0)),
        ],
        out_specs=pl.BlockSpec((None, rows, ATT_DIM_B), lambda i, j: (i, j, 0)),
        out_shape=jax.ShapeDtypeStruct((b, t, ATT_DIM_B), F32),
        scratch_shapes=[pltpu.VMEM((win, ATT_DIM_B), BF16), pltpu.VMEM((win, ATT_DIM_B), BF16)],
        compiler_params=_cparams(("parallel", "arbitrary")),
        name="attn_prompt",
    )(qkvb, qkvb, qkvb, bias)


def _attn_sample_kernel(q_ref, kn_ref, vn_ref, kc_ref, vc_ref, biasc_ref, biasn_ref, o_ref):
    ds = q_ref.shape[0]
    for p in range(H_B // 2):
        sl = slice(p * LANES, (p + 1) * LANES)
        q2 = _pair_queries(q_ref[:, sl] * (HD_B ** -0.5))
        s1 = _mm_nt(q2, kc_ref[:, sl]) + biasc_ref[p]
        s2 = _mm_nt(q2, kn_ref[:, sl]) + biasn_ref[p]
        m = jnp.maximum(jnp.max(s1, axis=-1, keepdims=True), jnp.max(s2, axis=-1, keepdims=True))
        e1 = jnp.exp(s1 - m)
        e2 = jnp.exp(s2 - m)
        l = jnp.sum(e1, axis=-1, keepdims=True) + jnp.sum(e2, axis=-1, keepdims=True)
        o = (_mm(e1, vc_ref[:, sl]) + _mm(e2, vn_ref[:, sl])) / l
        o_ref[:, sl] = _unpair(o, ds)


def _attend_sample(qkvb, cache_k, cache_v, biasc, biasn):
    b, ds, _ = qkvb.shape
    past = cache_k.shape[1]
    return pl.pallas_call(
        _attn_sample_kernel,
        grid=(b,),
        in_specs=[
            pl.BlockSpec((None, ds, ATT_DIM_B), lambda i: (i, 0, 0)),
            pl.BlockSpec((None, ds, ATT_DIM_B), lambda i: (i, 0, 1)),
            pl.BlockSpec((None, ds, ATT_DIM_B), lambda i: (i, 0, 2)),
            pl.BlockSpec((None, past, ATT_DIM_B), lambda i: (i, 0, 0)),
            pl.BlockSpec((None, past, ATT_DIM_B), lambda i: (i, 0, 0)),
            pl.BlockSpec(biasc.shape, lambda i: (0, 0, 0)),
            pl.BlockSpec(biasn.shape, lambda i: (0, 0, 0)),
        ],
        out_specs=pl.BlockSpec((None, ds, ATT_DIM_B), lambda i: (i, 0, 0)),
        out_shape=jax.ShapeDtypeStruct((b, ds, ATT_DIM_B), F32),
        compiler_params=_cparams(("parallel",)),
        name="attn_sample",
    )(qkvb, qkvb, qkvb, cache_k, cache_v, biasc, biasn)


def _router_kernel(oap_ref, obp_ref, xp_ref, oas_ref, obs_ref, xs_ref,
                   woa_ref, wob_ref, gffn_ref, wrt_ref, brt_ref, utri_ref,
                   x1_ref, h2_ref, idx_ref, gate_ref, rank_ref, cnt_ref,
                   oa_s, ob_s, x_s, carry_s, *, ntp):
    i = pl.program_id(0)
    tm = x_s.shape[0]

    @pl.when(i == 0)
    def _():
        carry_s[...] = jnp.zeros(carry_s.shape, F32)

    @pl.when(i < ntp)
    def _():
        oa_s[...] = oap_ref[...]
        ob_s[...] = obp_ref[...]
        x_s[...] = xp_ref[...]

    @pl.when(i >= ntp)
    def _():
        oa_s[...] = oas_ref[...]
        ob_s[...] = obs_ref[...]
        x_s[...] = xs_ref[...]

    x1 = x_s[...] + _mm(oa_s[...], woa_ref[...]) + _mm(ob_s[...], wob_ref[...])
    x1_ref[...] = x1
    h2 = _rms(x1, gffn_ref[...])
    h2_ref[...] = h2
    logits = _mm_nt(wrt_ref[...], h2) + brt_ref[:, 0:1]

    eidx = lax.broadcasted_iota(jnp.int32, (N_EXPERTS, tm), 0).astype(F32)
    vals = logits
    memf = jnp.zeros((N_EXPERTS, tm), F32)
    tops, sels = [], []
    for k in range(TOP_K):
        m = jnp.max(vals, axis=0, keepdims=True)
        ix = jnp.min(jnp.where(vals == m, eidx, float(N_EXPERTS)), axis=0, keepdims=True)
        sel = eidx == ix
        tops.append(m)
        sels.append(sel)
        idx_ref[k:k + 1, :] = ix.astype(jnp.int32)
        vals = jnp.where(sel, -jnp.inf, vals)
        memf = memf + jnp.where(sel, 1.0, 0.0)
    es = [jnp.exp(v - tops[0]) for v in tops]
    denom = es[0] + es[1] + es[2] + es[3]
    for k in range(TOP_K):
        gate_ref[k:k + 1, :] = es[k] / denom

    carry = carry_s[:, 0:1]
    excl = _mm(memf, utri_ref[...]) + carry
    for k in range(TOP_K):
        r = jnp.sum(jnp.where(sels[k], excl, 0.0), axis=0, keepdims=True)
        rank_ref[k:k + 1, :] = r.astype(jnp.int32)
    new_carry = carry + jnp.sum(memf, axis=1, keepdims=True)
    carry_s[...] = jnp.broadcast_to(new_carry, carry_s.shape)
    cnt_ref[...] = jnp.broadcast_to(new_carry, cnt_ref.shape)


def _out_router(oa_p, ob_p, x_p, oa_s, ob_s, x_s, woa, wob, gffn, wrt, brt):
    tp, ts = x_p.shape[0], x_s.shape[0]
    tm = min(PROJ_TM, ts, tp)
    ntp, nts = tp // tm, ts // tm
    ttot = tp + ts
    utri = jnp.triu(jnp.ones((tm, tm), BF16), k=1)
    pmap = lambda i: (jnp.minimum(i, ntp - 1), 0)
    smap = lambda i: (jnp.maximum(i - ntp, 0), 0)
    const = lambda i: (0, 0)
    kern = functools.partial(_router_kernel, ntp=ntp)
    return pl.pallas_call(
        kern,
        grid=(ntp + nts,),
        in_specs=[
            pl.BlockSpec((tm, VAL_DIM_A), pmap),
            pl.BlockSpec((tm, ATT_DIM_B), pmap),
            pl.BlockSpec((tm, D_MODEL), pmap),
            pl.BlockSpec((tm, VAL_DIM_A), smap),
            pl.BlockSpec((tm, ATT_DIM_B), smap),
            pl.BlockSpec((tm, D_MODEL), smap),
            pl.BlockSpec(woa.shape, const),
            pl.BlockSpec(wob.shape, const),
            pl.BlockSpec(gffn.shape, const),
            pl.BlockSpec(wrt.shape, const),
            pl.BlockSpec(brt.shape, const),
            pl.BlockSpec(utri.shape, const),
        ],
        out_specs=[
            pl.BlockSpec((tm, D_MODEL), lambda i: (i, 0)),
            pl.BlockSpec((tm, D_MODEL), lambda i: (i, 0)),
            pl.BlockSpec((TOP_K, tm), lambda i: (0, i)),
            pl.BlockSpec((TOP_K, tm), lambda i: (0, i)),
            pl.BlockSpec((TOP_K, tm), lambda i: (0, i)),
            pl.BlockSpec((N_EXPERTS, LANES), const),
        ],
        out_shape=[
            jax.ShapeDtypeStruct((ttot, D_MODEL), F32),
            jax.ShapeDtypeStruct((ttot, D_MODEL), F32),
            jax.ShapeDtypeStruct((TOP_K, ttot), jnp.int32),
            jax.ShapeDtypeStruct((TOP_K, ttot), F32),
            jax.ShapeDtypeStruct((TOP_K, ttot), jnp.int32),
            jax.ShapeDtypeStruct((N_EXPERTS, LANES), F32),
        ],
        scratch_shapes=[
            pltpu.VMEM((tm, VAL_DIM_A), F32),
            pltpu.VMEM((tm, ATT_DIM_B), F32),
            pltpu.VMEM((tm, D_MODEL), F32),
            pltpu.VMEM((N_EXPERTS, LANES), F32),
        ],
        compiler_params=_cparams(("arbitrary",)),
        name="out_router",
    )(oa_p, ob_p, x_p, oa_s, ob_s, x_s, woa, wob, gffn, wrt, brt, utri)


def _row_copy(src_ref, s, dst_ref, d, sem):
    return pltpu.make_async_copy(src_ref.at[pl.ds(s, 1), :], dst_ref.at[pl.ds(d, 1), :], sem)


def _dispatch_kernel(padlo_ref, padhi_ref, dest_ref, h2_ref, xs_ref, zrow_ref, sem):
    i = pl.program_id(0)
    tm = h2_ref.shape[0]

    def issue(t, carry):
        for k in range(TOP_K):
            _row_copy(h2_ref, t, xs_ref, dest_ref[k, t], sem).start()
        return carry

    lax.fori_loop(0, tm, issue, 0)

    def drain(t, carry):
        for k in range(TOP_K):
            _row_copy(h2_ref, 0, xs_ref, 0, sem).wait()
        return carry

    lax.fori_loop(0, tm, drain, 0)

    @pl.when(i == pl.num_programs(0) - 1)
    def _():
        zrow_ref[...] = jnp.zeros(zrow_ref.shape, F32)
        for e in range(N_EXPERTS):
            lo = padlo_ref[e]
            hi = padhi_ref[e]

            def zissue(r, carry):
                _row_copy(zrow_ref, 0, xs_ref, r, sem).start()
                return carry

            lax.fori_loop(lo, hi, zissue, 0)

            def zdrain(r, carry):
                _row_copy(zrow_ref, 0, xs_ref, 0, sem).wait()
                return carry

            lax.fori_loop(lo, hi, zdrain, 0)


def _dispatch(padlo, padhi, dest, h2, n_rows, tm_rows):
    ttot = h2.shape[0]
    tm = tm_rows
    grid_spec = pltpu.PrefetchScalarGridSpec(
        num_scalar_prefetch=2,
        grid=(ttot // tm,),
        in_specs=[
            pl.BlockSpec((TOP_K, tm), lambda i, lo, hi: (0, i), memory_space=pltpu.SMEM),
            pl.BlockSpec((tm, D_MODEL), lambda i, lo, hi: (i, 0)),
        ],
        out_specs=pl.BlockSpec(memory_space=pl.ANY),
        scratch_shapes=[pltpu.VMEM((SUBLANES, D_MODEL), F32), pltpu.SemaphoreType.DMA(())],
    )
    return pl.pallas_call(
        _dispatch_kernel,
        grid_spec=grid_spec,
        out_shape=jax.ShapeDtypeStruct((n_rows, D_MODEL), F32),
        compiler_params=_cparams(("arbitrary",)),
        name="dispatch",
    )(padlo, padhi, dest, h2)


def _expert_kernel(bexp_ref, nused_ref, x_ref, wgu_ref, bgu_ref, wd_ref, bd_ref, y_ref):
    j = pl.program_id(0)

    @pl.when(j < nused_ref[0])
    def _():
        gu = _mm(x_ref[...], wgu_ref[...]) + bgu_ref[...]
        gate = jnp.minimum(gu[:, 0:D_FF], SWIGLU_LIMIT)
        up = jnp.clip(gu[:, D_FF:], -SWIGLU_LIMIT, SWIGLU_LIMIT)
        hid = (up + 1.0) * (gate * _sigmoid(gate * SWIGLU_ALPHA))
        y_ref[...] = _mm(hid, wd_ref[...]) + bd_ref[...]


def _experts(bexp, nused, xs, wgu, bgu, wd, bd):
    n_rows = xs.shape[0]
    bm = MOE_BM
    nb = n_rows // bm
    row_map = lambda j, be, nu: (jnp.minimum(j, nu[0] - 1), 0)
    w_map = lambda j, be, nu: (be[j], 0, 0)
    grid_spec = pltpu.PrefetchScalarGridSpec(
        num_scalar_prefetch=2,
        grid=(nb,),
        in_specs=[
            pl.BlockSpec((bm, D_MODEL), row_map),
            pl.BlockSpec((None, D_MODEL, 2 * D_FF), w_map),
            pl.BlockSpec((None, 1, 2 * D_FF), w_map),
            pl.BlockSpec((None, D_FF, D_MODEL), w_map),
            pl.BlockSpec((None, 1, D_MODEL), w_map),
        ],
        out_specs=pl.BlockSpec((bm, D_MODEL), row_map),
    )
    return pl.pallas_call(
        _expert_kernel,
        grid_spec=grid_spec,
        out_shape=jax.ShapeDtypeStruct((n_rows, D_MODEL), F32),
        compiler_params=_cparams(("arbitrary",)),
        name="experts",
    )(bexp, nused, xs, wgu, bgu, wd, bd)


def _combine_kernel(dest_ref, gate_ref, x1_ref, pp_ref, ps_ref, yb_ref, gple_ref, wg_ref, wp_ref,
                    yp_ref, ys_ref, buf_ref, sem, *, ntp):
    i = pl.program_id(0)
    tm = x1_ref.shape[0]

    def issue(t, carry):
        for k in range(TOP_K):
            pltpu.make_async_copy(yb_ref.at[pl.ds(dest_ref[k, t], 1), :],
                                  buf_ref.at[k, pl.ds(t, 1), :], sem).start()
        return carry

    lax.fori_loop(0, tm, issue, 0)

    def drain(t, carry):
        for k in range(TOP_K):
            pltpu.make_async_copy(yb_ref.at[pl.ds(0, 1), :], buf_ref.at[0, pl.ds(0, 1), :], sem).wait()
        return carry

    lax.fori_loop(0, tm, drain, 0)

    gates = gate_ref[...]
    moe = buf_ref[0] * gates[:, 0:1]
    for k in range(1, TOP_K):
        moe = moe + buf_ref[k] * gates[:, k:k + 1]
    x2 = x1_ref[...] + moe
    gate = _sigmoid(_mm(_rms(x2, gple_ref[...]), wg_ref[...]))

    @pl.when(i < ntp)
    def _():
        yp_ref[...] = x2 + gate * _mm(pp_ref[...], wp_ref[...])

    @pl.when(i >= ntp)
    def _():
        ys_ref[...] = x2 + gate * _mm(ps_ref[...], wp_ref[...])


def _combine(dest, gates_col, x1, p_p, p_s, yb, gple, wg, wp):
    tp, ts = p_p.shape[0], p_s.shape[0]
    tm = min(COMB_TM, tp, ts)
    ntp, nts = tp // tm, ts // tm
    pmap = lambda i: (jnp.minimum(i, ntp - 1), 0)
    smap = lambda i: (jnp.maximum(i - ntp, 0), 0)
    const = lambda i: (0, 0)
    kern = functools.partial(_combine_kernel, ntp=ntp)
    return pl.pallas_call(
        kern,
        grid=(ntp + nts,),
        in_specs=[
            pl.BlockSpec((TOP_K, tm), lambda i: (0, i), memory_space=pltpu.SMEM),
            pl.BlockSpec((tm, TOP_K), lambda i: (i, 0)),
            pl.BlockSpec((tm, D_MODEL), lambda i: (i, 0)),
            pl.BlockSpec((tm, PLE_DIM), pmap),
            pl.BlockSpec((tm, PLE_DIM), smap),
            pl.BlockSpec(memory_space=pl.ANY),
            pl.BlockSpec(gple.shape, const),
            pl.BlockSpec(wg.shape, const),
            pl.BlockSpec(wp.shape, const),
        ],
        out_specs=[
            pl.BlockSpec((tm, D_MODEL), pmap),
            pl.BlockSpec((tm, D_MODEL), smap),
        ],
        out_shape=[
            jax.ShapeDtypeStruct((tp, D_MODEL), F32),
            jax.ShapeDtypeStruct((ts, D_MODEL), F32),
        ],
        scratch_shapes=[pltpu.VMEM((TOP_K, tm, D_MODEL), F32), pltpu.SemaphoreType.DMA(())],
        compiler_params=_cparams(("arbitrary",)),
        name="combine",
    )(dest, gates_col, x1, p_p, p_s, yb, gple, wg, wp)


def _bias_tables(rel_bias, n_q, n_k, key_offset, band):
    i = np.arange(n_q)[:, None]
    j = np.arange(n_k)[None, :]
    rel = np.clip(i - j + key_offset, -REL_MAX, REL_MAX) + REL_MAX
    tab = rel_bias[:, rel]
    if band:
        qc = i // CHUNK
        sc = j // CHUNK
        ok = (sc >= qc) & (sc <= qc + N_BACK)
        tab = jnp.where(jnp.asarray(ok)[None], tab, NEG_BIG)
    return tab.reshape(H_B // 2, 2 * n_q, n_k)


def kernel(x_prompt, x_sample, state_delta, state_conv, cache_k, cache_v, p_prompt, p_sample, g_mix, w_in, conv_w, a_log, dt_bias, g_onorm, g_qnorm, g_knorm, rel_bias, w_out, g_ffn, w_router, b_router, w_gu, b_gu, w_down, b_down, g_ple, w_ple_gate, w_ple_proj):
    assert w_in.shape[0] == 1, "single-layer kernel"
    bp, tp_len, _ = x_prompt.shape
    bs, ts_len, _ = x_sample.shape
    tp, ts = bp * tp_len, bs * ts_len
    past = cache_k.shape[2]

    w = w_in[0]
    o_z = CONV_DIM + VAL_DIM_A
    o_qb = o_z + 2 * H_A
    wa = w[:, 0:o_z].astype(BF16)
    wba = jnp.pad(w[:, o_z:o_qb], ((0, 0), (0, LANES - 2 * H_A))).astype(BF16)
    wb = w[:, o_qb:].astype(BF16)
    grp = np.arange(ATT_DIM_B) // HD_B
    bd = jnp.asarray(grp[:, None] == grp[None, :], BF16)
    gq = jnp.tile(g_qnorm[0], H_B)[None, :]
    gk = jnp.tile(g_knorm[0], H_B)[None, :]
    gmix = g_mix[0][None, :]
    lane_head = np.arange(KEY_DIM_A) // DK_A
    rep = np.zeros((LANES, 2 * KEY_DIM_A), np.float32)
    rep[lane_head, np.arange(KEY_DIM_A)] = 1.0
    rep[H_A + lane_head, KEY_DIM_A + np.arange(KEY_DIM_A)] = 1.0
    rep = jnp.asarray(rep)
    nega = jnp.repeat(-jnp.exp(a_log[0]), DK_A)[None, :]
    dtb = jnp.repeat(dt_bias[0], DK_A)[None, :]
    gon = g_onorm[0][None, :]
    convw = conv_w[0]
    woa = w_out[0, 0:VAL_DIM_A].astype(BF16)
    wob = w_out[0, VAL_DIM_A:].astype(BF16)
    gffn = g_ffn[0][None, :]
    wrt = w_router[0].T.astype(BF16)
    brt = jnp.broadcast_to(b_router[0][:, None], (N_EXPERTS, LANES))
    wgu = w_gu[0].astype(BF16)
    bgu = b_gu[0][:, None, :]
    wd = w_down[0].astype(BF16)
    bdn = b_down[0][:, None, :]
    gple = g_ple[0][None, :]
    wg = w_ple_gate[0].astype(BF16)
    wp = w_ple_proj[0].astype(BF16)

    xp2 = x_prompt.reshape(tp, D_MODEL)
    xs2 = x_sample.reshape(ts, D_MODEL)

    qkvz_p, ba_p, qkvb_p = _project(xp2, gmix, wa, wba, wb, bd, gq, gk)
    qkvz_s, ba_s, qkvb_s = _project(xs2, gmix, wa, wba, wb, bd, gq, gk)

    hist_pad = ((0, 0), (SUBLANES - (CONV_W - 1), 0), (0, 0))
    cbuf_p = jnp.zeros((bp, SUBLANES, CONV_DIM), F32)
    cbuf_s = jnp.pad(state_conv[0], hist_pad)
    s0_p = jnp.zeros((bp, H_A, DK_A, DV_A), F32)
    oa_p, sfin_p, cnew_p = _delta_mixer(qkvz_p.reshape(bp, tp_len, -1), ba_p.reshape(bp, tp_len, LANES),
                                        cbuf_p, s0_p, convw, rep, nega, dtb, gon)
    oa_s, sfin_s, cnew_s = _delta_mixer(qkvz_s.reshape(bs, ts_len, -1), ba_s.reshape(bs, ts_len, LANES),
                                        cbuf_s, state_delta[0], convw, rep, nega, dtb, gon)

    qkvb_p3 = qkvb_p.reshape(bp, tp_len, 3 * ATT_DIM_B)
    qkvb_s3 = qkvb_s.reshape(bs, ts_len, 3 * ATT_DIM_B)
    bias_p = _bias_tables(rel_bias[0], ATT_G * CHUNK, (N_BACK + ATT_G) * CHUNK, BAND_PAST, True)
    ob_p = _attend_prompt(qkvb_p3, bias_p)
    bias_s = _bias_tables(rel_bias[0], ts_len, past + ts_len, past, False)
    ob_s = _attend_sample(qkvb_s3, cache_k[0].reshape(bs, past, ATT_DIM_B),
                          cache_v[0].reshape(bs, past, ATT_DIM_B),
                          bias_s[:, :, 0:past], bias_s[:, :, past:])

    x1, h2, idx, gates, rank, cnt = _out_router(
        oa_p.reshape(tp, VAL_DIM_A), ob_p.reshape(tp, ATT_DIM_B), xp2,
        oa_s.reshape(ts, VAL_DIM_A), ob_s.reshape(ts, ATT_DIM_B), xs2,
        woa, wob, gffn, wrt, brt)
    ttot = tp + ts
    bm = MOE_BM
    counts = cnt[:, 0].astype(jnp.int32)
    padded = (counts + bm - 1) // bm * bm
    pend = jnp.cumsum(padded)
    pstart = pend - padded
    dest = pstart[idx] + rank
    nb = -(-(ttot * TOP_K) // bm) + N_EXPERTS
    nused = (pend[-1] // bm).astype(jnp.int32)[None]
    bexp = jnp.minimum(jnp.searchsorted(pend, jnp.arange(nb, dtype=jnp.int32) * bm, side="right"),
                       N_EXPERTS - 1).astype(jnp.int32)
    bexp = jnp.where(jnp.arange(nb) < nused[0], bexp, bexp[jnp.maximum(nused[0] - 1, 0)])

    xs_rows = _dispatch(pstart + counts, pend, dest, h2, nb * bm, min(COMB_TM, tp, ts))
    yb = _experts(bexp, nused, xs_rows, wgu, bgu, wd, bdn)
    y_p, y_s = _combine(dest, gates.T, x1, p_prompt[0].reshape(tp, PLE_DIM),
                        p_sample[0].reshape(ts, PLE_DIM), yb, gple, wg, wp)

    keep = min(BAND_PAST, tp_len)
    k_p = qkvb_p3[:, tp_len - keep:, ATT_DIM_B:2 * ATT_DIM_B].reshape(1, bp, keep, H_B, HD_B)
    v_p = qkvb_p3[:, tp_len - keep:, 2 * ATT_DIM_B:].reshape(1, bp, keep, H_B, HD_B)
    k_s = qkvb_s3[:, :, ATT_DIM_B:2 * ATT_DIM_B].reshape(1, bs, ts_len, H_B, HD_B)
    v_s = qkvb_s3[:, :, 2 * ATT_DIM_B:].reshape(1, bs, ts_len, H_B, HD_B)
    nconv = CONV_W - 1
    return (y_p.reshape(bp, tp_len, D_MODEL), y_s.reshape(bs, ts_len, D_MODEL),
            sfin_p[None], cnew_p[None, :, SUBLANES - nconv:, :], k_p, v_p,
            sfin_s[None], cnew_s[None, :, SUBLANES - nconv:, :], k_s, v_s)
```

```python
import functools

import jax
import jax.numpy as jnp
import numpy as np
from jax import lax
from jax.experimental import pallas as pl
from jax.experimental.pallas import tpu as pltpu

F32 = jnp.float32
BF16 = jnp.bfloat16

D_MODEL = 1024
CHUNK = 64
H_A = 4
DK_A = 128
DV_A = 128
CONV_W = 4
KEY_DIM_A = H_A * DK_A
VAL_DIM_A = H_A * DV_A
CONV_DIM = 2 * KEY_DIM_A + VAL_DIM_A
H_B = 8
HD_B = 64
ATT_DIM_B = H_B * HD_B
N_BACK = 8
BAND_PAST = N_BACK * CHUNK
REL_MAX = 256
N_EXPERTS = 32
TOP_K = 4
D_FF = 1024
SWIGLU_LIMIT = 7.0
SWIGLU_ALPHA = 1.702
PLE_DIM = 256
RMS_EPS = 1e-6
L2_EPS = 1e-6
NEG_BIG = -1e30

LANES = 128
SUBLANES = 8
VMEM_LIMIT = 56 * 1024 * 1024

PROJ_TM = 512
ATT_G = 2
DELTA_NCH = 4
MOE_BM = 512
COMB_TM = 256


def _mm(a, b):
    return jnp.dot(a.astype(BF16), b.astype(BF16), preferred_element_type=F32)


def _mm_nt(a, b):
    return lax.dot_general(a.astype(BF16), b.astype(BF16), (((1,), (1,)), ((), ())),
                           preferred_element_type=F32)


def _sigmoid(x):
    return 1.0 / (1.0 + jnp.exp(-x))


def _softplus(x):
    return jnp.maximum(x, 0.0) + jnp.log(1.0 + jnp.exp(-jnp.abs(x)))


def _rms(x, g):
    ms = jnp.mean(x * x, axis=-1, keepdims=True)
    return x * lax.rsqrt(ms + RMS_EPS) * g


def _cparams(sem):
    return pltpu.CompilerParams(dimension_semantics=sem, vmem_limit_bytes=VMEM_LIMIT)


def _proj_kernel(x_ref, gmix_ref, wa_ref, wba_ref, wb_ref, bd_ref, gq_ref, gk_ref,
                 qkvz_ref, ba_ref, qkvb_ref):
    h = _rms(x_ref[...], gmix_ref[...]).astype(BF16)
    qkvz_ref[...] = jnp.dot(h, wa_ref[...], preferred_element_type=F32)
    ba_ref[...] = jnp.dot(h, wba_ref[...], preferred_element_type=F32)
    pb = jnp.dot(h, wb_ref[...], preferred_element_type=F32)
    bd = bd_ref[...]

    def head_norm(q, g):
        sq = q * q
        hi = sq.astype(BF16)
        lo = (sq - hi.astype(F32)).astype(BF16)
        ss = (jnp.dot(hi, bd, preferred_element_type=F32)
              + jnp.dot(lo, bd, preferred_element_type=F32))
        return q * lax.rsqrt(ss * (1.0 / HD_B) + RMS_EPS) * g

    qkvb_ref[:, 0:ATT_DIM_B] = head_norm(pb[:, 0:ATT_DIM_B], gq_ref[...])
    qkvb_ref[:, ATT_DIM_B:2 * ATT_DIM_B] = head_norm(pb[:, ATT_DIM_B:2 * ATT_DIM_B], gk_ref[...])
    qkvb_ref[:, 2 * ATT_DIM_B:] = pb[:, 2 * ATT_DIM_B:]


def _project(x2d, gmix, wa, wba, wb, bd, gq, gk):
    t = x2d.shape[0]
    tm = min(PROJ_TM, t)
    const = lambda i: (0, 0)
    return pl.pallas_call(
        _proj_kernel,
        grid=(t // tm,),
        in_specs=[
            pl.BlockSpec((tm, D_MODEL), lambda i: (i, 0)),
            pl.BlockSpec(gmix.shape, const),
            pl.BlockSpec(wa.shape, const),
            pl.BlockSpec(wba.shape, const),
            pl.BlockSpec(wb.shape, const),
            pl.BlockSpec(bd.shape, const),
            pl.BlockSpec(gq.shape, const),
            pl.BlockSpec(gk.shape, const),
        ],
        out_specs=[
            pl.BlockSpec((tm, wa.shape[1]), lambda i: (i, 0)),
            pl.BlockSpec((tm, LANES), lambda i: (i, 0)),
            pl.BlockSpec((tm, wb.shape[1]), lambda i: (i, 0)),
        ],
        out_shape=[
            jax.ShapeDtypeStruct((t, wa.shape[1]), F32),
            jax.ShapeDtypeStruct((t, LANES), F32),
            jax.ShapeDtypeStruct((t, wb.shape[1]), F32),
        ],
        compiler_params=_cparams(("arbitrary",)),
        name="proj",
    )(x2d, gmix, wa, wba, wb, bd, gq, gk)


def _delta_kernel(qkvz_ref, ba_ref, cbuf_ref, s0_ref, convw_ref, nega_ref, dtb_ref,
                  gon_ref, lbd_ref,
                  o_ref, sfin_ref, cnew_ref, xh_ref, s_ref, *, c, nch):
    step = pl.program_id(1)
    tile = c * nch
    hist = SUBLANES

    @pl.when(step == 0)
    def _():
        xh_ref[0:hist, :] = cbuf_ref[...]
        s_ref[...] = s0_ref[...]

    xh_ref[hist:hist + tile, :] = qkvz_ref[:, 0:CONV_DIM]
    w = convw_ref[...]
    conv = (xh_ref[hist - 3:hist - 3 + tile, :] * w[0:1, :]
            + xh_ref[hist - 2:hist - 2 + tile, :] * w[1:2, :]
            + xh_ref[hist - 1:hist - 1 + tile, :] * w[2:3, :]
            + xh_ref[hist:hist + tile, :] * w[3:4, :])
    conv = conv * _sigmoid(conv)
    last_rows = xh_ref[tile:tile + hist, :]
    cnew_ref[...] = last_rows
    xh_ref[0:hist, :] = last_rows

    ba = ba_ref[...]
    z_all = qkvz_ref[:, CONV_DIM:CONV_DIM + VAL_DIM_A]

    rs = H_A * c
    sk = H_A * DK_A
    row = lax.broadcasted_iota(jnp.int32, (rs, rs), 0)
    col = lax.broadcasted_iota(jnp.int32, (rs, rs), 1)
    same = (row // c) == (col // c)
    causal = jnp.logical_and(same, row >= col)
    strict = jnp.logical_and(same, row > col)
    upper = jnp.logical_and(same, row > col)
    wrow = lax.broadcasted_iota(jnp.int32, (rs, sk), 0)
    wcol = lax.broadcasted_iota(jnp.int32, (rs, sk), 1)
    head_block = (wrow // c) == (wcol // DK_A)
    lbd = lbd_ref[...]
    nsq = int(np.log2(c))

    def stack(fn):
        return jnp.concatenate([fn(h) for h in range(H_A)], axis=0)

    def spread(m):
        return jnp.where(head_block, jnp.concatenate([m] * H_A, axis=1), 0.0)

    def mm_split3(a_bf16, x):
        hi = x.astype(BF16)
        r1 = x - hi.astype(F32)
        mid = r1.astype(BF16)
        lo = (r1 - mid.astype(F32)).astype(BF16)
        return (jnp.dot(a_bf16, hi, preferred_element_type=F32)
                + jnp.dot(a_bf16, mid, preferred_element_type=F32)
                + jnp.dot(a_bf16, lo, preferred_element_type=F32))

    pre = []
    for ci in range(nch):
        r0 = ci * c
        q_st = stack(lambda h: conv[r0:r0 + c, h * DK_A:(h + 1) * DK_A])
        k_st = stack(lambda h: conv[r0:r0 + c, KEY_DIM_A + h * DK_A:KEY_DIM_A + (h + 1) * DK_A])
        v_st = stack(lambda h: conv[r0:r0 + c, 2 * KEY_DIM_A + h * DV_A:2 * KEY_DIM_A + (h + 1) * DV_A])
        b_raw = stack(lambda h: jnp.broadcast_to(ba[r0:r0 + c, h:h + 1], (c, LANES)))
        a_raw = stack(lambda h: jnp.broadcast_to(ba[r0:r0 + c, H_A + h:H_A + h + 1], (c, LANES)))
        q_st = q_st * lax.rsqrt(jnp.sum(q_st * q_st, axis=-1, keepdims=True) + L2_EPS) * (DK_A ** -0.5)
        k_st = k_st * lax.rsqrt(jnp.sum(k_st * k_st, axis=-1, keepdims=True) + L2_EPS)
        beta = _sigmoid(b_raw)
        g = nega_ref[...] * _softplus(a_raw + dtb_ref[...])
        g_wide = jnp.concatenate([g] * (rs // LANES), axis=1)
        dcat = mm_split3(lbd, jnp.concatenate([jnp.where(upper, g_wide, 0.0), g], axis=1))
        dlog = dcat[:, 0:rs]
        gc = dcat[:, rs:]
        decay = jnp.exp(jnp.where(causal, dlog, -jnp.inf))
        egc = jnp.exp(gc)
        glast = stack(lambda h: jnp.broadcast_to(gc[h * c + c - 1:h * c + c, :], (c, LANES)))
        kb = k_st * beta
        a_mat = jnp.where(strict, _mm_nt(kb, k_st) * decay, 0.0)
        x = jnp.concatenate([v_st * beta, kb * egc], axis=-1)
        p = -a_mat
        for s in range(nsq):
            x = x + _mm(p, x)
            if s + 1 < nsq:
                p = _mm(p, p)
        qk = _mm_nt(q_st, k_st) * decay
        eglast = stack(lambda h: jnp.broadcast_to(jnp.exp(gc[h * c + c - 1:h * c + c, :]), (DK_A, DV_A)))
        pre.append((x[:, 0:DV_A], spread(x[:, DV_A:]), qk, spread(q_st * egc),
                    spread(k_st * jnp.exp(glast - gc)), eglast))

    for ci in range(nch):
        r0 = ci * c
        u0, wbd, qk, qbd, kbd, eglast = pre[ci]
        s_all = s_ref[...]
        u = u0 - _mm(wbd, s_all)
        o = _mm(qbd, s_all) + _mm(qk, u)
        s_ref[...] = s_all * eglast + lax.dot_general(
            kbd.astype(BF16), u.astype(BF16), (((0,), (0,)), ((), ())), preferred_element_type=F32)
        on = _rms(o, gon_ref[...])
        for h in range(H_A):
            z = z_all[r0:r0 + c, h * DV_A:(h + 1) * DV_A]
            o_ref[r0:r0 + c, h * DV_A:(h + 1) * DV_A] = on[h * c:(h + 1) * c, :] * (z * _sigmoid(z))

    sfin_ref[...] = s_ref[...]


def _delta_mixer(qkvz, ba, cbuf, s0, convw, a_log, dt_bias, gon):
    b, t, _ = qkvz.shape
    c = min(CHUNK, t)
    nch = min(DELTA_NCH, t // c)
    tile = c * nch
    rs = H_A * c
    sk = H_A * DK_A
    assert rs % LANES == 0
    r = np.arange(rs)
    lbd = jnp.asarray((r[:, None] // c == r[None, :] // c) & (r[:, None] >= r[None, :]), BF16)
    nega = jnp.broadcast_to(jnp.repeat(-jnp.exp(a_log), c)[:, None], (rs, LANES))
    dtb = jnp.broadcast_to(jnp.repeat(dt_bias, c)[:, None], (rs, LANES))
    const2 = lambda i, j: (0, 0)
    kern = functools.partial(_delta_kernel, c=c, nch=nch)
    return pl.pallas_call(
        kern,
        grid=(b, t // tile),
        in_specs=[
            pl.BlockSpec((None, tile, qkvz.shape[2]), lambda i, j: (i, j, 0)),
            pl.BlockSpec((None, tile, LANES), lambda i, j: (i, j, 0)),
            pl.BlockSpec((None, SUBLANES, CONV_DIM), lambda i, j: (i, 0, 0)),
            pl.BlockSpec((None, sk, DV_A), lambda i, j: (i, 0, 0)),
            pl.BlockSpec(convw.shape, const2),
            pl.BlockSpec(nega.shape, const2),
            pl.BlockSpec(dtb.shape, const2),
            pl.BlockSpec(gon.shape, const2),
            pl.BlockSpec(lbd.shape, const2),
        ],
        out_specs=[
            pl.BlockSpec((None, tile, VAL_DIM_A), lambda i, j: (i, j, 0)),
            pl.BlockSpec((None, sk, DV_A), lambda i, j: (i, 0, 0)),
            pl.BlockSpec((None, SUBLANES, CONV_DIM), lambda i, j: (i, 0, 0)),
        ],
        out_shape=[
            jax.ShapeDtypeStruct((b, t, VAL_DIM_A), F32),
            jax.ShapeDtypeStruct((b, sk, DV_A), F32),
            jax.ShapeDtypeStruct((b, SUBLANES, CONV_DIM), F32),
        ],
        scratch_shapes=[
            pltpu.VMEM((SUBLANES + tile, CONV_DIM), F32),
            pltpu.VMEM((sk, DV_A), F32),
        ],
        compiler_params=_cparams(("arbitrary", "arbitrary")),
        name="delta",
    )(qkvz, ba, cbuf, s0, convw, nega, dtb, gon, lbd)


def _pair_queries(qp):
    lane = lax.broadcasted_iota(jnp.int32, qp.shape, 1)
    q_even = jnp.where(lane < HD_B, qp, 0.0)
    q_odd = jnp.where(lane >= HD_B, qp, 0.0)
    return jnp.concatenate([q_even, q_odd], axis=0).astype(BF16)


def _unpair(o, r):
    lane = lax.broadcasted_iota(jnp.int32, (r, LANES), 1)
    return jnp.where(lane < HD_B, o[0:r, :], o[r:2 * r, :])


def _attn_prompt_kernel(q_ref, k_ref, v_ref, bias_ref, o_ref, kwin_ref, vwin_ref, *, g):
    cstep = pl.program_id(1)
    rows = g * CHUNK
    win = (N_BACK + g) * CHUNK

    @pl.when(cstep == 0)
    def _():
        kwin_ref[...] = jnp.zeros(kwin_ref.shape, BF16)
        vwin_ref[...] = jnp.zeros(vwin_ref.shape, BF16)

    for i in range(N_BACK // g):
        kwin_ref[i * rows:(i + 1) * rows, :] = kwin_ref[(i + 1) * rows:(i + 2) * rows, :]
        vwin_ref[i * rows:(i + 1) * rows, :] = vwin_ref[(i + 1) * rows:(i + 2) * rows, :]
    kwin_ref[win - rows:win, :] = k_ref[...].astype(BF16)
    vwin_ref[win - rows:win, :] = v_ref[...].astype(BF16)

    slot_chunk = lax.broadcasted_iota(jnp.int32, (2 * rows, win), 1) // CHUNK
    in_seq = slot_chunk >= N_BACK - cstep * g
    for p in range(H_B // 2):
        sl = slice(p * LANES, (p + 1) * LANES)
        q2 = _pair_queries(q_ref[:, sl] * (HD_B ** -0.5))
        s = _mm_nt(q2, kwin_ref[:, sl]) + bias_ref[p]
        s = jnp.where(in_seq, s, NEG_BIG)
        m = jnp.max(s, axis=-1, keepdims=True)
        e = jnp.exp(s - m)
        l = jnp.sum(e, axis=-1, keepdims=True)
        o = jnp.dot(e.astype(BF16), vwin_ref[:, sl], preferred_element_type=F32) / l
        o_ref[:, sl] = _unpair(o, rows)


def _attend_prompt(qkvb, bias):
    b, t, _ = qkvb.shape
    g = ATT_G
    rows = g * CHUNK
    win = (N_BACK + g) * CHUNK
    kern = functools.partial(_attn_prompt_kernel, g=g)
    return pl.pallas_call(
        kern,
        grid=(b, t // rows),
        in_specs=[
            pl.BlockSpec((None, rows, ATT_DIM_B), lambda i, j: (i, j, 0)),
            pl.BlockSpec((None, rows, ATT_DIM_B), lambda i, j: (i, j, 1)),
            pl.BlockSpec((None, rows, ATT_DIM_B), lambda i, j: (i, j, 2)),
            pl.BlockSpec(bias.shape, lambda i, j: (0, 0, 0)),
        ],
        out_specs=pl.BlockSpec((None, rows, ATT_DIM_B), lambda i, j: (i, j, 0)),
        out_shape=jax.ShapeDtypeStruct((b, t, ATT_DIM_B), F32),
        scratch_shapes=[pltpu.VMEM((win, ATT_DIM_B), BF16), pltpu.VMEM((win, ATT_DIM_B), BF16)],
        compiler_params=_cparams(("arbitrary", "arbitrary")),
        name="attn_prompt",
    )(qkvb, qkvb, qkvb, bias)


def _attn_sample_kernel(q_ref, kn_ref, vn_ref, kc_ref, vc_ref, biasc_ref, biasn_ref, o_ref):
    ds = q_ref.shape[0]
    for p in range(H_B // 2):
        sl = slice(p * LANES, (p + 1) * LANES)
        q2 = _pair_queries(q_ref[:, sl] * (HD_B ** -0.5))
        s1 = _mm_nt(q2, kc_ref[:, sl]) + biasc_ref[p]
        s2 = _mm_nt(q2, kn_ref[:, sl]) + biasn_ref[p]
        m = jnp.maximum(jnp.max(s1, axis=-1, keepdims=True), jnp.max(s2, axis=-1, keepdims=True))
        e1 = jnp.exp(s1 - m)
        e2 = jnp.exp(s2 - m)
        l = jnp.sum(e1, axis=-1, keepdims=True) + jnp.sum(e2, axis=-1, keepdims=True)
        o = (_mm(e1, vc_ref[:, sl]) + _mm(e2, vn_ref[:, sl])) / l
        o_ref[:, sl] = _unpair(o, ds)


def _attend_sample(qkvb, cache_k, cache_v, biasc, biasn):
    b, ds, _ = qkvb.shape
    past = cache_k.shape[1]
    return pl.pallas_call(
        _attn_sample_kernel,
        grid=(b,),
        in_specs=[
            pl.BlockSpec((None, ds, ATT_DIM_B), lambda i: (i, 0, 0)),
            pl.BlockSpec((None, ds, ATT_DIM_B), lambda i: (i, 0, 1)),
            pl.BlockSpec((None, ds, ATT_DIM_B), lambda i: (i, 0, 2)),
            pl.BlockSpec((None, past, ATT_DIM_B), lambda i: (i, 0, 0)),
            pl.BlockSpec((None, past, ATT_DIM_B), lambda i: (i, 0, 0)),
            pl.BlockSpec(biasc.shape, lambda i: (0, 0, 0)),
            pl.BlockSpec(biasn.shape, lambda i: (0, 0, 0)),
        ],
        out_specs=pl.BlockSpec((None, ds, ATT_DIM_B), lambda i: (i, 0, 0)),
        out_shape=jax.ShapeDtypeStruct((b, ds, ATT_DIM_B), F32),
        compiler_params=_cparams(("arbitrary",)),
        name="attn_sample",
    )(qkvb, qkvb, qkvb, cache_k, cache_v, biasc, biasn)


def _router_kernel(oap_ref, obp_ref, xp_ref, oas_ref, obs_ref, xs_ref,
                   woa_ref, wob_ref, gffn_ref, wrt_ref, brt_ref, utri_ref,
                   x1_ref, h2_ref, idx_ref, gate_ref, rank_ref, cnt_ref,
                   oa_s, ob_s, x_s, carry_s, *, ntp):
    i = pl.program_id(0)
    tm = x_s.shape[0]

    @pl.when(i == 0)
    def _():
        carry_s[...] = jnp.zeros(carry_s.shape, F32)

    @pl.when(i < ntp)
    def _():
        oa_s[...] = oap_ref[...]
        ob_s[...] = obp_ref[...]
        x_s[...] = xp_ref[...]

    @pl.when(i >= ntp)
    def _():
        oa_s[...] = oas_ref[...]
        ob_s[...] = obs_ref[...]
        x_s[...] = xs_ref[...]

    x1 = x_s[...] + _mm(oa_s[...], woa_ref[...]) + _mm(ob_s[...], wob_ref[...])
    x1_ref[...] = x1
    h2 = _rms(x1, gffn_ref[...])
    h2_ref[...] = h2
    logits = _mm_nt(wrt_ref[...], h2) + brt_ref[:, 0:1]

    eidx = lax.broadcasted_iota(jnp.int32, (N_EXPERTS, tm), 0).astype(F32)
    vals = logits
    memf = jnp.zeros((N_EXPERTS, tm), F32)
    tops, sels = [], []
    for k in range(TOP_K):
        m = jnp.max(vals, axis=0, keepdims=True)
        ix = jnp.min(jnp.where(vals == m, eidx, float(N_EXPERTS)), axis=0, keepdims=True)
        sel = eidx == ix
        tops.append(m)
        sels.append(sel)
        idx_ref[k:k + 1, :] = ix.astype(jnp.int32)
        vals = jnp.where(sel, -jnp.inf, vals)
        memf = memf + jnp.where(sel, 1.0, 0.0)
    es = [jnp.exp(v - tops[0]) for v in tops]
    denom = es[0] + es[1] + es[2] + es[3]
    for k in range(TOP_K):
        gate_ref[k:k + 1, :] = es[k] / denom

    carry = carry_s[:, 0:1]
    excl = _mm(memf, utri_ref[...]) + carry
    for k in range(TOP_K):
        r = jnp.sum(jnp.where(sels[k], excl, 0.0), axis=0, keepdims=True)
        rank_ref[k:k + 1, :] = r.astype(jnp.int32)
    new_carry = carry + jnp.sum(memf, axis=1, keepdims=True)
    carry_s[...] = jnp.broadcast_to(new_carry, carry_s.shape)
    cnt_ref[...] = jnp.broadcast_to(new_carry, cnt_ref.shape)


def _out_router(oa_p, ob_p, x_p, oa_s, ob_s, x_s, woa, wob, gffn, wrt, brt):
    tp, ts = x_p.shape[0], x_s.shape[0]
    tm = min(PROJ_TM, ts, tp)
    ntp, nts = tp // tm, ts // tm
    ttot = tp + ts
    utri = jnp.triu(jnp.ones((tm, tm), BF16), k=1)
    pmap = lambda i: (jnp.minimum(i, ntp - 1), 0)
    smap = lambda i: (jnp.maximum(i - ntp, 0), 0)
    const = lambda i: (0, 0)
    kern = functools.partial(_router_kernel, ntp=ntp)
    return pl.pallas_call(
        kern,
        grid=(ntp + nts,),
        in_specs=[
            pl.BlockSpec((tm, VAL_DIM_A), pmap),
            pl.BlockSpec((tm, ATT_DIM_B), pmap),
            pl.BlockSpec((tm, D_MODEL), pmap),
            pl.BlockSpec((tm, VAL_DIM_A), smap),
            pl.BlockSpec((tm, ATT_DIM_B), smap),
            pl.BlockSpec((tm, D_MODEL), smap),
            pl.BlockSpec(woa.shape, const),
            pl.BlockSpec(wob.shape, const),
            pl.BlockSpec(gffn.shape, const),
            pl.BlockSpec(wrt.shape, const),
            pl.BlockSpec(brt.shape, const),
            pl.BlockSpec(utri.shape, const),
        ],
        out_specs=[
            pl.BlockSpec((tm, D_MODEL), lambda i: (i, 0)),
            pl.BlockSpec((tm, D_MODEL), lambda i: (i, 0)),
            pl.BlockSpec((TOP_K, tm), lambda i: (0, i)),
            pl.BlockSpec((TOP_K, tm), lambda i: (0, i)),
            pl.BlockSpec((TOP_K, tm), lambda i: (0, i)),
            pl.BlockSpec((N_EXPERTS, LANES), const),
        ],
        out_shape=[
            jax.ShapeDtypeStruct((ttot, D_MODEL), F32),
            jax.ShapeDtypeStruct((ttot, D_MODEL), F32),
            jax.ShapeDtypeStruct((TOP_K, ttot), jnp.int32),
            jax.ShapeDtypeStruct((TOP_K, ttot), F32),
            jax.ShapeDtypeStruct((TOP_K, ttot), jnp.int32),
            jax.ShapeDtypeStruct((N_EXPERTS, LANES), F32),
        ],
        scratch_shapes=[
            pltpu.VMEM((tm, VAL_DIM_A), F32),
            pltpu.VMEM((tm, ATT_DIM_B), F32),
            pltpu.VMEM((tm, D_MODEL), F32),
            pltpu.VMEM((N_EXPERTS, LANES), F32),
        ],
        compiler_params=_cparams(("arbitrary",)),
        name="out_router",
    )(oa_p, ob_p, x_p, oa_s, ob_s, x_s, woa, wob, gffn, wrt, brt, utri)


def _row_copy(src_ref, s, dst_ref, d, sem):
    return pltpu.make_async_copy(src_ref.at[pl.ds(s, 1), :], dst_ref.at[pl.ds(d, 1), :], sem)


def _dispatch_kernel(padlo_ref, padhi_ref, dest_ref, h2_ref, xs_ref, zrow_ref, sem):
    i = pl.program_id(0)
    tm = h2_ref.shape[0]

    def issue(t, carry):
        for k in range(TOP_K):
            _row_copy(h2_ref, t, xs_ref, dest_ref[k, t], sem).start(priority=k % 2)
        return carry

    lax.fori_loop(0, tm, issue, 0, unroll=2)
    for k in range(TOP_K):
        pltpu.make_async_copy(h2_ref, xs_ref.at[pl.ds(0, tm), :], sem).wait()

    @pl.when(i == pl.num_programs(0) - 1)
    def _():
        zrow_ref[...] = jnp.zeros(zrow_ref.shape, F32)
        for e in range(N_EXPERTS):
            lo = padlo_ref[e]
            hi = padhi_ref[e]

            def zissue(r, carry):
                _row_copy(zrow_ref, 0, xs_ref, r, sem).start()
                return carry

            lax.fori_loop(lo, hi, zissue, 0)

            def zdrain(r, carry):
                _row_copy(zrow_ref, 0, xs_ref, 0, sem).wait()
                return carry

            lax.fori_loop(lo, hi, zdrain, 0)


def _dispatch(padlo, padhi, dest, h2, n_rows, tm_rows):
    ttot = h2.shape[0]
    tm = tm_rows
    grid_spec = pltpu.PrefetchScalarGridSpec(
        num_scalar_prefetch=2,
        grid=(ttot // tm,),
        in_specs=[
            pl.BlockSpec((TOP_K, tm), lambda i, lo, hi: (0, i), memory_space=pltpu.SMEM),
            pl.BlockSpec((tm, D_MODEL), lambda i, lo, hi: (i, 0)),
        ],
        out_specs=pl.BlockSpec(memory_space=pl.ANY),
        scratch_shapes=[pltpu.VMEM((SUBLANES, D_MODEL), F32), pltpu.SemaphoreType.DMA(())],
    )
    return pl.pallas_call(
        _dispatch_kernel,
        grid_spec=grid_spec,
        out_shape=jax.ShapeDtypeStruct((n_rows, D_MODEL), F32),
        compiler_params=_cparams(("arbitrary",)),
        name="dispatch",
    )(padlo, padhi, dest, h2)


def _expert_kernel(bexp_ref, nused_ref, x_ref, wgu_ref, bgu_ref, wd_ref, bd_ref, y_ref, wgu_s, wd_s):
    j = pl.program_id(0)
    active = j < nused_ref[0]
    new_expert = jnp.logical_or(j == 0, bexp_ref[j] != bexp_ref[jnp.maximum(j - 1, 0)])

    @pl.when(jnp.logical_and(active, new_expert))
    def _():
        rows = 128
        for r in range(0, D_MODEL, rows):
            wgu_s[r:r + rows, :] = wgu_ref[r:r + rows, :].astype(BF16)
        for r in range(0, D_FF, rows):
            wd_s[r:r + rows, :] = wd_ref[r:r + rows, :].astype(BF16)

    @pl.when(active)
    def _():
        gu = _mm(x_ref[...], wgu_s[...]) + bgu_ref[...]
        gate = jnp.minimum(gu[:, 0:D_FF], SWIGLU_LIMIT)
        up = jnp.clip(gu[:, D_FF:], -SWIGLU_LIMIT, SWIGLU_LIMIT)
        hid = (up + 1.0) * (gate * _sigmoid(gate * SWIGLU_ALPHA))
        y_ref[...] = _mm(hid, wd_s[...]) + bd_ref[...]


def _experts(bexp, nused, xs, wgu, bgu, wd, bd):
    n_rows = xs.shape[0]
    bm = MOE_BM
    nb = n_rows // bm
    row_map = lambda j, be, nu: (jnp.minimum(j, nu[0] - 1), 0)
    w_map = lambda j, be, nu: (be[j], 0, 0)
    grid_spec = pltpu.PrefetchScalarGridSpec(
        num_scalar_prefetch=2,
        grid=(nb,),
        in_specs=[
            pl.BlockSpec((bm, D_MODEL), row_map),
            pl.BlockSpec((None, D_MODEL, 2 * D_FF), w_map),
            pl.BlockSpec((None, 1, 2 * D_FF), w_map),
            pl.BlockSpec((None, D_FF, D_MODEL), w_map),
            pl.BlockSpec((None, 1, D_MODEL), w_map),
        ],
        out_specs=pl.BlockSpec((bm, D_MODEL), row_map),
        scratch_shapes=[pltpu.VMEM((D_MODEL, 2 * D_FF), BF16), pltpu.VMEM((D_FF, D_MODEL), BF16)],
    )
    return pl.pallas_call(
        _expert_kernel,
        grid_spec=grid_spec,
        out_shape=jax.ShapeDtypeStruct((n_rows, D_MODEL), F32),
        compiler_params=_cparams(("arbitrary",)),
        name="experts",
    )(bexp, nused, xs, wgu, bgu, wd, bd)


def _combine_kernel(dest_ref, gate_ref, x1_ref, pp_ref, ps_ref, yb_ref, gple_ref, wg_ref, wp_ref,
                    yp_ref, ys_ref, buf_ref, sem, *, ntp):
    i = pl.program_id(0)
    tm = x1_ref.shape[0]

    def issue(t, carry):
        for k in range(TOP_K):
            pltpu.make_async_copy(yb_ref.at[pl.ds(dest_ref[k, t], 1), :],
                                  buf_ref.at[k, pl.ds(t, 1), :], sem).start(priority=k % 2)
        return carry

    lax.fori_loop(0, tm, issue, 0, unroll=2)
    for k in range(TOP_K):
        pltpu.make_async_copy(yb_ref.at[pl.ds(0, tm), :], buf_ref.at[k], sem).wait()

    gates = gate_ref[...]
    moe = buf_ref[0] * gates[:, 0:1]
    for k in range(1, TOP_K):
        moe = moe + buf_ref[k] * gates[:, k:k + 1]
    x2 = x1_ref[...] + moe
    gate = _sigmoid(_mm(_rms(x2, gple_ref[...]), wg_ref[...]))

    @pl.when(i < ntp)
    def _():
        yp_ref[...] = x2 + gate * _mm(pp_ref[...], wp_ref[...])

    @pl.when(i >= ntp)
    def _():
        ys_ref[...] = x2 + gate * _mm(ps_ref[...], wp_ref[...])


def _combine(dest, gates_col, x1, p_p, p_s, yb, gple, wg, wp):
    tp, ts = p_p.shape[0], p_s.shape[0]
    tm = min(COMB_TM, tp, ts)
    ntp, nts = tp // tm, ts // tm
    pmap = lambda i: (jnp.minimum(i, ntp - 1), 0)
    smap = lambda i: (jnp.maximum(i - ntp, 0), 0)
    const = lambda i: (0, 0)
    kern = functools.partial(_combine_kernel, ntp=ntp)
    return pl.pallas_call(
        kern,
        grid=(ntp + nts,),
        in_specs=[
            pl.BlockSpec((TOP_K, tm), lambda i: (0, i), memory_space=pltpu.SMEM),
            pl.BlockSpec((tm, TOP_K), lambda i: (i, 0)),
            pl.BlockSpec((tm, D_MODEL), lambda i: (i, 0)),
            pl.BlockSpec((tm, PLE_DIM), pmap),
            pl.BlockSpec((tm, PLE_DIM), smap),
            pl.BlockSpec(memory_space=pl.ANY),
            pl.BlockSpec(gple.shape, const),
            pl.BlockSpec(wg.shape, const),
            pl.BlockSpec(wp.shape, const),
        ],
        out_specs=[
            pl.BlockSpec((tm, D_MODEL), pmap),
            pl.BlockSpec((tm, D_MODEL), smap),
        ],
        out_shape=[
            jax.ShapeDtypeStruct((tp, D_MODEL), F32),
            jax.ShapeDtypeStruct((ts, D_MODEL), F32),
        ],
        scratch_shapes=[pltpu.VMEM((TOP_K, tm, D_MODEL), F32), pltpu.SemaphoreType.DMA(())],
        compiler_params=_cparams(("arbitrary",)),
        name="combine",
    )(dest, gates_col, x1, p_p, p_s, yb, gple, wg, wp)


def _bias_tables(rel_bias, n_q, n_k, key_offset, band):
    i = np.arange(n_q)[:, None]
    j = np.arange(n_k)[None, :]
    d_max = n_q - 1 + key_offset
    d_min = key_offset - (n_k - 1)
    n_hi = max(0, d_max - REL_MAX)
    n_lo = max(0, -REL_MAX - d_min)
    mid = rel_bias[:, max(d_min, -REL_MAX) + REL_MAX:min(d_max, REL_MAX) + REL_MAX + 1][:, ::-1]
    e = jnp.concatenate([jnp.broadcast_to(rel_bias[:, 2 * REL_MAX:], (H_B, n_hi)), mid,
                         jnp.broadcast_to(rel_bias[:, 0:1], (H_B, n_lo))], axis=1)
    period = n_q + n_k
    e = jnp.pad(e, ((0, 0), (0, 1)))
    flat = jnp.tile(e, (1, n_q + 1))[:, 0:n_q * (period + 1)]
    tab = flat.reshape(H_B, n_q, period + 1)[:, ::-1, 0:n_k]
    if band:
        qc = i // CHUNK
        sc = j // CHUNK
        ok = (sc >= qc) & (sc <= qc + N_BACK)
        tab = jnp.where(jnp.asarray(ok)[None], tab, NEG_BIG)
    return tab.reshape(H_B // 2, 2 * n_q, n_k)


def kernel(x_prompt, x_sample, state_delta, state_conv, cache_k, cache_v, p_prompt, p_sample, g_mix, w_in, conv_w, a_log, dt_bias, g_onorm, g_qnorm, g_knorm, rel_bias, w_out, g_ffn, w_router, b_router, w_gu, b_gu, w_down, b_down, g_ple, w_ple_gate, w_ple_proj):
    assert w_in.shape[0] == 1, "single-layer kernel"
    bp, tp_len, _ = x_prompt.shape
    bs, ts_len, _ = x_sample.shape
    tp, ts = bp * tp_len, bs * ts_len
    past = cache_k.shape[2]

    w = w_in[0]
    o_z = CONV_DIM + VAL_DIM_A
    o_qb = o_z + 2 * H_A
    wa = w[:, 0:o_z].astype(BF16)
    wba = jnp.pad(w[:, o_z:o_qb], ((0, 0), (0, LANES - 2 * H_A))).astype(BF16)
    wb = w[:, o_qb:].astype(BF16)
    grp = np.arange(ATT_DIM_B) // HD_B
    bd = jnp.asarray(grp[:, None] == grp[None, :], BF16)
    gq = jnp.tile(g_qnorm[0], H_B)[None, :]
    gk = jnp.tile(g_knorm[0], H_B)[None, :]
    gmix = g_mix[0][None, :]
    gon = g_onorm[0][None, :]
    convw = conv_w[0]
    woa = w_out[0, 0:VAL_DIM_A].astype(BF16)
    wob = w_out[0, VAL_DIM_A:].astype(BF16)
    gffn = g_ffn[0][None, :]
    wrt = w_router[0].T.astype(BF16)
    brt = jnp.broadcast_to(b_router[0][:, None], (N_EXPERTS, LANES))
    wgu = w_gu[0]
    bgu = b_gu[0][:, None, :]
    wd = w_down[0]
    bdn = b_down[0][:, None, :]
    gple = g_ple[0][None, :]
    wg = w_ple_gate[0].astype(BF16)
    wp = w_ple_proj[0].astype(BF16)

    xp2 = x_prompt.reshape(tp, D_MODEL)
    xs2 = x_sample.reshape(ts, D_MODEL)

    qkvz_p, ba_p, qkvb_p = _project(xp2, gmix, wa, wba, wb, bd, gq, gk)
    qkvz_s, ba_s, qkvb_s = _project(xs2, gmix, wa, wba, wb, bd, gq, gk)

    hist_pad = ((0, 0), (SUBLANES - (CONV_W - 1), 0), (0, 0))
    cbuf_p = jnp.zeros((bp, SUBLANES, CONV_DIM), F32)
    cbuf_s = jnp.pad(state_conv[0], hist_pad)
    s0_p = jnp.zeros((bp, H_A * DK_A, DV_A), F32)
    s0_s = state_delta[0].reshape(bs, H_A * DK_A, DV_A)
    oa_p, sfin_p, cnew_p = _delta_mixer(qkvz_p.reshape(bp, tp_len, -1), ba_p.reshape(bp, tp_len, LANES),
                                        cbuf_p, s0_p, convw, a_log[0], dt_bias[0], gon)
    oa_s, sfin_s, cnew_s = _delta_mixer(qkvz_s.reshape(bs, ts_len, -1), ba_s.reshape(bs, ts_len, LANES),
                                        cbuf_s, s0_s, convw, a_log[0], dt_bias[0], gon)
    sfin_p = sfin_p.reshape(bp, H_A, DK_A, DV_A)
    sfin_s = sfin_s.reshape(bs, H_A, DK_A, DV_A)

    qkvb_p3 = qkvb_p.reshape(bp, tp_len, 3 * ATT_DIM_B)
    qkvb_s3 = qkvb_s.reshape(bs, ts_len, 3 * ATT_DIM_B)
    bias_p = _bias_tables(rel_bias[0], ATT_G * CHUNK, (N_BACK + ATT_G) * CHUNK, BAND_PAST, True)
    ob_p = _attend_prompt(qkvb_p3, bias_p)
    bias_s = _bias_tables(rel_bias[0], ts_len, past + ts_len, past, False)
    ob_s = _attend_sample(qkvb_s3, cache_k[0].reshape(bs, past, ATT_DIM_B),
                          cache_v[0].reshape(bs, past, ATT_DIM_B),
                          bias_s[:, :, 0:past], bias_s[:, :, past:])

    x1, h2, idx, gates, rank, cnt = _out_router(
        oa_p.reshape(tp, VAL_DIM_A), ob_p.reshape(tp, ATT_DIM_B), xp2,
        oa_s.reshape(ts, VAL_DIM_A), ob_s.reshape(ts, ATT_DIM_B), xs2,
        woa, wob, gffn, wrt, brt)
    ttot = tp + ts
    bm = MOE_BM
    counts = cnt[:, 0].astype(jnp.int32)
    padded = (counts + bm - 1) // bm * bm
    pend = jnp.cumsum(padded)
    pstart = pend - padded
    eids = jnp.arange(N_EXPERTS, dtype=jnp.int32)
    start_of = jnp.sum(jnp.where(idx[None] == eids[:, None, None], pstart[:, None, None], 0), axis=0)
    dest = start_of + rank
    nb = -(-(ttot * TOP_K) // bm) + N_EXPERTS
    nused = (pend[-1] // bm).astype(jnp.int32)[None]
    first = jnp.minimum(jnp.arange(nb, dtype=jnp.int32), nused[0] - 1) * bm
    bexp = jnp.minimum(jnp.sum((pend[None, :] <= first[:, None]).astype(jnp.int32), axis=1),
                       N_EXPERTS - 1)

    xs_rows = _dispatch(pstart + counts, pend, dest, h2, nb * bm, min(COMB_TM, tp, ts))
    yb = _experts(bexp, nused, xs_rows, wgu, bgu, wd, bdn)
    y_p, y_s = _combine(dest, gates.T, x1, p_prompt[0].reshape(tp, PLE_DIM),
                        p_sample[0].reshape(ts, PLE_DIM), yb, gple, wg, wp)

    keep = min(BAND_PAST, tp_len)
    k_p = qkvb_p3[:, tp_len - keep:, ATT_DIM_B:2 * ATT_DIM_B].reshape(1, bp, keep, H_B, HD_B)
    v_p = qkvb_p3[:, tp_len - keep:, 2 * ATT_DIM_B:].reshape(1, bp, keep, H_B, HD_B)
    k_s = qkvb_s3[:, :, ATT_DIM_B:2 * ATT_DIM_B].reshape(1, bs, ts_len, H_B, HD_B)
    v_s = qkvb_s3[:, :, 2 * ATT_DIM_B:].reshape(1, bs, ts_len, H_B, HD_B)
    nconv = CONV_W - 1
    return (y_p.reshape(bp, tp_len, D_MODEL), y_s.reshape(bs, ts_len, D_MODEL),
            sfin_p[None], cnew_p[None, :, SUBLANES - nconv:, :], k_p, v_p,
            sfin_s[None], cnew_s[None, :, SUBLANES - nconv:, :], k_s, v_s)
```

```python
import functools

import jax
import jax.numpy as jnp
import numpy as np
from jax import lax
from jax.experimental import pallas as pl
from jax.experimental.pallas import tpu as pltpu

F32 = jnp.float32
BF16 = jnp.bfloat16

D_MODEL = 1024
CHUNK = 64
H_A = 4
DK_A = 128
DV_A = 128
CONV_W = 4
KEY_DIM_A = H_A * DK_A
VAL_DIM_A = H_A * DV_A
CONV_DIM = 2 * KEY_DIM_A + VAL_DIM_A
H_B = 8
HD_B = 64
ATT_DIM_B = H_B * HD_B
N_BACK = 8
BAND_PAST = N_BACK * CHUNK
REL_MAX = 256
N_EXPERTS = 32
TOP_K = 4
D_FF = 1024
SWIGLU_LIMIT = 7.0
SWIGLU_ALPHA = 1.702
PLE_DIM = 256
RMS_EPS = 1e-6
L2_EPS = 1e-6
NEG_BIG = -1e30

LANES = 128
SUBLANES = 8
VMEM_LIMIT = 56 * 1024 * 1024

PROJ_TM = 512
ATT_G = 2
DELTA_NCH = 4
MOE_BM = 512
COMB_TM = 256


def _mm(a, b):
    return jnp.dot(a.astype(BF16), b.astype(BF16), preferred_element_type=F32)


def _mm_nt(a, b):
    return lax.dot_general(a.astype(BF16), b.astype(BF16), (((1,), (1,)), ((), ())),
                           preferred_element_type=F32)


def _sigmoid(x):
    return 1.0 / (1.0 + jnp.exp(-x))


def _softplus(x):
    return jnp.maximum(x, 0.0) + jnp.log(1.0 + jnp.exp(-jnp.abs(x)))


def _rms(x, g):
    ms = jnp.mean(x * x, axis=-1, keepdims=True)
    return x * lax.rsqrt(ms + RMS_EPS) * g


def _cparams(sem):
    return pltpu.CompilerParams(dimension_semantics=sem, vmem_limit_bytes=VMEM_LIMIT)


def _proj_kernel(x_ref, gmix_ref, wa_ref, wba_ref, wb_ref, bd_ref, gq_ref, gk_ref,
                 qkvz_ref, ba_ref, qkvb_ref):
    h = _rms(x_ref[...], gmix_ref[...]).astype(BF16)
    qkvz_ref[...] = jnp.dot(h, wa_ref[...], preferred_element_type=F32)
    ba_ref[...] = jnp.dot(h, wba_ref[...], preferred_element_type=F32)
    pb = jnp.dot(h, wb_ref[...], preferred_element_type=F32)
    bd = bd_ref[...]

    def head_norm(q, g):
        sq = q * q
        hi = sq.astype(BF16)
        lo = (sq - hi.astype(F32)).astype(BF16)
        ss = (jnp.dot(hi, bd, preferred_element_type=F32)
              + jnp.dot(lo, bd, preferred_element_type=F32))
        return q * lax.rsqrt(ss * (1.0 / HD_B) + RMS_EPS) * g

    qkvb_ref[:, 0:ATT_DIM_B] = head_norm(pb[:, 0:ATT_DIM_B], gq_ref[...])
    qkvb_ref[:, ATT_DIM_B:2 * ATT_DIM_B] = head_norm(pb[:, ATT_DIM_B:2 * ATT_DIM_B], gk_ref[...])
    qkvb_ref[:, 2 * ATT_DIM_B:] = pb[:, 2 * ATT_DIM_B:]


def _project(x2d, gmix, wa, wba, wb, bd, gq, gk):
    t = x2d.shape[0]
    tm = min(PROJ_TM, t)
    const = lambda i: (0, 0)
    return pl.pallas_call(
        _proj_kernel,
        grid=(t // tm,),
        in_specs=[
            pl.BlockSpec((tm, D_MODEL), lambda i: (i, 0)),
            pl.BlockSpec(gmix.shape, const),
            pl.BlockSpec(wa.shape, const),
            pl.BlockSpec(wba.shape, const),
            pl.BlockSpec(wb.shape, const),
            pl.BlockSpec(bd.shape, const),
            pl.BlockSpec(gq.shape, const),
            pl.BlockSpec(gk.shape, const),
        ],
        out_specs=[
            pl.BlockSpec((tm, wa.shape[1]), lambda i: (i, 0)),
            pl.BlockSpec((tm, LANES), lambda i: (i, 0)),
            pl.BlockSpec((tm, wb.shape[1]), lambda i: (i, 0)),
        ],
        out_shape=[
            jax.ShapeDtypeStruct((t, wa.shape[1]), F32),
            jax.ShapeDtypeStruct((t, LANES), F32),
            jax.ShapeDtypeStruct((t, wb.shape[1]), F32),
        ],
        compiler_params=_cparams(("arbitrary",)),
        name="proj",
    )(x2d, gmix, wa, wba, wb, bd, gq, gk)


def _delta_kernel(qkvz_ref, ba_ref, cbuf_ref, s0_ref, convw_ref, nega_ref, dtb_ref,
                  gon_ref, lbd_ref,
                  o_ref, sfin_ref, cnew_ref, xh_ref, s_ref, *, c, nch):
    step = pl.program_id(1)
    tile = c * nch
    hist = SUBLANES

    @pl.when(step == 0)
    def _():
        xh_ref[0:hist, :] = cbuf_ref[...]
        s_ref[...] = s0_ref[...]

    xh_ref[hist:hist + tile, :] = qkvz_ref[:, 0:CONV_DIM]
    w = convw_ref[...]
    conv = (xh_ref[hist - 3:hist - 3 + tile, :] * w[0:1, :]
            + xh_ref[hist - 2:hist - 2 + tile, :] * w[1:2, :]
            + xh_ref[hist - 1:hist - 1 + tile, :] * w[2:3, :]
            + xh_ref[hist:hist + tile, :] * w[3:4, :])
    conv = conv * _sigmoid(conv)
    last_rows = xh_ref[tile:tile + hist, :]
    cnew_ref[...] = last_rows
    xh_ref[0:hist, :] = last_rows

    ba = ba_ref[...]
    z_all = qkvz_ref[:, CONV_DIM:CONV_DIM + VAL_DIM_A]

    rs = H_A * c
    sk = H_A * DK_A
    row = lax.broadcasted_iota(jnp.int32, (rs, rs), 0)
    col = lax.broadcasted_iota(jnp.int32, (rs, rs), 1)
    same = (row // c) == (col // c)
    causal = jnp.logical_and(same, row >= col)
    strict = jnp.logical_and(same, row > col)
    wrow = lax.broadcasted_iota(jnp.int32, (rs, sk), 0)
    wcol = lax.broadcasted_iota(jnp.int32, (rs, sk), 1)
    head_block = (wrow // c) == (wcol // DK_A)
    lbd = lbd_ref[...]
    nsq = int(np.log2(c))

    def stack(fn):
        return jnp.concatenate([fn(h) for h in range(H_A)], axis=0)

    def spread(m):
        return jnp.where(head_block, jnp.concatenate([m] * H_A, axis=1), 0.0)

    def mm_split3(a_bf16, x):
        hi = x.astype(BF16)
        r1 = x - hi.astype(F32)
        mid = r1.astype(BF16)
        lo = (r1 - mid.astype(F32)).astype(BF16)
        return (jnp.dot(a_bf16, hi, preferred_element_type=F32)
                + jnp.dot(a_bf16, mid, preferred_element_type=F32)
                + jnp.dot(a_bf16, lo, preferred_element_type=F32))

    def l2n(m):
        return m * lax.rsqrt(jnp.sum(m * m, axis=-1, keepdims=True) + L2_EPS)

    def last_row(m, h, shape):
        return jnp.broadcast_to(m[h * c + c - 1:h * c + c, :], shape)

    cs = range(nch)
    q_st = [l2n(stack(lambda h: conv[ci * c:ci * c + c, h * DK_A:(h + 1) * DK_A])) * (DK_A ** -0.5)
            for ci in cs]
    k_st = [l2n(stack(lambda h: conv[ci * c:ci * c + c, KEY_DIM_A + h * DK_A:KEY_DIM_A + (h + 1) * DK_A]))
            for ci in cs]
    v_st = [stack(lambda h: conv[ci * c:ci * c + c, 2 * KEY_DIM_A + h * DV_A:2 * KEY_DIM_A + (h + 1) * DV_A])
            for ci in cs]
    beta = [_sigmoid(stack(lambda h: jnp.broadcast_to(ba[ci * c:ci * c + c, h:h + 1], (c, LANES))))
            for ci in cs]
    g = [nega_ref[...] * _softplus(
        stack(lambda h: jnp.broadcast_to(ba[ci * c:ci * c + c, H_A + h:H_A + h + 1], (c, LANES)))
        + dtb_ref[...]) for ci in cs]
    gc = [mm_split3(lbd, gi) for gi in g]
    kb = [k * b for k, b in zip(k_st, beta)]
    kk = [_mm_nt(a, b) for a, b in zip(kb, k_st)]
    qk = [_mm_nt(a, b) for a, b in zip(q_st, k_st)]
    decay = [jnp.exp(jnp.where(causal,
                               jnp.concatenate([m] * (rs // LANES), axis=1)
                               - jnp.broadcast_to(m.T[0:1, :], (rs, rs)),
                               -jnp.inf)) for m in gc]
    egc = [jnp.exp(m) for m in gc]
    ps = [-jnp.where(strict, a * d, 0.0) for a, d in zip(kk, decay)]
    qk = [a * d for a, d in zip(qk, decay)]
    xs = [jnp.concatenate([v * b, k * e], axis=-1) for v, b, k, e in zip(v_st, beta, kb, egc)]
    for s in range(nsq):
        xs = [x + _mm(p, x) for x, p in zip(xs, ps)]
        if s + 1 < nsq:
            ps = [_mm(p, p) for p in ps]
    qbd = [spread(q * e) for q, e in zip(q_st, egc)]
    kbd = [spread(k * jnp.exp(stack(lambda h: last_row(m, h, (c, LANES))) - m)) for k, m in zip(k_st, gc)]
    egl = [stack(lambda h: last_row(e, h, (DK_A, DV_A))) for e in egc]

    for ci in range(nch):
        r0 = ci * c
        eglast = egl[ci]
        u0 = xs[ci][:, 0:DV_A]
        wbd = spread(xs[ci][:, DV_A:])
        s_all = s_ref[...]
        u = u0 - _mm(wbd, s_all)
        o = _mm(qbd[ci], s_all) + _mm(qk[ci], u)
        s_ref[...] = s_all * eglast + lax.dot_general(
            kbd[ci].astype(BF16), u.astype(BF16), (((0,), (0,)), ((), ())), preferred_element_type=F32)
        on = _rms(o, gon_ref[...])
        for h in range(H_A):
            z = z_all[r0:r0 + c, h * DV_A:(h + 1) * DV_A]
            o_ref[r0:r0 + c, h * DV_A:(h + 1) * DV_A] = on[h * c:(h + 1) * c, :] * (z * _sigmoid(z))

    sfin_ref[...] = s_ref[...]


def _delta_mixer(qkvz, ba, cbuf, s0, convw, a_log, dt_bias, gon):
    b, t, _ = qkvz.shape
    c = min(CHUNK, t)
    nch = min(DELTA_NCH, t // c)
    tile = c * nch
    rs = H_A * c
    sk = H_A * DK_A
    assert rs % LANES == 0
    r = np.arange(rs)
    lbd = jnp.asarray((r[:, None] // c == r[None, :] // c) & (r[:, None] >= r[None, :]), BF16)
    nega = jnp.broadcast_to(jnp.repeat(-jnp.exp(a_log), c)[:, None], (rs, LANES))
    dtb = jnp.broadcast_to(jnp.repeat(dt_bias, c)[:, None], (rs, LANES))
    const2 = lambda i, j: (0, 0)
    kern = functools.partial(_delta_kernel, c=c, nch=nch)
    return pl.pallas_call(
        kern,
        grid=(b, t // tile),
        in_specs=[
            pl.BlockSpec((None, tile, qkvz.shape[2]), lambda i, j: (i, j, 0)),
            pl.BlockSpec((None, tile, LANES), lambda i, j: (i, j, 0)),
            pl.BlockSpec((None, SUBLANES, CONV_DIM), lambda i, j: (i, 0, 0)),
            pl.BlockSpec((None, sk, DV_A), lambda i, j: (i, 0, 0)),
            pl.BlockSpec(convw.shape, const2),
            pl.BlockSpec(nega.shape, const2),
            pl.BlockSpec(dtb.shape, const2),
            pl.BlockSpec(gon.shape, const2),
            pl.BlockSpec(lbd.shape, const2),
        ],
        out_specs=[
            pl.BlockSpec((None, tile, VAL_DIM_A), lambda i, j: (i, j, 0)),
            pl.BlockSpec((None, sk, DV_A), lambda i, j: (i, 0, 0)),
            pl.BlockSpec((None, SUBLANES, CONV_DIM), lambda i, j: (i, 0, 0)),
        ],
        out_shape=[
            jax.ShapeDtypeStruct((b, t, VAL_DIM_A), F32),
            jax.ShapeDtypeStruct((b, sk, DV_A), F32),
            jax.ShapeDtypeStruct((b, SUBLANES, CONV_DIM), F32),
        ],
        scratch_shapes=[
            pltpu.VMEM((SUBLANES + tile, CONV_DIM), F32),
            pltpu.VMEM((sk, DV_A), F32),
        ],
        compiler_params=_cparams(("arbitrary", "arbitrary")),
        name="delta",
    )(qkvz, ba, cbuf, s0, convw, nega, dtb, gon, lbd)


def _pair_queries(qp):
    lane = lax.broadcasted_iota(jnp.int32, qp.shape, 1)
    q_even = jnp.where(lane < HD_B, qp, 0.0)
    q_odd = jnp.where(lane >= HD_B, qp, 0.0)
    return jnp.concatenate([q_even, q_odd], axis=0).astype(BF16)


def _unpair(o, r):
    lane = lax.broadcasted_iota(jnp.int32, (r, LANES), 1)
    return jnp.where(lane < HD_B, o[0:r, :], o[r:2 * r, :])


def _attn_prompt_kernel(q_ref, k_ref, v_ref, bias_ref, o_ref, kwin_ref, vwin_ref, *, g):
    cstep = pl.program_id(1)
    rows = g * CHUNK
    win = (N_BACK + g) * CHUNK

    @pl.when(cstep == 0)
    def _():
        kwin_ref[...] = jnp.zeros(kwin_ref.shape, BF16)
        vwin_ref[...] = jnp.zeros(vwin_ref.shape, BF16)

    for i in range(N_BACK // g):
        kwin_ref[i * rows:(i + 1) * rows, :] = kwin_ref[(i + 1) * rows:(i + 2) * rows, :]
        vwin_ref[i * rows:(i + 1) * rows, :] = vwin_ref[(i + 1) * rows:(i + 2) * rows, :]
    kwin_ref[win - rows:win, :] = k_ref[...].astype(BF16)
    vwin_ref[win - rows:win, :] = v_ref[...].astype(BF16)

    slot_chunk = lax.broadcasted_iota(jnp.int32, (2 * rows, win), 1) // CHUNK
    in_seq = slot_chunk >= N_BACK - cstep * g
    pairs = range(H_B // 2)
    sls = [slice(p * LANES, (p + 1) * LANES) for p in pairs]
    s = [_mm_nt(_pair_queries(q_ref[:, sl] * (HD_B ** -0.5)), kwin_ref[:, sl]) for sl in sls]
    s = [jnp.where(in_seq, sp + bias_ref[p], NEG_BIG) for p, sp in zip(pairs, s)]
    m = [jnp.max(sp, axis=-1, keepdims=True) for sp in s]
    e = [jnp.exp(sp - mp) for sp, mp in zip(s, m)]
    l = [jnp.sum(ep, axis=-1, keepdims=True) for ep in e]
    o = [jnp.dot(ep.astype(BF16), vwin_ref[:, sl], preferred_element_type=F32) for ep, sl in zip(e, sls)]
    for sl, op, lp in zip(sls, o, l):
        o_ref[:, sl] = _unpair(op / lp, rows)


def _attend_prompt(qkvb, bias):
    b, t, _ = qkvb.shape
    g = ATT_G
    rows = g * CHUNK
    win = (N_BACK + g) * CHUNK
    kern = functools.partial(_attn_prompt_kernel, g=g)
    return pl.pallas_call(
        kern,
        grid=(b, t // rows),
        in_specs=[
            pl.BlockSpec((None, rows, ATT_DIM_B), lambda i, j: (i, j, 0)),
            pl.BlockSpec((None, rows, ATT_DIM_B), lambda i, j: (i, j, 1)),
            pl.BlockSpec((None, rows, ATT_DIM_B), lambda i, j: (i, j, 2)),
            pl.BlockSpec(bias.shape, lambda i, j: (0, 0, 0)),
        ],
        out_specs=pl.BlockSpec((None, rows, ATT_DIM_B), lambda i, j: (i, j, 0)),
        out_shape=jax.ShapeDtypeStruct((b, t, ATT_DIM_B), F32),
        scratch_shapes=[pltpu.VMEM((win, ATT_DIM_B), BF16), pltpu.VMEM((win, ATT_DIM_B), BF16)],
        compiler_params=_cparams(("arbitrary", "arbitrary")),
        name="attn_prompt",
    )(qkvb, qkvb, qkvb, bias)


def _attn_sample_kernel(q_ref, kn_ref, vn_ref, kc_ref, vc_ref, biasc_ref, biasn_ref, o_ref):
    ds = q_ref.shape[0]
    for p in range(H_B // 2):
        sl = slice(p * LANES, (p + 1) * LANES)
        q2 = _pair_queries(q_ref[:, sl] * (HD_B ** -0.5))
        s1 = _mm_nt(q2, kc_ref[:, sl]) + biasc_ref[p]
        s2 = _mm_nt(q2, kn_ref[:, sl]) + biasn_ref[p]
        m = jnp.maximum(jnp.max(s1, axis=-1, keepdims=True), jnp.max(s2, axis=-1, keepdims=True))
        e1 = jnp.exp(s1 - m)
        e2 = jnp.exp(s2 - m)
        l = jnp.sum(e1, axis=-1, keepdims=True) + jnp.sum(e2, axis=-1, keepdims=True)
        o = (_mm(e1, vc_ref[:, sl]) + _mm(e2, vn_ref[:, sl])) / l
        o_ref[:, sl] = _unpair(o, ds)


def _attend_sample(qkvb, cache_k, cache_v, biasc, biasn):
    b, ds, _ = qkvb.shape
    past = cache_k.shape[1]
    return pl.pallas_call(
        _attn_sample_kernel,
        grid=(b,),
        in_specs=[
            pl.BlockSpec((None, ds, ATT_DIM_B), lambda i: (i, 0, 0)),
            pl.BlockSpec((None, ds, ATT_DIM_B), lambda i: (i, 0, 1)),
            pl.BlockSpec((None, ds, ATT_DIM_B), lambda i: (i, 0, 2)),
            pl.BlockSpec((None, past, ATT_DIM_B), lambda i: (i, 0, 0)),
            pl.BlockSpec((None, past, ATT_DIM_B), lambda i: (i, 0, 0)),
            pl.BlockSpec(biasc.shape, lambda i: (0, 0, 0)),
            pl.BlockSpec(biasn.shape, lambda i: (0, 0, 0)),
        ],
        out_specs=pl.BlockSpec((None, ds, ATT_DIM_B), lambda i: (i, 0, 0)),
        out_shape=jax.ShapeDtypeStruct((b, ds, ATT_DIM_B), F32),
        compiler_params=_cparams(("arbitrary",)),
        name="attn_sample",
    )(qkvb, qkvb, qkvb, cache_k, cache_v, biasc, biasn)


def _router_kernel(oap_ref, obp_ref, xp_ref, oas_ref, obs_ref, xs_ref,
                   woa_ref, wob_ref, gffn_ref, wrt_ref, brt_ref, utri_ref,
                   x1_ref, h2_ref, idx_ref, gate_ref, rank_ref, cnt_ref,
                   oa_s, ob_s, x_s, carry_s, *, ntp):
    i = pl.program_id(0)
    tm = x_s.shape[0]

    @pl.when(i == 0)
    def _():
        carry_s[...] = jnp.zeros(carry_s.shape, F32)

    @pl.when(i < ntp)
    def _():
        oa_s[...] = oap_ref[...]
        ob_s[...] = obp_ref[...]
        x_s[...] = xp_ref[...]

    @pl.when(i >= ntp)
    def _():
        oa_s[...] = oas_ref[...]
        ob_s[...] = obs_ref[...]
        x_s[...] = xs_ref[...]

    x1 = x_s[...] + _mm(oa_s[...], woa_ref[...]) + _mm(ob_s[...], wob_ref[...])
    x1_ref[...] = x1
    h2 = _rms(x1, gffn_ref[...])
    h2_ref[...] = h2
    logits = _mm_nt(wrt_ref[...], h2) + brt_ref[:, 0:1]

    eidx = lax.broadcasted_iota(jnp.int32, (N_EXPERTS, tm), 0).astype(F32)
    vals = logits
    memf = jnp.zeros((N_EXPERTS, tm), F32)
    tops, sels = [], []
    for k in range(TOP_K):
        m = jnp.max(vals, axis=0, keepdims=True)
        ix = jnp.min(jnp.where(vals == m, eidx, float(N_EXPERTS)), axis=0, keepdims=True)
        sel = eidx == ix
        tops.append(m)
        sels.append(sel)
        idx_ref[k:k + 1, :] = ix.astype(jnp.int32)
        vals = jnp.where(sel, -jnp.inf, vals)
        memf = memf + jnp.where(sel, 1.0, 0.0)
    es = [jnp.exp(v - tops[0]) for v in tops]
    denom = es[0] + es[1] + es[2] + es[3]
    for k in range(TOP_K):
        gate_ref[k:k + 1, :] = es[k] / denom

    carry = carry_s[:, 0:1]
    excl = _mm(memf, utri_ref[...]) + carry
    for k in range(TOP_K):
        r = jnp.sum(jnp.where(sels[k], excl, 0.0), axis=0, keepdims=True)
        rank_ref[k:k + 1, :] = r.astype(jnp.int32)
    new_carry = carry + jnp.sum(memf, axis=1, keepdims=True)
    carry_s[...] = jnp.broadcast_to(new_carry, carry_s.shape)
    cnt_ref[...] = jnp.broadcast_to(new_carry, cnt_ref.shape)


def _out_router(oa_p, ob_p, x_p, oa_s, ob_s, x_s, woa, wob, gffn, wrt, brt):
    tp, ts = x_p.shape[0], x_s.shape[0]
    tm = min(PROJ_TM, ts, tp)
    ntp, nts = tp // tm, ts // tm
    ttot = tp + ts
    utri = jnp.triu(jnp.ones((tm, tm), BF16), k=1)
    pmap = lambda i: (jnp.minimum(i, ntp - 1), 0)
    smap = lambda i: (jnp.maximum(i - ntp, 0), 0)
    const = lambda i: (0, 0)
    kern = functools.partial(_router_kernel, ntp=ntp)
    return pl.pallas_call(
        kern,
        grid=(ntp + nts,),
        in_specs=[
            pl.BlockSpec((tm, VAL_DIM_A), pmap),
            pl.BlockSpec((tm, ATT_DIM_B), pmap),
            pl.BlockSpec((tm, D_MODEL), pmap),
            pl.BlockSpec((tm, VAL_DIM_A), smap),
            pl.BlockSpec((tm, ATT_DIM_B), smap),
            pl.BlockSpec((tm, D_MODEL), smap),
            pl.BlockSpec(woa.shape, const),
            pl.BlockSpec(wob.shape, const),
            pl.BlockSpec(gffn.shape, const),
            pl.BlockSpec(wrt.shape, const),
            pl.BlockSpec(brt.shape, const),
            pl.BlockSpec(utri.shape, const),
        ],
        out_specs=[
            pl.BlockSpec((tm, D_MODEL), lambda i: (i, 0)),
            pl.BlockSpec((tm, D_MODEL), lambda i: (i, 0)),
            pl.BlockSpec((TOP_K, tm), lambda i: (0, i)),
            pl.BlockSpec((TOP_K, tm), lambda i: (0, i)),
            pl.BlockSpec((TOP_K, tm), lambda i: (0, i)),
            pl.BlockSpec((N_EXPERTS, LANES), const),
        ],
        out_shape=[
            jax.ShapeDtypeStruct((ttot, D_MODEL), F32),
            jax.ShapeDtypeStruct((ttot, D_MODEL), F32),
            jax.ShapeDtypeStruct((TOP_K, ttot), jnp.int32),
            jax.ShapeDtypeStruct((TOP_K, ttot), F32),
            jax.ShapeDtypeStruct((TOP_K, ttot), jnp.int32),
            jax.ShapeDtypeStruct((N_EXPERTS, LANES), F32),
        ],
        scratch_shapes=[
            pltpu.VMEM((tm, VAL_DIM_A), F32),
            pltpu.VMEM((tm, ATT_DIM_B), F32),
            pltpu.VMEM((tm, D_MODEL), F32),
            pltpu.VMEM((N_EXPERTS, LANES), F32),
        ],
        compiler_params=_cparams(("arbitrary",)),
        name="out_router",
    )(oa_p, ob_p, x_p, oa_s, ob_s, x_s, woa, wob, gffn, wrt, brt, utri)


def _row_copy(src_ref, s, dst_ref, d, sem):
    return pltpu.make_async_copy(src_ref.at[pl.ds(s, 1), :], dst_ref.at[pl.ds(d, 1), :], sem)


def _dispatch_kernel(padlo_ref, padhi_ref, dest_ref, h2_ref, xs_ref, zrow_ref, sem):
    i = pl.program_id(0)
    tm = h2_ref.shape[0]

    def issue(t, carry):
        for k in range(TOP_K):
            _row_copy(h2_ref, t, xs_ref, dest_ref[k, t], sem).start(priority=k % 2)
        return carry

    lax.fori_loop(0, tm, issue, 0, unroll=8)
    for k in range(TOP_K):
        pltpu.make_async_copy(h2_ref, xs_ref.at[pl.ds(0, tm), :], sem).wait()

    @pl.when(i == pl.num_programs(0) - 1)
    def _():
        zrow_ref[...] = jnp.zeros(zrow_ref.shape, F32)
        for e in range(N_EXPERTS):
            lo = padlo_ref[e]
            hi = padhi_ref[e]

            def zissue(r, carry):
                _row_copy(zrow_ref, 0, xs_ref, r, sem).start()
                return carry

            lax.fori_loop(lo, hi, zissue, 0)

            def zdrain(r, carry):
                _row_copy(zrow_ref, 0, xs_ref, 0, sem).wait()
                return carry

            lax.fori_loop(lo, hi, zdrain, 0)


def _dispatch(padlo, padhi, dest, h2, n_rows, tm_rows):
    ttot = h2.shape[0]
    tm = tm_rows
    grid_spec = pltpu.PrefetchScalarGridSpec(
        num_scalar_prefetch=2,
        grid=(ttot // tm,),
        in_specs=[
            pl.BlockSpec((TOP_K, tm), lambda i, lo, hi: (0, i), memory_space=pltpu.SMEM),
            pl.BlockSpec((tm, D_MODEL), lambda i, lo, hi: (i, 0)),
        ],
        out_specs=pl.BlockSpec(memory_space=pl.ANY),
        scratch_shapes=[pltpu.VMEM((SUBLANES, D_MODEL), F32), pltpu.SemaphoreType.DMA(())],
    )
    return pl.pallas_call(
        _dispatch_kernel,
        grid_spec=grid_spec,
        out_shape=jax.ShapeDtypeStruct((n_rows, D_MODEL), F32),
        compiler_params=_cparams(("arbitrary",)),
        name="dispatch",
    )(padlo, padhi, dest, h2)


def _expert_kernel(bexp_ref, nused_ref, x_ref, wgu_ref, bgu_ref, wd_ref, bd_ref, y_ref, wgu_s, wd_s):
    j = pl.program_id(0)
    active = j < nused_ref[0]
    new_expert = jnp.logical_or(j == 0, bexp_ref[j] != bexp_ref[jnp.maximum(j - 1, 0)])

    @pl.when(jnp.logical_and(active, new_expert))
    def _():
        rows = 128
        for r in range(0, D_MODEL, rows):
            wgu_s[r:r + rows, :] = wgu_ref[r:r + rows, :].astype(BF16)
        for r in range(0, D_FF, rows):
            wd_s[r:r + rows, :] = wd_ref[r:r + rows, :].astype(BF16)

    @pl.when(active)
    def _():
        gu = _mm(x_ref[...], wgu_s[...]) + bgu_ref[...]
        gate = jnp.minimum(gu[:, 0:D_FF], SWIGLU_LIMIT)
        up = jnp.clip(gu[:, D_FF:], -SWIGLU_LIMIT, SWIGLU_LIMIT)
        hid = (up + 1.0) * (gate * _sigmoid(gate * SWIGLU_ALPHA))
        y_ref[...] = _mm(hid, wd_s[...]) + bd_ref[...]


def _experts(bexp, nused, xs, wgu, bgu, wd, bd):
    n_rows = xs.shape[0]
    bm = MOE_BM
    nb = n_rows // bm
    row_map = lambda j, be, nu: (jnp.minimum(j, nu[0] - 1), 0)
    w_map = lambda j, be, nu: (be[j], 0, 0)
    grid_spec = pltpu.PrefetchScalarGridSpec(
        num_scalar_prefetch=2,
        grid=(nb,),
        in_specs=[
            pl.BlockSpec((bm, D_MODEL), row_map),
            pl.BlockSpec((None, D_MODEL, 2 * D_FF), w_map),
            pl.BlockSpec((None, 1, 2 * D_FF), w_map),
            pl.BlockSpec((None, D_FF, D_MODEL), w_map),
            pl.BlockSpec((None, 1, D_MODEL), w_map),
        ],
        out_specs=pl.BlockSpec((bm, D_MODEL), row_map),
        scratch_shapes=[pltpu.VMEM((D_MODEL, 2 * D_FF), BF16), pltpu.VMEM((D_FF, D_MODEL), BF16)],
    )
    return pl.pallas_call(
        _expert_kernel,
        grid_spec=grid_spec,
        out_shape=jax.ShapeDtypeStruct((n_rows, D_MODEL), F32),
        compiler_params=_cparams(("arbitrary",)),
        name="experts",
    )(bexp, nused, xs, wgu, bgu, wd, bd)


def _combine_kernel(dest_ref, gate_ref, x1_ref, pp_ref, ps_ref, yb_ref, gple_ref, wg_ref, wp_ref,
                    yp_ref, ys_ref, buf_ref, sem, *, ntp):
    i = pl.program_id(0)
    tm = x1_ref.shape[0]

    def issue(t, carry):
        for k in range(TOP_K):
            pltpu.make_async_copy(yb_ref.at[pl.ds(dest_ref[k, t], 1), :],
                                  buf_ref.at[k, pl.ds(t, 1), :], sem).start(priority=k % 2)
        return carry

    lax.fori_loop(0, tm, issue, 0, unroll=8)
    for k in range(TOP_K):
        pltpu.make_async_copy(yb_ref.at[pl.ds(0, tm), :], buf_ref.at[k], sem).wait()

    gates = gate_ref[...]
    moe = buf_ref[0] * gates[:, 0:1]
    for k in range(1, TOP_K):
        moe = moe + buf_ref[k] * gates[:, k:k + 1]
    x2 = x1_ref[...] + moe
    gate = _sigmoid(_mm(_rms(x2, gple_ref[...]), wg_ref[...]))

    @pl.when(i < ntp)
    def _():
        yp_ref[...] = x2 + gate * _mm(pp_ref[...], wp_ref[...])

    @pl.when(i >= ntp)
    def _():
        ys_ref[...] = x2 + gate * _mm(ps_ref[...], wp_ref[...])


def _combine(dest, gates_col, x1, p_p, p_s, yb, gple, wg, wp):
    tp, ts = p_p.shape[0], p_s.shape[0]
    tm = min(COMB_TM, tp, ts)
    ntp, nts = tp // tm, ts // tm
    pmap = lambda i: (jnp.minimum(i, ntp - 1), 0)
    smap = lambda i: (jnp.maximum(i - ntp, 0), 0)
    const = lambda i: (0, 0)
    kern = functools.partial(_combine_kernel, ntp=ntp)
    return pl.pallas_call(
        kern,
        grid=(ntp + nts,),
        in_specs=[
            pl.BlockSpec((TOP_K, tm), lambda i: (0, i), memory_space=pltpu.SMEM),
            pl.BlockSpec((tm, TOP_K), lambda i: (i, 0)),
            pl.BlockSpec((tm, D_MODEL), lambda i: (i, 0)),
            pl.BlockSpec((tm, PLE_DIM), pmap),
            pl.BlockSpec((tm, PLE_DIM), smap),
            pl.BlockSpec(memory_space=pl.ANY),
            pl.BlockSpec(gple.shape, const),
            pl.BlockSpec(wg.shape, const),
            pl.BlockSpec(wp.shape, const),
        ],
        out_specs=[
            pl.BlockSpec((tm, D_MODEL), pmap),
            pl.BlockSpec((tm, D_MODEL), smap),
        ],
        out_shape=[
            jax.ShapeDtypeStruct((tp, D_MODEL), F32),
            jax.ShapeDtypeStruct((ts, D_MODEL), F32),
        ],
        scratch_shapes=[pltpu.VMEM((TOP_K, tm, D_MODEL), F32), pltpu.SemaphoreType.DMA(())],
        compiler_params=_cparams(("arbitrary",)),
        name="combine",
    )(dest, gates_col, x1, p_p, p_s, yb, gple, wg, wp)


def _bias_tables(rel_bias, n_q, n_k, key_offset, band):
    i = np.arange(n_q)[:, None]
    j = np.arange(n_k)[None, :]
    d_max = n_q - 1 + key_offset
    d_min = key_offset - (n_k - 1)
    n_hi = max(0, d_max - REL_MAX)
    n_lo = max(0, -REL_MAX - d_min)
    mid = rel_bias[:, max(d_min, -REL_MAX) + REL_MAX:min(d_max, REL_MAX) + REL_MAX + 1][:, ::-1]
    e = jnp.concatenate([jnp.broadcast_to(rel_bias[:, 2 * REL_MAX:], (H_B, n_hi)), mid,
                         jnp.broadcast_to(rel_bias[:, 0:1], (H_B, n_lo))], axis=1)
    period = n_q + n_k
    e = jnp.pad(e, ((0, 0), (0, 1)))
    flat = jnp.tile(e, (1, n_q + 1))[:, 0:n_q * (period + 1)]
    tab = flat.reshape(H_B, n_q, period + 1)[:, ::-1, 0:n_k]
    if band:
        qc = i // CHUNK
        sc = j // CHUNK
        ok = (sc >= qc) & (sc <= qc + N_BACK)
        tab = jnp.where(jnp.asarray(ok)[None], tab, NEG_BIG)
    return tab.reshape(H_B // 2, 2 * n_q, n_k)


def kernel(x_prompt, x_sample, state_delta, state_conv, cache_k, cache_v, p_prompt, p_sample, g_mix, w_in, conv_w, a_log, dt_bias, g_onorm, g_qnorm, g_knorm, rel_bias, w_out, g_ffn, w_router, b_router, w_gu, b_gu, w_down, b_down, g_ple, w_ple_gate, w_ple_proj):
    assert w_in.shape[0] == 1, "single-layer kernel"
    bp, tp_len, _ = x_prompt.shape
    bs, ts_len, _ = x_sample.shape
    tp, ts = bp * tp_len, bs * ts_len
    past = cache_k.shape[2]

    w = w_in[0]
    o_z = CONV_DIM + VAL_DIM_A
    o_qb = o_z + 2 * H_A
    wa = w[:, 0:o_z].astype(BF16)
    wba = jnp.pad(w[:, o_z:o_qb], ((0, 0), (0, LANES - 2 * H_A))).astype(BF16)
    wb = w[:, o_qb:].astype(BF16)
    grp = np.arange(ATT_DIM_B) // HD_B
    bd = jnp.asarray(grp[:, None] == grp[None, :], BF16)
    gq = jnp.tile(g_qnorm[0], H_B)[None, :]
    gk = jnp.tile(g_knorm[0], H_B)[None, :]
    gmix = g_mix[0][None, :]
    gon = g_onorm[0][None, :]
    convw = conv_w[0]
    woa = w_out[0, 0:VAL_DIM_A].astype(BF16)
    wob = w_out[0, VAL_DIM_A:].astype(BF16)
    gffn = g_ffn[0][None, :]
    wrt = w_router[0].T.astype(BF16)
    brt = jnp.broadcast_to(b_router[0][:, None], (N_EXPERTS, LANES))
    wgu = w_gu[0]
    bgu = b_gu[0][:, None, :]
    wd = w_down[0]
    bdn = b_down[0][:, None, :]
    gple = g_ple[0][None, :]
    wg = w_ple_gate[0].astype(BF16)
    wp = w_ple_proj[0].astype(BF16)

    xp2 = x_prompt.reshape(tp, D_MODEL)
    xs2 = x_sample.reshape(ts, D_MODEL)

    qkvz_p, ba_p, qkvb_p = _project(xp2, gmix, wa, wba, wb, bd, gq, gk)
    qkvz_s, ba_s, qkvb_s = _project(xs2, gmix, wa, wba, wb, bd, gq, gk)

    hist_pad = ((0, 0), (SUBLANES - (CONV_W - 1), 0), (0, 0))
    cbuf_p = jnp.zeros((bp, SUBLANES, CONV_DIM), F32)
    cbuf_s = jnp.pad(state_conv[0], hist_pad)
    s0_p = jnp.zeros((bp, H_A * DK_A, DV_A), F32)
    s0_s = state_delta[0].reshape(bs, H_A * DK_A, DV_A)
    oa_p, sfin_p, cnew_p = _delta_mixer(qkvz_p.reshape(bp, tp_len, -1), ba_p.reshape(bp, tp_len, LANES),
                                        cbuf_p, s0_p, convw, a_log[0], dt_bias[0], gon)
    oa_s, sfin_s, cnew_s = _delta_mixer(qkvz_s.reshape(bs, ts_len, -1), ba_s.reshape(bs, ts_len, LANES),
                                        cbuf_s, s0_s, convw, a_log[0], dt_bias[0], gon)
    sfin_p = sfin_p.reshape(bp, H_A, DK_A, DV_A)
    sfin_s = sfin_s.reshape(bs, H_A, DK_A, DV_A)

    qkvb_p3 = qkvb_p.reshape(bp, tp_len, 3 * ATT_DIM_B)
    qkvb_s3 = qkvb_s.reshape(bs, ts_len, 3 * ATT_DIM_B)
    bias_p = _bias_tables(rel_bias[0], ATT_G * CHUNK, (N_BACK + ATT_G) * CHUNK, BAND_PAST, True)
    ob_p = _attend_prompt(qkvb_p3, bias_p)
    bias_s = _bias_tables(rel_bias[0], ts_len, past + ts_len, past, False)
    ob_s = _attend_sample(qkvb_s3, cache_k[0].reshape(bs, past, ATT_DIM_B),
                          cache_v[0].reshape(bs, past, ATT_DIM_B),
                          bias_s[:, :, 0:past], bias_s[:, :, past:])

    x1, h2, idx, gates, rank, cnt = _out_router(
        oa_p.reshape(tp, VAL_DIM_A), ob_p.reshape(tp, ATT_DIM_B), xp2,
        oa_s.reshape(ts, VAL_DIM_A), ob_s.reshape(ts, ATT_DIM_B), xs2,
        woa, wob, gffn, wrt, brt)
    ttot = tp + ts
    bm = MOE_BM
    counts = cnt[:, 0].astype(jnp.int32)
    padded = (counts + bm - 1) // bm * bm
    pend = jnp.cumsum(padded)
    pstart = pend - padded
    eids = jnp.arange(N_EXPERTS, dtype=jnp.int32)
    start_of = jnp.sum(jnp.where(idx[None] == eids[:, None, None], pstart[:, None, None], 0), axis=0)
    dest = start_of + rank
    nb = -(-(ttot * TOP_K) // bm) + N_EXPERTS
    nused = (pend[-1] // bm).astype(jnp.int32)[None]
    first = jnp.minimum(jnp.arange(nb, dtype=jnp.int32), nused[0] - 1) * bm
    bexp = jnp.minimum(jnp.sum((pend[None, :] <= first[:, None]).astype(jnp.int32), axis=1),
                       N_EXPERTS - 1)

    xs_rows = _dispatch(pstart + counts, pend, dest, h2, nb * bm, min(COMB_TM, tp, ts))
    yb = _experts(bexp, nused, xs_rows, wgu, bgu, wd, bdn)
    y_p, y_s = _combine(dest, gates.T, x1, p_prompt[0].reshape(tp, PLE_DIM),
                        p_sample[0].reshape(ts, PLE_DIM), yb, gple, wg, wp)

    keep = min(BAND_PAST, tp_len)
    k_p = qkvb_p3[:, tp_len - keep:, ATT_DIM_B:2 * ATT_DIM_B].reshape(1, bp, keep, H_B, HD_B)
    v_p = qkvb_p3[:, tp_len - keep:, 2 * ATT_DIM_B:].reshape(1, bp, keep, H_B, HD_B)
    k_s = qkvb_s3[:, :, ATT_DIM_B:2 * ATT_DIM_B].reshape(1, bs, ts_len, H_B, HD_B)
    v_s = qkvb_s3[:, :, 2 * ATT_DIM_B:].reshape(1, bs, ts_len, H_B, HD_B)
    nconv = CONV_W - 1
    return (y_p.reshape(bp, tp_len, D_MODEL), y_s.reshape(bs, ts_len, D_MODEL),
            sfin_p[None], cnew_p[None, :, SUBLANES - nconv:, :], k_p, v_p,
            sfin_s[None], cnew_s[None, :, SUBLANES - nconv:, :], k_s, v_s)
```

```python
import functools

import jax
import jax.numpy as jnp
import numpy as np
from jax import lax
from jax.experimental import pallas as pl
from jax.experimental.pallas import tpu as pltpu

F32 = jnp.float32
BF16 = jnp.bfloat16

D_MODEL = 1024
CHUNK = 64
H_A = 4
DK_A = 128
DV_A = 128
CONV_W = 4
KEY_DIM_A = H_A * DK_A
VAL_DIM_A = H_A * DV_A
CONV_DIM = 2 * KEY_DIM_A + VAL_DIM_A
H_B = 8
HD_B = 64
ATT_DIM_B = H_B * HD_B
N_BACK = 8
BAND_PAST = N_BACK * CHUNK
REL_MAX = 256
N_EXPERTS = 32
TOP_K = 4
D_FF = 1024
SWIGLU_LIMIT = 7.0
SWIGLU_ALPHA = 1.702
PLE_DIM = 256
RMS_EPS = 1e-6
L2_EPS = 1e-6
NEG_BIG = -1e30

LANES = 128
SUBLANES = 8
VMEM_LIMIT = 56 * 1024 * 1024

PROJ_TM = 512
ATT_G = 2
DELTA_NCH = 4
MOE_BM = 512
COMB_TM = 256


def _mm(a, b):
    return jnp.dot(a.astype(BF16), b.astype(BF16), preferred_element_type=F32)


def _mm_nt(a, b):
    return lax.dot_general(a.astype(BF16), b.astype(BF16), (((1,), (1,)), ((), ())),
                           preferred_element_type=F32)


def _sigmoid(x):
    return 1.0 / (1.0 + jnp.exp(-x))


def _softplus(x):
    return jnp.maximum(x, 0.0) + jnp.log(1.0 + jnp.exp(-jnp.abs(x)))


def _rms(x, g):
    ms = jnp.mean(x * x, axis=-1, keepdims=True)
    return x * lax.rsqrt(ms + RMS_EPS) * g


NSLAB = D_MODEL // LANES


def _store_slabs(ref, val):
    n = val.shape[0]
    for s in range(NSLAB):
        ref[pl.ds(s, n, stride=NSLAB), :] = val[:, s * LANES:(s + 1) * LANES]


def _load_slabs(ref, n):
    return jnp.concatenate([ref[pl.ds(s, n, stride=NSLAB), :] for s in range(NSLAB)], axis=1)


def _slab_rows(ref, r):
    return ref.at[pl.ds(pl.multiple_of(r * NSLAB, NSLAB), NSLAB), :]


def _cparams(sem):
    return pltpu.CompilerParams(dimension_semantics=sem, vmem_limit_bytes=VMEM_LIMIT)


def _proj_kernel(x_ref, gmix_ref, wa_ref, wba_ref, wb_ref, bd_ref, gq_ref, gk_ref,
                 qkvz_ref, ba_ref, qkvb_ref):
    h = _rms(x_ref[...], gmix_ref[...]).astype(BF16)
    qkvz_ref[...] = jnp.dot(h, wa_ref[...], preferred_element_type=F32)
    ba_ref[...] = jnp.dot(h, wba_ref[...], preferred_element_type=F32)
    pb = jnp.dot(h, wb_ref[...], preferred_element_type=F32)
    bd = bd_ref[...]

    def head_norm(q, g):
        sq = q * q
        hi = sq.astype(BF16)
        lo = (sq - hi.astype(F32)).astype(BF16)
        ss = (jnp.dot(hi, bd, preferred_element_type=F32)
              + jnp.dot(lo, bd, preferred_element_type=F32))
        return q * lax.rsqrt(ss * (1.0 / HD_B) + RMS_EPS) * g

    qkvb_ref[:, 0:ATT_DIM_B] = head_norm(pb[:, 0:ATT_DIM_B], gq_ref[...])
    qkvb_ref[:, ATT_DIM_B:2 * ATT_DIM_B] = head_norm(pb[:, ATT_DIM_B:2 * ATT_DIM_B], gk_ref[...])
    qkvb_ref[:, 2 * ATT_DIM_B:] = pb[:, 2 * ATT_DIM_B:]


def _project(x2d, gmix, wa, wba, wb, bd, gq, gk):
    t = x2d.shape[0]
    tm = min(PROJ_TM, t)
    const = lambda i: (0, 0)
    return pl.pallas_call(
        _proj_kernel,
        grid=(t // tm,),
        in_specs=[
            pl.BlockSpec((tm, D_MODEL), lambda i: (i, 0)),
            pl.BlockSpec(gmix.shape, const),
            pl.BlockSpec(wa.shape, const),
            pl.BlockSpec(wba.shape, const),
            pl.BlockSpec(wb.shape, const),
            pl.BlockSpec(bd.shape, const),
            pl.BlockSpec(gq.shape, const),
            pl.BlockSpec(gk.shape, const),
        ],
        out_specs=[
            pl.BlockSpec((tm, wa.shape[1]), lambda i: (i, 0)),
            pl.BlockSpec((tm, LANES), lambda i: (i, 0)),
            pl.BlockSpec((tm, wb.shape[1]), lambda i: (i, 0)),
        ],
        out_shape=[
            jax.ShapeDtypeStruct((t, wa.shape[1]), F32),
            jax.ShapeDtypeStruct((t, LANES), F32),
            jax.ShapeDtypeStruct((t, wb.shape[1]), F32),
        ],
        compiler_params=_cparams(("arbitrary",)),
        name="proj",
    )(x2d, gmix, wa, wba, wb, bd, gq, gk)


def _delta_kernel(qkvz_ref, ba_ref, cbuf_ref, s0_ref, convw_ref, nega_ref, dtb_ref,
                  gon_ref, lbd_ref,
                  o_ref, sfin_ref, cnew_ref, xh_ref, s_ref, *, c, nch):
    step = pl.program_id(1)
    tile = c * nch
    hist = SUBLANES

    @pl.when(step == 0)
    def _():
        xh_ref[0:hist, :] = cbuf_ref[...]
        s_ref[...] = s0_ref[...]

    xh_ref[hist:hist + tile, :] = qkvz_ref[:, 0:CONV_DIM]
    w = convw_ref[...]
    conv = (xh_ref[hist - 3:hist - 3 + tile, :] * w[0:1, :]
            + xh_ref[hist - 2:hist - 2 + tile, :] * w[1:2, :]
            + xh_ref[hist - 1:hist - 1 + tile, :] * w[2:3, :]
            + xh_ref[hist:hist + tile, :] * w[3:4, :])
    conv = conv * _sigmoid(conv)
    last_rows = xh_ref[tile:tile + hist, :]
    cnew_ref[...] = last_rows
    xh_ref[0:hist, :] = last_rows

    ba = ba_ref[...]
    z_all = qkvz_ref[:, CONV_DIM:CONV_DIM + VAL_DIM_A]

    rs = H_A * c
    sk = H_A * DK_A
    row = lax.broadcasted_iota(jnp.int32, (rs, rs), 0)
    col = lax.broadcasted_iota(jnp.int32, (rs, rs), 1)
    same = (row // c) == (col // c)
    causal = jnp.logical_and(same, row >= col)
    strict = jnp.logical_and(same, row > col)
    wrow = lax.broadcasted_iota(jnp.int32, (rs, sk), 0)
    wcol = lax.broadcasted_iota(jnp.int32, (rs, sk), 1)
    head_block = (wrow // c) == (wcol // DK_A)
    lbd = lbd_ref[...]
    nsq = int(np.log2(c))

    def stack(fn):
        return jnp.concatenate([fn(h) for h in range(H_A)], axis=0)

    def spread(m):
        return jnp.where(head_block, jnp.concatenate([m] * H_A, axis=1), 0.0)

    def mm_split3(a_bf16, x):
        hi = x.astype(BF16)
        r1 = x - hi.astype(F32)
        mid = r1.astype(BF16)
        lo = (r1 - mid.astype(F32)).astype(BF16)
        return (jnp.dot(a_bf16, hi, preferred_element_type=F32)
                + jnp.dot(a_bf16, mid, preferred_element_type=F32)
                + jnp.dot(a_bf16, lo, preferred_element_type=F32))

    def l2n(m):
        return m * lax.rsqrt(jnp.sum(m * m, axis=-1, keepdims=True) + L2_EPS)

    def last_row(m, h, shape):
        return jnp.broadcast_to(m[h * c + c - 1:h * c + c, :], shape)

    cs = range(nch)
    q_st = [l2n(stack(lambda h: conv[ci * c:ci * c + c, h * DK_A:(h + 1) * DK_A])) * (DK_A ** -0.5)
            for ci in cs]
    k_st = [l2n(stack(lambda h: conv[ci * c:ci * c + c, KEY_DIM_A + h * DK_A:KEY_DIM_A + (h + 1) * DK_A]))
            for ci in cs]
    v_st = [stack(lambda h: conv[ci * c:ci * c + c, 2 * KEY_DIM_A + h * DV_A:2 * KEY_DIM_A + (h + 1) * DV_A])
            for ci in cs]
    beta = [_sigmoid(stack(lambda h: jnp.broadcast_to(ba[ci * c:ci * c + c, h:h + 1], (c, LANES))))
            for ci in cs]
    g = [nega_ref[...] * _softplus(
        stack(lambda h: jnp.broadcast_to(ba[ci * c:ci * c + c, H_A + h:H_A + h + 1], (c, LANES)))
        + dtb_ref[...]) for ci in cs]
    gc = [mm_split3(lbd, gi) for gi in g]
    kb = [k * b for k, b in zip(k_st, beta)]
    kk = [_mm_nt(a, b) for a, b in zip(kb, k_st)]
    qk = [_mm_nt(a, b) for a, b in zip(q_st, k_st)]
    decay = [jnp.exp(jnp.where(causal,
                               jnp.concatenate([m] * (rs // LANES), axis=1)
                               - jnp.broadcast_to(m.T[0:1, :], (rs, rs)),
                               -jnp.inf)) for m in gc]
    egc = [jnp.exp(m) for m in gc]
    ps = [-jnp.where(strict, a * d, 0.0) for a, d in zip(kk, decay)]
    qk = [a * d for a, d in zip(qk, decay)]
    xs = [jnp.concatenate([v * b, k * e], axis=-1) for v, b, k, e in zip(v_st, beta, kb, egc)]
    for s in range(nsq):
        xs = [x + _mm(p, x) for x, p in zip(xs, ps)]
        if s + 1 < nsq:
            ps = [_mm(p, p) for p in ps]
    qbd = [spread(q * e) for q, e in zip(q_st, egc)]
    kbd = [spread(k * jnp.exp(stack(lambda h: last_row(m, h, (c, LANES))) - m)) for k, m in zip(k_st, gc)]
    egl = [stack(lambda h: last_row(e, h, (DK_A, DV_A))) for e in egc]

    for ci in range(nch):
        r0 = ci * c
        eglast = egl[ci]
        u0 = xs[ci][:, 0:DV_A]
        wbd = spread(xs[ci][:, DV_A:])
        s_all = s_ref[...]
        u = u0 - _mm(wbd, s_all)
        o = _mm(qbd[ci], s_all) + _mm(qk[ci], u)
        s_ref[...] = s_all * eglast + lax.dot_general(
            kbd[ci].astype(BF16), u.astype(BF16), (((0,), (0,)), ((), ())), preferred_element_type=F32)
        on = _rms(o, gon_ref[...])
        for h in range(H_A):
            z = z_all[r0:r0 + c, h * DV_A:(h + 1) * DV_A]
            o_ref[r0:r0 + c, h * DV_A:(h + 1) * DV_A] = on[h * c:(h + 1) * c, :] * (z * _sigmoid(z))

    sfin_ref[...] = s_ref[...]


def _delta_mixer(qkvz, ba, cbuf, s0, convw, a_log, dt_bias, gon):
    b, t, _ = qkvz.shape
    c = min(CHUNK, t)
    nch = min(DELTA_NCH, t // c)
    tile = c * nch
    rs = H_A * c
    sk = H_A * DK_A
    assert rs % LANES == 0
    r = np.arange(rs)
    lbd = jnp.asarray((r[:, None] // c == r[None, :] // c) & (r[:, None] >= r[None, :]), BF16)
    nega = jnp.broadcast_to(jnp.repeat(-jnp.exp(a_log), c)[:, None], (rs, LANES))
    dtb = jnp.broadcast_to(jnp.repeat(dt_bias, c)[:, None], (rs, LANES))
    const2 = lambda i, j: (0, 0)
    kern = functools.partial(_delta_kernel, c=c, nch=nch)
    return pl.pallas_call(
        kern,
        grid=(b, t // tile),
        in_specs=[
            pl.BlockSpec((None, tile, qkvz.shape[2]), lambda i, j: (i, j, 0)),
            pl.BlockSpec((None, tile, LANES), lambda i, j: (i, j, 0)),
            pl.BlockSpec((None, SUBLANES, CONV_DIM), lambda i, j: (i, 0, 0)),
            pl.BlockSpec((None, sk, DV_A), lambda i, j: (i, 0, 0)),
            pl.BlockSpec(convw.shape, const2),
            pl.BlockSpec(nega.shape, const2),
            pl.BlockSpec(dtb.shape, const2),
            pl.BlockSpec(gon.shape, const2),
            pl.BlockSpec(lbd.shape, const2),
        ],
        out_specs=[
            pl.BlockSpec((None, tile, VAL_DIM_A), lambda i, j: (i, j, 0)),
            pl.BlockSpec((None, sk, DV_A), lambda i, j: (i, 0, 0)),
            pl.BlockSpec((None, SUBLANES, CONV_DIM), lambda i, j: (i, 0, 0)),
        ],
        out_shape=[
            jax.ShapeDtypeStruct((b, t, VAL_DIM_A), F32),
            jax.ShapeDtypeStruct((b, sk, DV_A), F32),
            jax.ShapeDtypeStruct((b, SUBLANES, CONV_DIM), F32),
        ],
        scratch_shapes=[
            pltpu.VMEM((SUBLANES + tile, CONV_DIM), F32),
            pltpu.VMEM((sk, DV_A), F32),
        ],
        compiler_params=_cparams(("arbitrary", "arbitrary")),
        name="delta",
    )(qkvz, ba, cbuf, s0, convw, nega, dtb, gon, lbd)


def _pair_queries(qp):
    lane = lax.broadcasted_iota(jnp.int32, qp.shape, 1)
    q_even = jnp.where(lane < HD_B, qp, 0.0)
    q_odd = jnp.where(lane >= HD_B, qp, 0.0)
    return jnp.concatenate([q_even, q_odd], axis=0).astype(BF16)


def _unpair(o, r):
    lane = lax.broadcasted_iota(jnp.int32, (r, LANES), 1)
    return jnp.where(lane < HD_B, o[0:r, :], o[r:2 * r, :])


def _attn_prompt_kernel(q_ref, k_ref, v_ref, bias_ref, o_ref, kwin_ref, vwin_ref, *, g):
    cstep = pl.program_id(1)
    rows = g * CHUNK
    win = (N_BACK + g) * CHUNK

    @pl.when(cstep == 0)
    def _():
        kwin_ref[...] = jnp.zeros(kwin_ref.shape, BF16)
        vwin_ref[...] = jnp.zeros(vwin_ref.shape, BF16)

    for i in range(N_BACK // g):
        kwin_ref[i * rows:(i + 1) * rows, :] = kwin_ref[(i + 1) * rows:(i + 2) * rows, :]
        vwin_ref[i * rows:(i + 1) * rows, :] = vwin_ref[(i + 1) * rows:(i + 2) * rows, :]
    kwin_ref[win - rows:win, :] = k_ref[...].astype(BF16)
    vwin_ref[win - rows:win, :] = v_ref[...].astype(BF16)

    slot_chunk = lax.broadcasted_iota(jnp.int32, (2 * rows, win), 1) // CHUNK
    in_seq = slot_chunk >= N_BACK - cstep * g
    pairs = range(H_B // 2)
    sls = [slice(p * LANES, (p + 1) * LANES) for p in pairs]
    s = [_mm_nt(_pair_queries(q_ref[:, sl] * (HD_B ** -0.5)), kwin_ref[:, sl]) for sl in sls]
    s = [jnp.where(in_seq, sp + bias_ref[p], NEG_BIG) for p, sp in zip(pairs, s)]
    m = [jnp.max(sp, axis=-1, keepdims=True) for sp in s]
    e = [jnp.exp(sp - mp) for sp, mp in zip(s, m)]
    l = [jnp.sum(ep, axis=-1, keepdims=True) for ep in e]
    o = [jnp.dot(ep.astype(BF16), vwin_ref[:, sl], preferred_element_type=F32) for ep, sl in zip(e, sls)]
    for sl, op, lp in zip(sls, o, l):
        o_ref[:, sl] = _unpair(op / lp, rows)


def _attend_prompt(qkvb, bias):
    b, t, _ = qkvb.shape
    g = ATT_G
    rows = g * CHUNK
    win = (N_BACK + g) * CHUNK
    kern = functools.partial(_attn_prompt_kernel, g=g)
    return pl.pallas_call(
        kern,
        grid=(b, t // rows),
        in_specs=[
            pl.BlockSpec((None, rows, ATT_DIM_B), lambda i, j: (i, j, 0)),
            pl.BlockSpec((None, rows, ATT_DIM_B), lambda i, j: (i, j, 1)),
            pl.BlockSpec((None, rows, ATT_DIM_B), lambda i, j: (i, j, 2)),
            pl.BlockSpec(bias.shape, lambda i, j: (0, 0, 0)),
        ],
        out_specs=pl.BlockSpec((None, rows, ATT_DIM_B), lambda i, j: (i, j, 0)),
        out_shape=jax.ShapeDtypeStruct((b, t, ATT_DIM_B), F32),
        scratch_shapes=[pltpu.VMEM((win, ATT_DIM_B), BF16), pltpu.VMEM((win, ATT_DIM_B), BF16)],
        compiler_params=_cparams(("arbitrary", "arbitrary")),
        name="attn_prompt",
    )(qkvb, qkvb, qkvb, bias)


def _attn_sample_kernel(q_ref, kn_ref, vn_ref, kc_ref, vc_ref, biasc_ref, biasn_ref, o_ref):
    ds = q_ref.shape[0]
    for p in range(H_B // 2):
        sl = slice(p * LANES, (p + 1) * LANES)
        q2 = _pair_queries(q_ref[:, sl] * (HD_B ** -0.5))
        s1 = _mm_nt(q2, kc_ref[:, sl]) + biasc_ref[p]
        s2 = _mm_nt(q2, kn_ref[:, sl]) + biasn_ref[p]
        m = jnp.maximum(jnp.max(s1, axis=-1, keepdims=True), jnp.max(s2, axis=-1, keepdims=True))
        e1 = jnp.exp(s1 - m)
        e2 = jnp.exp(s2 - m)
        l = jnp.sum(e1, axis=-1, keepdims=True) + jnp.sum(e2, axis=-1, keepdims=True)
        o = (_mm(e1, vc_ref[:, sl]) + _mm(e2, vn_ref[:, sl])) / l
        o_ref[:, sl] = _unpair(o, ds)


def _attend_sample(qkvb, cache_k, cache_v, biasc, biasn):
    b, ds, _ = qkvb.shape
    past = cache_k.shape[1]
    return pl.pallas_call(
        _attn_sample_kernel,
        grid=(b,),
        in_specs=[
            pl.BlockSpec((None, ds, ATT_DIM_B), lambda i: (i, 0, 0)),
            pl.BlockSpec((None, ds, ATT_DIM_B), lambda i: (i, 0, 1)),
            pl.BlockSpec((None, ds, ATT_DIM_B), lambda i: (i, 0, 2)),
            pl.BlockSpec((None, past, ATT_DIM_B), lambda i: (i, 0, 0)),
            pl.BlockSpec((None, past, ATT_DIM_B), lambda i: (i, 0, 0)),
            pl.BlockSpec(biasc.shape, lambda i: (0, 0, 0)),
            pl.BlockSpec(biasn.shape, lambda i: (0, 0, 0)),
        ],
        out_specs=pl.BlockSpec((None, ds, ATT_DIM_B), lambda i: (i, 0, 0)),
        out_shape=jax.ShapeDtypeStruct((b, ds, ATT_DIM_B), F32),
        compiler_params=_cparams(("arbitrary",)),
        name="attn_sample",
    )(qkvb, qkvb, qkvb, cache_k, cache_v, biasc, biasn)


def _router_kernel(oap_ref, obp_ref, xp_ref, oas_ref, obs_ref, xs_ref,
                   woa_ref, wob_ref, gffn_ref, wrt_ref, brt_ref, utri_ref,
                   x1_ref, h2_ref, idx_ref, gate_ref, rank_ref, cnt_ref,
                   oa_s, ob_s, x_s, carry_s, *, ntp):
    i = pl.program_id(0)
    tm = x_s.shape[0]

    @pl.when(i == 0)
    def _():
        carry_s[...] = jnp.zeros(carry_s.shape, F32)

    @pl.when(i < ntp)
    def _():
        oa_s[...] = oap_ref[...]
        ob_s[...] = obp_ref[...]
        x_s[...] = xp_ref[...]

    @pl.when(i >= ntp)
    def _():
        oa_s[...] = oas_ref[...]
        ob_s[...] = obs_ref[...]
        x_s[...] = xs_ref[...]

    x1 = x_s[...] + _mm(oa_s[...], woa_ref[...]) + _mm(ob_s[...], wob_ref[...])
    x1_ref[...] = x1
    h2 = _rms(x1, gffn_ref[...])
    _store_slabs(h2_ref, h2)
    logits = _mm_nt(wrt_ref[...], h2) + brt_ref[:, 0:1]

    eidx = lax.broadcasted_iota(jnp.int32, (N_EXPERTS, tm), 0).astype(F32)
    vals = logits
    memf = jnp.zeros((N_EXPERTS, tm), F32)
    tops, sels = [], []
    for k in range(TOP_K):
        m = jnp.max(vals, axis=0, keepdims=True)
        ix = jnp.min(jnp.where(vals == m, eidx, float(N_EXPERTS)), axis=0, keepdims=True)
        sel = eidx == ix
        tops.append(m)
        sels.append(sel)
        idx_ref[k:k + 1, :] = ix.astype(jnp.int32)
        vals = jnp.where(sel, -jnp.inf, vals)
        memf = memf + jnp.where(sel, 1.0, 0.0)
    es = [jnp.exp(v - tops[0]) for v in tops]
    denom = es[0] + es[1] + es[2] + es[3]
    for k in range(TOP_K):
        gate_ref[k:k + 1, :] = es[k] / denom

    carry = carry_s[:, 0:1]
    excl = _mm(memf, utri_ref[...]) + carry
    for k in range(TOP_K):
        r = jnp.sum(jnp.where(sels[k], excl, 0.0), axis=0, keepdims=True)
        rank_ref[k:k + 1, :] = r.astype(jnp.int32)
    new_carry = carry + jnp.sum(memf, axis=1, keepdims=True)
    carry_s[...] = jnp.broadcast_to(new_carry, carry_s.shape)
    cnt_ref[...] = jnp.broadcast_to(new_carry, cnt_ref.shape)


def _out_router(oa_p, ob_p, x_p, oa_s, ob_s, x_s, woa, wob, gffn, wrt, brt):
    tp, ts = x_p.shape[0], x_s.shape[0]
    tm = min(PROJ_TM, ts, tp)
    ntp, nts = tp // tm, ts // tm
    ttot = tp + ts
    utri = jnp.triu(jnp.ones((tm, tm), BF16), k=1)
    pmap = lambda i: (jnp.minimum(i, ntp - 1), 0)
    smap = lambda i: (jnp.maximum(i - ntp, 0), 0)
    const = lambda i: (0, 0)
    kern = functools.partial(_router_kernel, ntp=ntp)
    return pl.pallas_call(
        kern,
        grid=(ntp + nts,),
        in_specs=[
            pl.BlockSpec((tm, VAL_DIM_A), pmap),
            pl.BlockSpec((tm, ATT_DIM_B), pmap),
            pl.BlockSpec((tm, D_MODEL), pmap),
            pl.BlockSpec((tm, VAL_DIM_A), smap),
            pl.BlockSpec((tm, ATT_DIM_B), smap),
            pl.BlockSpec((tm, D_MODEL), smap),
            pl.BlockSpec(woa.shape, const),
            pl.BlockSpec(wob.shape, const),
            pl.BlockSpec(gffn.shape, const),
            pl.BlockSpec(wrt.shape, const),
            pl.BlockSpec(brt.shape, const),
            pl.BlockSpec(utri.shape, const),
        ],
        out_specs=[
            pl.BlockSpec((tm, D_MODEL), lambda i: (i, 0)),
            pl.BlockSpec((tm * NSLAB, LANES), lambda i: (i, 0)),
            pl.BlockSpec((TOP_K, tm), lambda i: (0, i)),
            pl.BlockSpec((TOP_K, tm), lambda i: (0, i)),
            pl.BlockSpec((TOP_K, tm), lambda i: (0, i)),
            pl.BlockSpec((N_EXPERTS, LANES), const),
        ],
        out_shape=[
            jax.ShapeDtypeStruct((ttot, D_MODEL), F32),
            jax.ShapeDtypeStruct((ttot * NSLAB, LANES), F32),
            jax.ShapeDtypeStruct((TOP_K, ttot), jnp.int32),
            jax.ShapeDtypeStruct((TOP_K, ttot), F32),
            jax.ShapeDtypeStruct((TOP_K, ttot), jnp.int32),
            jax.ShapeDtypeStruct((N_EXPERTS, LANES), F32),
        ],
        scratch_shapes=[
            pltpu.VMEM((tm, VAL_DIM_A), F32),
            pltpu.VMEM((tm, ATT_DIM_B), F32),
            pltpu.VMEM((tm, D_MODEL), F32),
            pltpu.VMEM((N_EXPERTS, LANES), F32),
        ],
        compiler_params=_cparams(("arbitrary",)),
        name="out_router",
    )(oa_p, ob_p, x_p, oa_s, ob_s, x_s, woa, wob, gffn, wrt, brt, utri)


def _row_copy(src_ref, s, dst_ref, d, sem):
    return pltpu.make_async_copy(_slab_rows(src_ref, s), _slab_rows(dst_ref, d), sem)


def _dispatch_kernel(padlo_ref, padhi_ref, dest_ref, h2_ref, xs_ref, zrow_ref, sem):
    i = pl.program_id(0)
    tm = h2_ref.shape[0] // NSLAB

    def issue(t, carry):
        for k in range(TOP_K):
            _row_copy(h2_ref, t, xs_ref, dest_ref[k, t], sem).start(priority=k % 2)
        return carry

    lax.fori_loop(0, tm, issue, 0, unroll=8)
    for k in range(TOP_K):
        pltpu.make_async_copy(h2_ref, xs_ref.at[pl.ds(0, tm * NSLAB), :], sem).wait()

    @pl.when(i == pl.num_programs(0) - 1)
    def _():
        zrow_ref[...] = jnp.zeros(zrow_ref.shape, F32)
        for e in range(N_EXPERTS):
            lo = padlo_ref[e]
            hi = padhi_ref[e]

            def zissue(r, carry):
                _row_copy(zrow_ref, 0, xs_ref, r, sem).start()
                return carry

            lax.fori_loop(lo, hi, zissue, 0)

            def zdrain(r, carry):
                _row_copy(zrow_ref, 0, xs_ref, 0, sem).wait()
                return carry

            lax.fori_loop(lo, hi, zdrain, 0)


def _dispatch(padlo, padhi, dest, h2, n_rows, tm_rows):
    ttot = h2.shape[0] // NSLAB
    tm = tm_rows
    grid_spec = pltpu.PrefetchScalarGridSpec(
        num_scalar_prefetch=2,
        grid=(ttot // tm,),
        in_specs=[
            pl.BlockSpec((TOP_K, tm), lambda i, lo, hi: (0, i), memory_space=pltpu.SMEM),
            pl.BlockSpec((tm * NSLAB, LANES), lambda i, lo, hi: (i, 0)),
        ],
        out_specs=pl.BlockSpec(memory_space=pl.ANY),
        scratch_shapes=[pltpu.VMEM((NSLAB, LANES), F32), pltpu.SemaphoreType.DMA(())],
    )
    return pl.pallas_call(
        _dispatch_kernel,
        grid_spec=grid_spec,
        out_shape=jax.ShapeDtypeStruct((n_rows * NSLAB, LANES), F32),
        compiler_params=_cparams(("arbitrary",)),
        name="dispatch",
    )(padlo, padhi, dest, h2)


def _expert_kernel(bexp_ref, nused_ref, x_ref, wgu_ref, bgu_ref, wd_ref, bd_ref, y_ref, wgu_s, wd_s):
    j = pl.program_id(0)
    active = j < nused_ref[0]
    new_expert = jnp.logical_or(j == 0, bexp_ref[j] != bexp_ref[jnp.maximum(j - 1, 0)])

    @pl.when(jnp.logical_and(active, new_expert))
    def _():
        rows = 128
        for r in range(0, D_MODEL, rows):
            wgu_s[r:r + rows, :] = wgu_ref[r:r + rows, :].astype(BF16)
        for r in range(0, D_FF, rows):
            wd_s[r:r + rows, :] = wd_ref[r:r + rows, :].astype(BF16)

    @pl.when(active)
    def _():
        bm = x_ref.shape[0] // NSLAB
        gu = _mm(_load_slabs(x_ref, bm), wgu_s[...]) + bgu_ref[...]
        gate = jnp.minimum(gu[:, 0:D_FF], SWIGLU_LIMIT)
        up = jnp.clip(gu[:, D_FF:], -SWIGLU_LIMIT, SWIGLU_LIMIT)
        hid = (up + 1.0) * (gate * _sigmoid(gate * SWIGLU_ALPHA))
        _store_slabs(y_ref, _mm(hid, wd_s[...]) + bd_ref[...])


def _experts(bexp, nused, xs, wgu, bgu, wd, bd):
    n_rows = xs.shape[0] // NSLAB
    bm = MOE_BM
    nb = n_rows // bm
    row_map = lambda j, be, nu: (jnp.minimum(j, nu[0] - 1), 0)
    w_map = lambda j, be, nu: (be[j], 0, 0)
    grid_spec = pltpu.PrefetchScalarGridSpec(
        num_scalar_prefetch=2,
        grid=(nb,),
        in_specs=[
            pl.BlockSpec((bm * NSLAB, LANES), row_map),
            pl.BlockSpec((None, D_MODEL, 2 * D_FF), w_map),
            pl.BlockSpec((None, 1, 2 * D_FF), w_map),
            pl.BlockSpec((None, D_FF, D_MODEL), w_map),
            pl.BlockSpec((None, 1, D_MODEL), w_map),
        ],
        out_specs=pl.BlockSpec((bm * NSLAB, LANES), row_map),
        scratch_shapes=[pltpu.VMEM((D_MODEL, 2 * D_FF), BF16), pltpu.VMEM((D_FF, D_MODEL), BF16)],
    )
    return pl.pallas_call(
        _expert_kernel,
        grid_spec=grid_spec,
        out_shape=jax.ShapeDtypeStruct((n_rows * NSLAB, LANES), F32),
        compiler_params=_cparams(("arbitrary",)),
        name="experts",
    )(bexp, nused, xs, wgu, bgu, wd, bd)


def _combine_kernel(dest_ref, gate_ref, x1_ref, pp_ref, ps_ref, yb_ref, gple_ref, wg_ref, wp_ref,
                    yp_ref, ys_ref, buf_ref, sem, *, ntp):
    i = pl.program_id(0)
    tm = x1_ref.shape[0]

    def issue(t, carry):
        for k in range(TOP_K):
            _row_copy(yb_ref, dest_ref[k, t], buf_ref, k * tm + t, sem).start(priority=k % 2)
        return carry

    lax.fori_loop(0, tm, issue, 0, unroll=8)
    pltpu.make_async_copy(yb_ref.at[pl.ds(0, TOP_K * tm * NSLAB), :], buf_ref, sem).wait()

    gates = gate_ref[...]
    moe = None
    for k in range(TOP_K):
        rows_k = jnp.concatenate(
            [buf_ref[pl.ds(k * tm * NSLAB + s, tm, stride=NSLAB), :] for s in range(NSLAB)], axis=1)
        term = rows_k * gates[:, k:k + 1]
        moe = term if moe is None else moe + term
    x2 = x1_ref[...] + moe
    gate = _sigmoid(_mm(_rms(x2, gple_ref[...]), wg_ref[...]))

    @pl.when(i < ntp)
    def _():
        yp_ref[...] = x2 + gate * _mm(pp_ref[...], wp_ref[...])

    @pl.when(i >= ntp)
    def _():
        ys_ref[...] = x2 + gate * _mm(ps_ref[...], wp_ref[...])


def _combine(dest, gates_col, x1, p_p, p_s, yb, gple, wg, wp):
    tp, ts = p_p.shape[0], p_s.shape[0]
    tm = min(COMB_TM, tp, ts)
    ntp, nts = tp // tm, ts // tm
    pmap = lambda i: (jnp.minimum(i, ntp - 1), 0)
    smap = lambda i: (jnp.maximum(i - ntp, 0), 0)
    const = lambda i: (0, 0)
    kern = functools.partial(_combine_kernel, ntp=ntp)
    return pl.pallas_call(
        kern,
        grid=(ntp + nts,),
        in_specs=[
            pl.BlockSpec((TOP_K, tm), lambda i: (0, i), memory_space=pltpu.SMEM),
            pl.BlockSpec((tm, TOP_K), lambda i: (i, 0)),
            pl.BlockSpec((tm, D_MODEL), lambda i: (i, 0)),
            pl.BlockSpec((tm, PLE_DIM), pmap),
            pl.BlockSpec((tm, PLE_DIM), smap),
            pl.BlockSpec(memory_space=pl.ANY),
            pl.BlockSpec(gple.shape, const),
            pl.BlockSpec(wg.shape, const),
            pl.BlockSpec(wp.shape, const),
        ],
        out_specs=[
            pl.BlockSpec((tm, D_MODEL), pmap),
            pl.BlockSpec((tm, D_MODEL), smap),
        ],
        out_shape=[
            jax.ShapeDtypeStruct((tp, D_MODEL), F32),
            jax.ShapeDtypeStruct((ts, D_MODEL), F32),
        ],
        scratch_shapes=[pltpu.VMEM((TOP_K * tm * NSLAB, LANES), F32), pltpu.SemaphoreType.DMA(())],
        compiler_params=_cparams(("arbitrary",)),
        name="combine",
    )(dest, gates_col, x1, p_p, p_s, yb, gple, wg, wp)


def _bias_tables(rel_bias, n_q, n_k, key_offset, band):
    i = np.arange(n_q)[:, None]
    j = np.arange(n_k)[None, :]
    d_max = n_q - 1 + key_offset
    d_min = key_offset - (n_k - 1)
    n_hi = max(0, d_max - REL_MAX)
    n_lo = max(0, -REL_MAX - d_min)
    mid = rel_bias[:, max(d_min, -REL_MAX) + REL_MAX:min(d_max, REL_MAX) + REL_MAX + 1][:, ::-1]
    e = jnp.concatenate([jnp.broadcast_to(rel_bias[:, 2 * REL_MAX:], (H_B, n_hi)), mid,
                         jnp.broadcast_to(rel_bias[:, 0:1], (H_B, n_lo))], axis=1)
    period = n_q + n_k
    e = jnp.pad(e, ((0, 0), (0, 1)))
    flat = jnp.tile(e, (1, n_q + 1))[:, 0:n_q * (period + 1)]
    tab = flat.reshape(H_B, n_q, period + 1)[:, ::-1, 0:n_k]
    if band:
        qc = i // CHUNK
        sc = j // CHUNK
        ok = (sc >= qc) & (sc <= qc + N_BACK)
        tab = jnp.where(jnp.asarray(ok)[None], tab, NEG_BIG)
    return tab.reshape(H_B // 2, 2 * n_q, n_k)


def kernel(x_prompt, x_sample, state_delta, state_conv, cache_k, cache_v, p_prompt, p_sample, g_mix, w_in, conv_w, a_log, dt_bias, g_onorm, g_qnorm, g_knorm, rel_bias, w_out, g_ffn, w_router, b_router, w_gu, b_gu, w_down, b_down, g_ple, w_ple_gate, w_ple_proj):
    assert w_in.shape[0] == 1, "single-layer kernel"
    bp, tp_len, _ = x_prompt.shape
    bs, ts_len, _ = x_sample.shape
    tp, ts = bp * tp_len, bs * ts_len
    past = cache_k.shape[2]

    w = w_in[0]
    o_z = CONV_DIM + VAL_DIM_A
    o_qb = o_z + 2 * H_A
    wa = w[:, 0:o_z].astype(BF16)
    wba = jnp.pad(w[:, o_z:o_qb], ((0, 0), (0, LANES - 2 * H_A))).astype(BF16)
    wb = w[:, o_qb:].astype(BF16)
    grp = np.arange(ATT_DIM_B) // HD_B
    bd = jnp.asarray(grp[:, None] == grp[None, :], BF16)
    gq = jnp.tile(g_qnorm[0], H_B)[None, :]
    gk = jnp.tile(g_knorm[0], H_B)[None, :]
    gmix = g_mix[0][None, :]
    gon = g_onorm[0][None, :]
    convw = conv_w[0]
    woa = w_out[0, 0:VAL_DIM_A].astype(BF16)
    wob = w_out[0, VAL_DIM_A:].astype(BF16)
    gffn = g_ffn[0][None, :]
    wrt = w_router[0].T.astype(BF16)
    brt = jnp.broadcast_to(b_router[0][:, None], (N_EXPERTS, LANES))
    wgu = w_gu[0]
    bgu = b_gu[0][:, None, :]
    wd = w_down[0]
    bdn = b_down[0][:, None, :]
    gple = g_ple[0][None, :]
    wg = w_ple_gate[0].astype(BF16)
    wp = w_ple_proj[0].astype(BF16)

    xp2 = x_prompt.reshape(tp, D_MODEL)
    xs2 = x_sample.reshape(ts, D_MODEL)

    qkvz_p, ba_p, qkvb_p = _project(xp2, gmix, wa, wba, wb, bd, gq, gk)
    qkvz_s, ba_s, qkvb_s = _project(xs2, gmix, wa, wba, wb, bd, gq, gk)

    hist_pad = ((0, 0), (SUBLANES - (CONV_W - 1), 0), (0, 0))
    cbuf_p = jnp.zeros((bp, SUBLANES, CONV_DIM), F32)
    cbuf_s = jnp.pad(state_conv[0], hist_pad)
    s0_p = jnp.zeros((bp, H_A * DK_A, DV_A), F32)
    s0_s = state_delta[0].reshape(bs, H_A * DK_A, DV_A)
    oa_p, sfin_p, cnew_p = _delta_mixer(qkvz_p.reshape(bp, tp_len, -1), ba_p.reshape(bp, tp_len, LANES),
                                        cbuf_p, s0_p, convw, a_log[0], dt_bias[0], gon)
    oa_s, sfin_s, cnew_s = _delta_mixer(qkvz_s.reshape(bs, ts_len, -1), ba_s.reshape(bs, ts_len, LANES),
                                        cbuf_s, s0_s, convw, a_log[0], dt_bias[0], gon)
    sfin_p = sfin_p.reshape(bp, H_A, DK_A, DV_A)
    sfin_s = sfin_s.reshape(bs, H_A, DK_A, DV_A)

    qkvb_p3 = qkvb_p.reshape(bp, tp_len, 3 * ATT_DIM_B)
    qkvb_s3 = qkvb_s.reshape(bs, ts_len, 3 * ATT_DIM_B)
    bias_p = _bias_tables(rel_bias[0], ATT_G * CHUNK, (N_BACK + ATT_G) * CHUNK, BAND_PAST, True)
    ob_p = _attend_prompt(qkvb_p3, bias_p)
    bias_s = _bias_tables(rel_bias[0], ts_len, past + ts_len, past, False)
    ob_s = _attend_sample(qkvb_s3, cache_k[0].reshape(bs, past, ATT_DIM_B),
                          cache_v[0].reshape(bs, past, ATT_DIM_B),
                          bias_s[:, :, 0:past], bias_s[:, :, past:])

    x1, h2, idx, gates, rank, cnt = _out_router(
        oa_p.reshape(tp, VAL_DIM_A), ob_p.reshape(tp, ATT_DIM_B), xp2,
        oa_s.reshape(ts, VAL_DIM_A), ob_s.reshape(ts, ATT_DIM_B), xs2,
        woa, wob, gffn, wrt, brt)
    ttot = tp + ts
    bm = MOE_BM
    counts = cnt[:, 0].astype(jnp.int32)
    padded = (counts + bm - 1) // bm * bm
    pend = jnp.cumsum(padded)
    pstart = pend - padded
    eids = jnp.arange(N_EXPERTS, dtype=jnp.int32)
    start_of = jnp.sum(jnp.where(idx[None] == eids[:, None, None], pstart[:, None, None], 0), axis=0)
    dest = start_of + rank
    nb = -(-(ttot * TOP_K) // bm) + N_EXPERTS
    nused = (pend[-1] // bm).astype(jnp.int32)[None]
    first = jnp.minimum(jnp.arange(nb, dtype=jnp.int32), nused[0] - 1) * bm
    bexp = jnp.minimum(jnp.sum((pend[None, :] <= first[:, None]).astype(jnp.int32), axis=1),
                       N_EXPERTS - 1)

    xs_rows = _dispatch(pstart + counts, pend, dest, h2, nb * bm, min(COMB_TM, tp, ts))
    yb = _experts(bexp, nused, xs_rows, wgu, bgu, wd, bdn)
    y_p, y_s = _combine(dest, gates.T, x1, p_prompt[0].reshape(tp, PLE_DIM),
                        p_sample[0].reshape(ts, PLE_DIM), yb, gple, wg, wp)

    keep = min(BAND_PAST, tp_len)
    k_p = qkvb_p3[:, tp_len - keep:, ATT_DIM_B:2 * ATT_DIM_B].reshape(1, bp, keep, H_B, HD_B)
    v_p = qkvb_p3[:, tp_len - keep:, 2 * ATT_DIM_B:].reshape(1, bp, keep, H_B, HD_B)
    k_s = qkvb_s3[:, :, ATT_DIM_B:2 * ATT_DIM_B].reshape(1, bs, ts_len, H_B, HD_B)
    v_s = qkvb_s3[:, :, 2 * ATT_DIM_B:].reshape(1, bs, ts_len, H_B, HD_B)
    nconv = CONV_W - 1
    return (y_p.reshape(bp, tp_len, D_MODEL), y_s.reshape(bs, ts_len, D_MODEL),
            sfin_p[None], cnew_p[None, :, SUBLANES - nconv:, :], k_p, v_p,
            sfin_s[None], cnew_s[None, :, SUBLANES - nconv:, :], k_s, v_s)
```

```python
import functools

import jax
import jax.numpy as jnp
import numpy as np
from jax import lax
from jax.experimental import pallas as pl
from jax.experimental.pallas import tpu as pltpu

F32 = jnp.float32
BF16 = jnp.bfloat16

D_MODEL = 1024
CHUNK = 64
H_A = 4
DK_A = 128
DV_A = 128
CONV_W = 4
KEY_DIM_A = H_A * DK_A
VAL_DIM_A = H_A * DV_A
CONV_DIM = 2 * KEY_DIM_A + VAL_DIM_A
H_B = 8
HD_B = 64
ATT_DIM_B = H_B * HD_B
N_BACK = 8
BAND_PAST = N_BACK * CHUNK
REL_MAX = 256
N_EXPERTS = 32
TOP_K = 4
D_FF = 1024
SWIGLU_LIMIT = 7.0
SWIGLU_ALPHA = 1.702
PLE_DIM = 256
RMS_EPS = 1e-6
L2_EPS = 1e-6
NEG_BIG = -1e30

LANES = 128
SUBLANES = 8
VMEM_LIMIT = 56 * 1024 * 1024

PROJ_TM = 512
ATT_G = 2
DELTA_NCH = 4
MOE_BM = 512
COMB_TM = 256


def _mm(a, b):
    return jnp.dot(a.astype(BF16), b.astype(BF16), preferred_element_type=F32)


def _mm_nt(a, b):
    return lax.dot_general(a.astype(BF16), b.astype(BF16), (((1,), (1,)), ((), ())),
                           preferred_element_type=F32)


def _sigmoid(x):
    return 1.0 / (1.0 + jnp.exp(-x))


def _softplus(x):
    return jnp.maximum(x, 0.0) + jnp.log(1.0 + jnp.exp(-jnp.abs(x)))


def _rms(x, g):
    ms = jnp.mean(x * x, axis=-1, keepdims=True)
    return x * lax.rsqrt(ms + RMS_EPS) * g


NSLAB = D_MODEL // LANES


def _store_slabs(ref, val):
    n = val.shape[0]
    for s in range(NSLAB):
        ref[pl.ds(s, n, stride=NSLAB), :] = val[:, s * LANES:(s + 1) * LANES]


def _load_slabs(ref, n):
    return jnp.concatenate([ref[pl.ds(s, n, stride=NSLAB), :] for s in range(NSLAB)], axis=1)


def _slab_rows(ref, r):
    return ref.at[pl.ds(pl.multiple_of(r * NSLAB, NSLAB), NSLAB), :]


def _cparams(sem):
    return pltpu.CompilerParams(dimension_semantics=sem, vmem_limit_bytes=VMEM_LIMIT)


def _proj_kernel(x_ref, gmix_ref, wa_ref, wba_ref, wb_ref, bd_ref, gq_ref, gk_ref,
                 qkvz_ref, ba_ref, qkvb_ref):
    h = _rms(x_ref[...], gmix_ref[...]).astype(BF16)
    qkvz_ref[...] = jnp.dot(h, wa_ref[...], preferred_element_type=F32)
    ba_ref[...] = jnp.dot(h, wba_ref[...], preferred_element_type=F32)
    pb = jnp.dot(h, wb_ref[...], preferred_element_type=F32)
    bd = bd_ref[...]

    def head_norm(q, g):
        sq = q * q
        hi = sq.astype(BF16)
        lo = (sq - hi.astype(F32)).astype(BF16)
        ss = (jnp.dot(hi, bd, preferred_element_type=F32)
              + jnp.dot(lo, bd, preferred_element_type=F32))
        return q * lax.rsqrt(ss * (1.0 / HD_B) + RMS_EPS) * g

    qkvb_ref[:, 0:ATT_DIM_B] = head_norm(pb[:, 0:ATT_DIM_B], gq_ref[...])
    qkvb_ref[:, ATT_DIM_B:2 * ATT_DIM_B] = head_norm(pb[:, ATT_DIM_B:2 * ATT_DIM_B], gk_ref[...])
    qkvb_ref[:, 2 * ATT_DIM_B:] = pb[:, 2 * ATT_DIM_B:]


def _project(x2d, gmix, wa, wba, wb, bd, gq, gk):
    t = x2d.shape[0]
    tm = min(PROJ_TM, t)
    const = lambda i: (0, 0)
    return pl.pallas_call(
        _proj_kernel,
        grid=(t // tm,),
        in_specs=[
            pl.BlockSpec((tm, D_MODEL), lambda i: (i, 0)),
            pl.BlockSpec(gmix.shape, const),
            pl.BlockSpec(wa.shape, const),
            pl.BlockSpec(wba.shape, const),
            pl.BlockSpec(wb.shape, const),
            pl.BlockSpec(bd.shape, const),
            pl.BlockSpec(gq.shape, const),
            pl.BlockSpec(gk.shape, const),
        ],
        out_specs=[
            pl.BlockSpec((tm, wa.shape[1]), lambda i: (i, 0)),
            pl.BlockSpec((tm, LANES), lambda i: (i, 0)),
            pl.BlockSpec((tm, wb.shape[1]), lambda i: (i, 0)),
        ],
        out_shape=[
            jax.ShapeDtypeStruct((t, wa.shape[1]), F32),
            jax.ShapeDtypeStruct((t, LANES), F32),
            jax.ShapeDtypeStruct((t, wb.shape[1]), F32),
        ],
        compiler_params=_cparams(("arbitrary",)),
        name="proj",
    )(x2d, gmix, wa, wba, wb, bd, gq, gk)


def _delta_kernel(qkvz_ref, ba_ref, cbuf_ref, s0_ref, convw_ref, nega_ref, dtb_ref,
                  gon_ref, lbd_ref,
                  o_ref, sfin_ref, cnew_ref, xh_ref, s_ref, *, c, nch):
    step = pl.program_id(1)
    tile = c * nch
    hist = SUBLANES

    @pl.when(step == 0)
    def _():
        xh_ref[0:hist, :] = cbuf_ref[...]
        s_ref[...] = s0_ref[...]

    xh_ref[hist:hist + tile, :] = qkvz_ref[:, 0:CONV_DIM]
    w = convw_ref[...]
    conv = (xh_ref[hist - 3:hist - 3 + tile, :] * w[0:1, :]
            + xh_ref[hist - 2:hist - 2 + tile, :] * w[1:2, :]
            + xh_ref[hist - 1:hist - 1 + tile, :] * w[2:3, :]
            + xh_ref[hist:hist + tile, :] * w[3:4, :])
    conv = conv * _sigmoid(conv)
    last_rows = xh_ref[tile:tile + hist, :]
    cnew_ref[...] = last_rows
    xh_ref[0:hist, :] = last_rows

    ba = ba_ref[...]
    z_all = qkvz_ref[:, CONV_DIM:CONV_DIM + VAL_DIM_A]

    rs = H_A * c
    sk = H_A * DK_A
    row = lax.broadcasted_iota(jnp.int32, (rs, rs), 0)
    col = lax.broadcasted_iota(jnp.int32, (rs, rs), 1)
    same = (row // c) == (col // c)
    causal = jnp.logical_and(same, row >= col)
    strict = jnp.logical_and(same, row > col)
    wrow = lax.broadcasted_iota(jnp.int32, (rs, sk), 0)
    wcol = lax.broadcasted_iota(jnp.int32, (rs, sk), 1)
    head_block = (wrow // c) == (wcol // DK_A)
    lbd = lbd_ref[...]
    nsq = int(np.log2(c))

    def stack(fn):
        return jnp.concatenate([fn(h) for h in range(H_A)], axis=0)

    def spread(m):
        return jnp.where(head_block, jnp.concatenate([m] * H_A, axis=1), 0.0)

    def mm_split3(a_bf16, x):
        hi = x.astype(BF16)
        r1 = x - hi.astype(F32)
        mid = r1.astype(BF16)
        lo = (r1 - mid.astype(F32)).astype(BF16)
        return (jnp.dot(a_bf16, hi, preferred_element_type=F32)
                + jnp.dot(a_bf16, mid, preferred_element_type=F32)
                + jnp.dot(a_bf16, lo, preferred_element_type=F32))

    def l2n(m):
        return m * lax.rsqrt(jnp.sum(m * m, axis=-1, keepdims=True) + L2_EPS)

    def last_row(m, h, shape):
        return jnp.broadcast_to(m[h * c + c - 1:h * c + c, :], shape)

    cs = range(nch)
    q_st = [l2n(stack(lambda h: conv[ci * c:ci * c + c, h * DK_A:(h + 1) * DK_A])) * (DK_A ** -0.5)
            for ci in cs]
    k_st = [l2n(stack(lambda h: conv[ci * c:ci * c + c, KEY_DIM_A + h * DK_A:KEY_DIM_A + (h + 1) * DK_A]))
            for ci in cs]
    v_st = [stack(lambda h: conv[ci * c:ci * c + c, 2 * KEY_DIM_A + h * DV_A:2 * KEY_DIM_A + (h + 1) * DV_A])
            for ci in cs]
    beta = [_sigmoid(stack(lambda h: jnp.broadcast_to(ba[ci * c:ci * c + c, h:h + 1], (c, LANES))))
            for ci in cs]
    g = [nega_ref[...] * _softplus(
        stack(lambda h: jnp.broadcast_to(ba[ci * c:ci * c + c, H_A + h:H_A + h + 1], (c, LANES)))
        + dtb_ref[...]) for ci in cs]
    gc = [mm_split3(lbd, gi) for gi in g]
    kb = [k * b for k, b in zip(k_st, beta)]
    kk = [_mm_nt(a, b) for a, b in zip(kb, k_st)]
    qk = [_mm_nt(a, b) for a, b in zip(q_st, k_st)]
    decay = [jnp.exp(jnp.where(causal,
                               jnp.concatenate([m] * (rs // LANES), axis=1)
                               - jnp.broadcast_to(m.T[0:1, :], (rs, rs)),
                               -jnp.inf)) for m in gc]
    egc = [jnp.exp(m) for m in gc]
    ps = [-jnp.where(strict, a * d, 0.0) for a, d in zip(kk, decay)]
    qk = [a * d for a, d in zip(qk, decay)]
    xs = [jnp.concatenate([v * b, k * e], axis=-1) for v, b, k, e in zip(v_st, beta, kb, egc)]
    for s in range(nsq):
        xs = [x + _mm(p, x) for x, p in zip(xs, ps)]
        if s + 1 < nsq:
            ps = [_mm(p, p) for p in ps]
    qbd = [spread(q * e) for q, e in zip(q_st, egc)]
    kbd = [spread(k * jnp.exp(stack(lambda h: last_row(m, h, (c, LANES))) - m)) for k, m in zip(k_st, gc)]
    egl = [stack(lambda h: last_row(e, h, (DK_A, DV_A))) for e in egc]

    for ci in range(nch):
        r0 = ci * c
        eglast = egl[ci]
        u0 = xs[ci][:, 0:DV_A]
        wbd = spread(xs[ci][:, DV_A:])
        s_all = s_ref[...]
        u = u0 - _mm(wbd, s_all)
        o = _mm(qbd[ci], s_all) + _mm(qk[ci], u)
        s_ref[...] = s_all * eglast + lax.dot_general(
            kbd[ci].astype(BF16), u.astype(BF16), (((0,), (0,)), ((), ())), preferred_element_type=F32)
        on = _rms(o, gon_ref[...])
        for h in range(H_A):
            z = z_all[r0:r0 + c, h * DV_A:(h + 1) * DV_A]
            o_ref[r0:r0 + c, h * DV_A:(h + 1) * DV_A] = on[h * c:(h + 1) * c, :] * (z * _sigmoid(z))

    sfin_ref[...] = s_ref[...]


def _delta_mixer(qkvz, ba, cbuf, s0, convw, a_log, dt_bias, gon):
    b, t, _ = qkvz.shape
    c = min(CHUNK, t)
    nch = min(DELTA_NCH, t // c)
    tile = c * nch
    rs = H_A * c
    sk = H_A * DK_A
    assert rs % LANES == 0
    r = np.arange(rs)
    lbd = jnp.asarray((r[:, None] // c == r[None, :] // c) & (r[:, None] >= r[None, :]), BF16)
    nega = jnp.broadcast_to(jnp.repeat(-jnp.exp(a_log), c)[:, None], (rs, LANES))
    dtb = jnp.broadcast_to(jnp.repeat(dt_bias, c)[:, None], (rs, LANES))
    const2 = lambda i, j: (0, 0)
    kern = functools.partial(_delta_kernel, c=c, nch=nch)
    return pl.pallas_call(
        kern,
        grid=(b, t // tile),
        in_specs=[
            pl.BlockSpec((None, tile, qkvz.shape[2]), lambda i, j: (i, j, 0)),
            pl.BlockSpec((None, tile, LANES), lambda i, j: (i, j, 0)),
            pl.BlockSpec((None, SUBLANES, CONV_DIM), lambda i, j: (i, 0, 0)),
            pl.BlockSpec((None, sk, DV_A), lambda i, j: (i, 0, 0)),
            pl.BlockSpec(convw.shape, const2),
            pl.BlockSpec(nega.shape, const2),
            pl.BlockSpec(dtb.shape, const2),
            pl.BlockSpec(gon.shape, const2),
            pl.BlockSpec(lbd.shape, const2),
        ],
        out_specs=[
            pl.BlockSpec((None, tile, VAL_DIM_A), lambda i, j: (i, j, 0)),
            pl.BlockSpec((None, sk, DV_A), lambda i, j: (i, 0, 0)),
            pl.BlockSpec((None, SUBLANES, CONV_DIM), lambda i, j: (i, 0, 0)),
        ],
        out_shape=[
            jax.ShapeDtypeStruct((b, t, VAL_DIM_A), F32),
            jax.ShapeDtypeStruct((b, sk, DV_A), F32),
            jax.ShapeDtypeStruct((b, SUBLANES, CONV_DIM), F32),
        ],
        scratch_shapes=[
            pltpu.VMEM((SUBLANES + tile, CONV_DIM), F32),
            pltpu.VMEM((sk, DV_A), F32),
        ],
        compiler_params=_cparams(("arbitrary", "arbitrary")),
        name="delta",
    )(qkvz, ba, cbuf, s0, convw, nega, dtb, gon, lbd)


def _pair_queries(qp):
    lane = lax.broadcasted_iota(jnp.int32, qp.shape, 1)
    q_even = jnp.where(lane < HD_B, qp, 0.0)
    q_odd = jnp.where(lane >= HD_B, qp, 0.0)
    return jnp.concatenate([q_even, q_odd], axis=0).astype(BF16)


def _unpair(o, r):
    lane = lax.broadcasted_iota(jnp.int32, (r, LANES), 1)
    return jnp.where(lane < HD_B, o[0:r, :], o[r:2 * r, :])


def _attn_prompt_kernel(q_ref, k_ref, v_ref, bias_ref, o_ref, kwin_ref, vwin_ref, *, g):
    cstep = pl.program_id(1)
    rows = g * CHUNK
    win = (N_BACK + g) * CHUNK

    @pl.when(cstep == 0)
    def _():
        kwin_ref[...] = jnp.zeros(kwin_ref.shape, BF16)
        vwin_ref[...] = jnp.zeros(vwin_ref.shape, BF16)

    for i in range(N_BACK // g):
        kwin_ref[i * rows:(i + 1) * rows, :] = kwin_ref[(i + 1) * rows:(i + 2) * rows, :]
        vwin_ref[i * rows:(i + 1) * rows, :] = vwin_ref[(i + 1) * rows:(i + 2) * rows, :]
    kwin_ref[win - rows:win, :] = k_ref[...].astype(BF16)
    vwin_ref[win - rows:win, :] = v_ref[...].astype(BF16)

    slot_chunk = lax.broadcasted_iota(jnp.int32, (2 * rows, win), 1) // CHUNK
    in_seq = slot_chunk >= N_BACK - cstep * g
    pairs = range(H_B // 2)
    sls = [slice(p * LANES, (p + 1) * LANES) for p in pairs]
    s = [_mm_nt(_pair_queries(q_ref[:, sl] * (HD_B ** -0.5)), kwin_ref[:, sl]) for sl in sls]
    s = [jnp.where(in_seq, sp + bias_ref[p], NEG_BIG) for p, sp in zip(pairs, s)]
    m = [jnp.max(sp, axis=-1, keepdims=True) for sp in s]
    e = [jnp.exp(sp - mp) for sp, mp in zip(s, m)]
    l = [jnp.sum(ep, axis=-1, keepdims=True) for ep in e]
    o = [jnp.dot(ep.astype(BF16), vwin_ref[:, sl], preferred_element_type=F32) for ep, sl in zip(e, sls)]
    for sl, op, lp in zip(sls, o, l):
        o_ref[:, sl] = _unpair(op / lp, rows)


def _attend_prompt(qkvb, bias):
    b, t, _ = qkvb.shape
    g = ATT_G
    rows = g * CHUNK
    win = (N_BACK + g) * CHUNK
    kern = functools.partial(_attn_prompt_kernel, g=g)
    return pl.pallas_call(
        kern,
        grid=(b, t // rows),
        in_specs=[
            pl.BlockSpec((None, rows, ATT_DIM_B), lambda i, j: (i, j, 0)),
            pl.BlockSpec((None, rows, ATT_DIM_B), lambda i, j: (i, j, 1)),
            pl.BlockSpec((None, rows, ATT_DIM_B), lambda i, j: (i, j, 2)),
            pl.BlockSpec(bias.shape, lambda i, j: (0, 0, 0)),
        ],
        out_specs=pl.BlockSpec((None, rows, ATT_DIM_B), lambda i, j: (i, j, 0)),
        out_shape=jax.ShapeDtypeStruct((b, t, ATT_DIM_B), F32),
        scratch_shapes=[pltpu.VMEM((win, ATT_DIM_B), BF16), pltpu.VMEM((win, ATT_DIM_B), BF16)],
        compiler_params=_cparams(("arbitrary", "arbitrary")),
        name="attn_prompt",
    )(qkvb, qkvb, qkvb, bias)


def _attn_sample_kernel(q_ref, kn_ref, vn_ref, kc_ref, vc_ref, biasc_ref, biasn_ref, o_ref):
    ds = q_ref.shape[0]
    for p in range(H_B // 2):
        sl = slice(p * LANES, (p + 1) * LANES)
        q2 = _pair_queries(q_ref[:, sl] * (HD_B ** -0.5))
        s1 = _mm_nt(q2, kc_ref[:, sl]) + biasc_ref[p]
        s2 = _mm_nt(q2, kn_ref[:, sl]) + biasn_ref[p]
        m = jnp.maximum(jnp.max(s1, axis=-1, keepdims=True), jnp.max(s2, axis=-1, keepdims=True))
        e1 = jnp.exp(s1 - m)
        e2 = jnp.exp(s2 - m)
        l = jnp.sum(e1, axis=-1, keepdims=True) + jnp.sum(e2, axis=-1, keepdims=True)
        o = (_mm(e1, vc_ref[:, sl]) + _mm(e2, vn_ref[:, sl])) / l
        o_ref[:, sl] = _unpair(o, ds)


def _attend_sample(qkvb, cache_k, cache_v, biasc, biasn):
    b, ds, _ = qkvb.shape
    past = cache_k.shape[1]
    return pl.pallas_call(
        _attn_sample_kernel,
        grid=(b,),
        in_specs=[
            pl.BlockSpec((None, ds, ATT_DIM_B), lambda i: (i, 0, 0)),
            pl.BlockSpec((None, ds, ATT_DIM_B), lambda i: (i, 0, 1)),
            pl.BlockSpec((None, ds, ATT_DIM_B), lambda i: (i, 0, 2)),
            pl.BlockSpec((None, past, ATT_DIM_B), lambda i: (i, 0, 0)),
            pl.BlockSpec((None, past, ATT_DIM_B), lambda i: (i, 0, 0)),
            pl.BlockSpec(biasc.shape, lambda i: (0, 0, 0)),
            pl.BlockSpec(biasn.shape, lambda i: (0, 0, 0)),
        ],
        out_specs=pl.BlockSpec((None, ds, ATT_DIM_B), lambda i: (i, 0, 0)),
        out_shape=jax.ShapeDtypeStruct((b, ds, ATT_DIM_B), F32),
        compiler_params=_cparams(("arbitrary",)),
        name="attn_sample",
    )(qkvb, qkvb, qkvb, cache_k, cache_v, biasc, biasn)


def _router_kernel(oap_ref, obp_ref, xp_ref, oas_ref, obs_ref, xs_ref,
                   woa_ref, wob_ref, gffn_ref, wrt_ref, brt_ref, utri_ref,
                   x1_ref, h2_ref, idx_ref, gate_ref, rank_ref, cnt_ref,
                   carry_s, *, ntp):
    i = pl.program_id(0)
    tm = x1_ref.shape[0]

    @pl.when(i == 0)
    def _():
        carry_s[...] = jnp.zeros(carry_s.shape, F32)

    def mix(oa_ref, ob_ref, x_ref):
        x1_ref[...] = x_ref[...] + _mm(oa_ref[...], woa_ref[...]) + _mm(ob_ref[...], wob_ref[...])

    @pl.when(i < ntp)
    def _():
        mix(oap_ref, obp_ref, xp_ref)

    @pl.when(i >= ntp)
    def _():
        mix(oas_ref, obs_ref, xs_ref)

    x1 = x1_ref[...]
    h2 = _rms(x1, gffn_ref[...])
    _store_slabs(h2_ref, h2)
    logits = _mm_nt(wrt_ref[...], h2) + brt_ref[:, 0:1]

    eidx = lax.broadcasted_iota(jnp.int32, (N_EXPERTS, tm), 0).astype(F32)
    vals = logits
    memf = jnp.zeros((N_EXPERTS, tm), F32)
    tops, sels = [], []
    for k in range(TOP_K):
        m = jnp.max(vals, axis=0, keepdims=True)
        ix = jnp.min(jnp.where(vals == m, eidx, float(N_EXPERTS)), axis=0, keepdims=True)
        sel = eidx == ix
        tops.append(m)
        sels.append(sel)
        idx_ref[k:k + 1, :] = ix.astype(jnp.int32)
        vals = jnp.where(sel, -jnp.inf, vals)
        memf = memf + jnp.where(sel, 1.0, 0.0)
    es = [jnp.exp(v - tops[0]) for v in tops]
    denom = es[0] + es[1] + es[2] + es[3]
    for k in range(TOP_K):
        gate_ref[k:k + 1, :] = es[k] / denom

    carry = carry_s[:, 0:1]
    excl = _mm(memf, utri_ref[...]) + carry
    for k in range(TOP_K):
        r = jnp.sum(jnp.where(sels[k], excl, 0.0), axis=0, keepdims=True)
        rank_ref[k:k + 1, :] = r.astype(jnp.int32)
    new_carry = carry + jnp.sum(memf, axis=1, keepdims=True)
    carry_s[...] = jnp.broadcast_to(new_carry, carry_s.shape)
    cnt_ref[...] = jnp.broadcast_to(new_carry, cnt_ref.shape)


def _out_router(oa_p, ob_p, x_p, oa_s, ob_s, x_s, woa, wob, gffn, wrt, brt):
    tp, ts = x_p.shape[0], x_s.shape[0]
    tm = min(PROJ_TM, ts, tp)
    ntp, nts = tp // tm, ts // tm
    ttot = tp + ts
    utri = jnp.triu(jnp.ones((tm, tm), BF16), k=1)
    pmap = lambda i: (jnp.minimum(i, ntp - 1), 0)
    smap = lambda i: (jnp.maximum(i - ntp, 0), 0)
    const = lambda i: (0, 0)
    kern = functools.partial(_router_kernel, ntp=ntp)
    return pl.pallas_call(
        kern,
        grid=(ntp + nts,),
        in_specs=[
            pl.BlockSpec((tm, VAL_DIM_A), pmap),
            pl.BlockSpec((tm, ATT_DIM_B), pmap),
            pl.BlockSpec((tm, D_MODEL), pmap),
            pl.BlockSpec((tm, VAL_DIM_A), smap),
            pl.BlockSpec((tm, ATT_DIM_B), smap),
            pl.BlockSpec((tm, D_MODEL), smap),
            pl.BlockSpec(woa.shape, const),
            pl.BlockSpec(wob.shape, const),
            pl.BlockSpec(gffn.shape, const),
            pl.BlockSpec(wrt.shape, const),
            pl.BlockSpec(brt.shape, const),
            pl.BlockSpec(utri.shape, const),
        ],
        out_specs=[
            pl.BlockSpec((tm, D_MODEL), lambda i: (i, 0)),
            pl.BlockSpec((tm * NSLAB, LANES), lambda i: (i, 0)),
            pl.BlockSpec((TOP_K, tm), lambda i: (0, i)),
            pl.BlockSpec((TOP_K, tm), lambda i: (0, i)),
            pl.BlockSpec((TOP_K, tm), lambda i: (0, i)),
            pl.BlockSpec((N_EXPERTS, LANES), const),
        ],
        out_shape=[
            jax.ShapeDtypeStruct((ttot, D_MODEL), F32),
            jax.ShapeDtypeStruct((ttot * NSLAB, LANES), F32),
            jax.ShapeDtypeStruct((TOP_K, ttot), jnp.int32),
            jax.ShapeDtypeStruct((TOP_K, ttot), F32),
            jax.ShapeDtypeStruct((TOP_K, ttot), jnp.int32),
            jax.ShapeDtypeStruct((N_EXPERTS, LANES), F32),
        ],
        scratch_shapes=[pltpu.VMEM((N_EXPERTS, LANES), F32)],
        compiler_params=_cparams(("arbitrary",)),
        name="out_router",
    )(oa_p, ob_p, x_p, oa_s, ob_s, x_s, woa, wob, gffn, wrt, brt, utri)


def _row_copy(src_ref, s, dst_ref, d, sem):
    return pltpu.make_async_copy(_slab_rows(src_ref, s), _slab_rows(dst_ref, d), sem)


def _dispatch_kernel(padlo_ref, padhi_ref, dest_ref, h2_ref, xs_ref, zrow_ref, sem):
    i = pl.program_id(0)
    tm = h2_ref.shape[0] // NSLAB

    def issue(t, carry):
        for k in range(TOP_K):
            _row_copy(h2_ref, t, xs_ref, dest_ref[k, t], sem).start(priority=k % 2)
        return carry

    lax.fori_loop(0, tm, issue, 0, unroll=8)
    for k in range(TOP_K):
        pltpu.make_async_copy(h2_ref, xs_ref.at[pl.ds(0, tm * NSLAB), :], sem).wait()

    @pl.when(i == pl.num_programs(0) - 1)
    def _():
        zrow_ref[...] = jnp.zeros(zrow_ref.shape, F32)
        for e in range(N_EXPERTS):
            lo = padlo_ref[e]
            hi = padhi_ref[e]

            def zissue(r, carry):
                _row_copy(zrow_ref, 0, xs_ref, r, sem).start()
                return carry

            lax.fori_loop(lo, hi, zissue, 0)

            def zdrain(r, carry):
                _row_copy(zrow_ref, 0, xs_ref, 0, sem).wait()
                return carry

            lax.fori_loop(lo, hi, zdrain, 0)


def _dispatch(padlo, padhi, dest, h2, n_rows, tm_rows):
    ttot = h2.shape[0] // NSLAB
    tm = tm_rows
    grid_spec = pltpu.PrefetchScalarGridSpec(
        num_scalar_prefetch=2,
        grid=(ttot // tm,),
        in_specs=[
            pl.BlockSpec((TOP_K, tm), lambda i, lo, hi: (0, i), memory_space=pltpu.SMEM),
            pl.BlockSpec((tm * NSLAB, LANES), lambda i, lo, hi: (i, 0)),
        ],
        out_specs=pl.BlockSpec(memory_space=pl.ANY),
        scratch_shapes=[pltpu.VMEM((NSLAB, LANES), F32), pltpu.SemaphoreType.DMA(())],
    )
    return pl.pallas_call(
        _dispatch_kernel,
        grid_spec=grid_spec,
        out_shape=jax.ShapeDtypeStruct((n_rows * NSLAB, LANES), F32),
        compiler_params=_cparams(("arbitrary",)),
        name="dispatch",
    )(padlo, padhi, dest, h2)


def _expert_kernel(bexp_ref, nused_ref, x_ref, wgu_ref, bgu_ref, wd_ref, bd_ref, y_ref, wgu_s, wd_s):
    j = pl.program_id(0)
    active = j < nused_ref[0]
    new_expert = jnp.logical_or(j == 0, bexp_ref[j] != bexp_ref[jnp.maximum(j - 1, 0)])

    @pl.when(jnp.logical_and(active, new_expert))
    def _():
        rows = 128
        for r in range(0, D_MODEL, rows):
            wgu_s[r:r + rows, :] = wgu_ref[r:r + rows, :].astype(BF16)
        for r in range(0, D_FF, rows):
            wd_s[r:r + rows, :] = wd_ref[r:r + rows, :].astype(BF16)

    @pl.when(active)
    def _():
        bm = x_ref.shape[0] // NSLAB
        gu = _mm(_load_slabs(x_ref, bm), wgu_s[...]) + bgu_ref[...]
        gate = jnp.minimum(gu[:, 0:D_FF], SWIGLU_LIMIT)
        up = jnp.clip(gu[:, D_FF:], -SWIGLU_LIMIT, SWIGLU_LIMIT)
        hid = (up + 1.0) * (gate * _sigmoid(gate * SWIGLU_ALPHA))
        _store_slabs(y_ref, _mm(hid, wd_s[...]) + bd_ref[...])


def _experts(bexp, nused, xs, wgu, bgu, wd, bd):
    n_rows = xs.shape[0] // NSLAB
    bm = MOE_BM
    nb = n_rows // bm
    row_map = lambda j, be, nu: (jnp.minimum(j, nu[0] - 1), 0)
    w_map = lambda j, be, nu: (be[j], 0, 0)
    grid_spec = pltpu.PrefetchScalarGridSpec(
        num_scalar_prefetch=2,
        grid=(nb,),
        in_specs=[
            pl.BlockSpec((bm * NSLAB, LANES), row_map),
            pl.BlockSpec((None, D_MODEL, 2 * D_FF), w_map),
            pl.BlockSpec((None, 1, 2 * D_FF), w_map),
            pl.BlockSpec((None, D_FF, D_MODEL), w_map),
            pl.BlockSpec((None, 1, D_MODEL), w_map),
        ],
        out_specs=pl.BlockSpec((bm * NSLAB, LANES), row_map),
        scratch_shapes=[pltpu.VMEM((D_MODEL, 2 * D_FF), BF16), pltpu.VMEM((D_FF, D_MODEL), BF16)],
    )
    return pl.pallas_call(
        _expert_kernel,
        grid_spec=grid_spec,
        out_shape=jax.ShapeDtypeStruct((n_rows * NSLAB, LANES), F32),
        compiler_params=_cparams(("arbitrary",)),
        name="experts",
    )(bexp, nused, xs, wgu, bgu, wd, bd)


def _combine_kernel(dest_ref, destn_ref, gate_ref, x1_ref, pp_ref, ps_ref, yb_ref, gple_ref, wg_ref, wp_ref,
                    yp_ref, ys_ref, buf_ref, sems, *, ntp):
    i = pl.program_id(0)
    tm = x1_ref.shape[0]
    per_slot = TOP_K * tm
    slot = lax.rem(i, 2)

    def gather(d_ref, s):
        def issue(t, carry):
            for k in range(TOP_K):
                _row_copy(yb_ref, d_ref[k, t], buf_ref, s * per_slot + k * tm + t,
                          sems.at[s]).start(priority=k % 2)
            return carry

        lax.fori_loop(0, tm, issue, 0, unroll=8)

    @pl.when(i == 0)
    def _():
        gather(dest_ref, 0)

    @pl.when(i + 1 < pl.num_programs(0))
    def _():
        gather(destn_ref, 1 - slot)

    base = pl.multiple_of(slot * per_slot * NSLAB, NSLAB)
    pltpu.make_async_copy(yb_ref.at[pl.ds(0, per_slot * NSLAB), :],
                          buf_ref.at[pl.ds(base, per_slot * NSLAB), :], sems.at[slot]).wait()

    gates = gate_ref[...]
    moe = None
    for k in range(TOP_K):
        rows_k = jnp.concatenate(
            [buf_ref[pl.ds(base + k * tm * NSLAB + s, tm, stride=NSLAB), :] for s in range(NSLAB)], axis=1)
        term = rows_k * gates[:, k:k + 1]
        moe = term if moe is None else moe + term
    x2 = x1_ref[...] + moe
    gate = _sigmoid(_mm(_rms(x2, gple_ref[...]), wg_ref[...]))

    @pl.when(i < ntp)
    def _():
        yp_ref[...] = x2 + gate * _mm(pp_ref[...], wp_ref[...])

    @pl.when(i >= ntp)
    def _():
        ys_ref[...] = x2 + gate * _mm(ps_ref[...], wp_ref[...])


def _combine(dest, gates_col, x1, p_p, p_s, yb, gple, wg, wp):
    tp, ts = p_p.shape[0], p_s.shape[0]
    tm = min(COMB_TM, tp, ts)
    ntp, nts = tp // tm, ts // tm
    pmap = lambda i: (jnp.minimum(i, ntp - 1), 0)
    smap = lambda i: (jnp.maximum(i - ntp, 0), 0)
    const = lambda i: (0, 0)
    kern = functools.partial(_combine_kernel, ntp=ntp)
    return pl.pallas_call(
        kern,
        grid=(ntp + nts,),
        in_specs=[
            pl.BlockSpec((TOP_K, tm), lambda i: (0, i), memory_space=pltpu.SMEM),
            pl.BlockSpec((TOP_K, tm), lambda i: (0, jnp.minimum(i + 1, ntp + nts - 1)),
                         memory_space=pltpu.SMEM),
            pl.BlockSpec((tm, TOP_K), lambda i: (i, 0)),
            pl.BlockSpec((tm, D_MODEL), lambda i: (i, 0)),
            pl.BlockSpec((tm, PLE_DIM), pmap),
            pl.BlockSpec((tm, PLE_DIM), smap),
            pl.BlockSpec(memory_space=pl.ANY),
            pl.BlockSpec(gple.shape, const),
            pl.BlockSpec(wg.shape, const),
            pl.BlockSpec(wp.shape, const),
        ],
        out_specs=[
            pl.BlockSpec((tm, D_MODEL), pmap),
            pl.BlockSpec((tm, D_MODEL), smap),
        ],
        out_shape=[
            jax.ShapeDtypeStruct((tp, D_MODEL), F32),
            jax.ShapeDtypeStruct((ts, D_MODEL), F32),
        ],
        scratch_shapes=[pltpu.VMEM((2 * TOP_K * tm * NSLAB, LANES), F32), pltpu.SemaphoreType.DMA((2,))],
        compiler_params=_cparams(("arbitrary",)),
        name="combine",
    )(dest, dest, gates_col, x1, p_p, p_s, yb, gple, wg, wp)


def _bias_tables(rel_bias, n_q, n_k, key_offset, band):
    i = np.arange(n_q)[:, None]
    j = np.arange(n_k)[None, :]
    d_max = n_q - 1 + key_offset
    d_min = key_offset - (n_k - 1)
    n_hi = max(0, d_max - REL_MAX)
    n_lo = max(0, -REL_MAX - d_min)
    mid = rel_bias[:, max(d_min, -REL_MAX) + REL_MAX:min(d_max, REL_MAX) + REL_MAX + 1][:, ::-1]
    e = jnp.concatenate([jnp.broadcast_to(rel_bias[:, 2 * REL_MAX:], (H_B, n_hi)), mid,
                         jnp.broadcast_to(rel_bias[:, 0:1], (H_B, n_lo))], axis=1)
    period = n_q + n_k
    e = jnp.pad(e, ((0, 0), (0, 1)))
    flat = jnp.tile(e, (1, n_q + 1))[:, 0:n_q * (period + 1)]
    tab = flat.reshape(H_B, n_q, period + 1)[:, ::-1, 0:n_k]
    if band:
        qc = i // CHUNK
        sc = j // CHUNK
        ok = (sc >= qc) & (sc <= qc + N_BACK)
        tab = jnp.where(jnp.asarray(ok)[None], tab, NEG_BIG)
    return tab.reshape(H_B // 2, 2 * n_q, n_k)


def kernel(x_prompt, x_sample, state_delta, state_conv, cache_k, cache_v, p_prompt, p_sample, g_mix, w_in, conv_w, a_log, dt_bias, g_onorm, g_qnorm, g_knorm, rel_bias, w_out, g_ffn, w_router, b_router, w_gu, b_gu, w_down, b_down, g_ple, w_ple_gate, w_ple_proj):
    assert w_in.shape[0] == 1, "single-layer kernel"
    bp, tp_len, _ = x_prompt.shape
    bs, ts_len, _ = x_sample.shape
    tp, ts = bp * tp_len, bs * ts_len
    past = cache_k.shape[2]

    w = w_in[0]
    o_z = CONV_DIM + VAL_DIM_A
    o_qb = o_z + 2 * H_A
    wa = w[:, 0:o_z].astype(BF16)
    wba = jnp.pad(w[:, o_z:o_qb], ((0, 0), (0, LANES - 2 * H_A))).astype(BF16)
    wb = w[:, o_qb:].astype(BF16)
    grp = np.arange(ATT_DIM_B) // HD_B
    bd = jnp.asarray(grp[:, None] == grp[None, :], BF16)
    gq = jnp.tile(g_qnorm[0], H_B)[None, :]
    gk = jnp.tile(g_knorm[0], H_B)[None, :]
    gmix = g_mix[0][None, :]
    gon = g_onorm[0][None, :]
    convw = conv_w[0]
    woa = w_out[0, 0:VAL_DIM_A].astype(BF16)
    wob = w_out[0, VAL_DIM_A:].astype(BF16)
    gffn = g_ffn[0][None, :]
    wrt = w_router[0].T.astype(BF16)
    brt = jnp.broadcast_to(b_router[0][:, None], (N_EXPERTS, LANES))
    wgu = w_gu[0]
    bgu = b_gu[0][:, None, :]
    wd = w_down[0]
    bdn = b_down[0][:, None, :]
    gple = g_ple[0][None, :]
    wg = w_ple_gate[0].astype(BF16)
    wp = w_ple_proj[0].astype(BF16)

    xp2 = x_prompt.reshape(tp, D_MODEL)
    xs2 = x_sample.reshape(ts, D_MODEL)

    qkvz_p, ba_p, qkvb_p = _project(xp2, gmix, wa, wba, wb, bd, gq, gk)
    qkvz_s, ba_s, qkvb_s = _project(xs2, gmix, wa, wba, wb, bd, gq, gk)

    hist_pad = ((0, 0), (SUBLANES - (CONV_W - 1), 0), (0, 0))
    cbuf_p = jnp.zeros((bp, SUBLANES, CONV_DIM), F32)
    cbuf_s = jnp.pad(state_conv[0], hist_pad)
    s0_p = jnp.zeros((bp, H_A * DK_A, DV_A), F32)
    s0_s = state_delta[0].reshape(bs, H_A * DK_A, DV_A)
    oa_p, sfin_p, cnew_p = _delta_mixer(qkvz_p.reshape(bp, tp_len, -1), ba_p.reshape(bp, tp_len, LANES),
                                        cbuf_p, s0_p, convw, a_log[0], dt_bias[0], gon)
    oa_s, sfin_s, cnew_s = _delta_mixer(qkvz_s.reshape(bs, ts_len, -1), ba_s.reshape(bs, ts_len, LANES),
                                        cbuf_s, s0_s, convw, a_log[0], dt_bias[0], gon)
    sfin_p = sfin_p.reshape(bp, H_A, DK_A, DV_A)
    sfin_s = sfin_s.reshape(bs, H_A, DK_A, DV_A)

    qkvb_p3 = qkvb_p.reshape(bp, tp_len, 3 * ATT_DIM_B)
    qkvb_s3 = qkvb_s.reshape(bs, ts_len, 3 * ATT_DIM_B)
    bias_p = _bias_tables(rel_bias[0], ATT_G * CHUNK, (N_BACK + ATT_G) * CHUNK, BAND_PAST, True)
    ob_p = _attend_prompt(qkvb_p3, bias_p)
    bias_s = _bias_tables(rel_bias[0], ts_len, past + ts_len, past, False)
    ob_s = _attend_sample(qkvb_s3, cache_k[0].reshape(bs, past, ATT_DIM_B),
                          cache_v[0].reshape(bs, past, ATT_DIM_B),
                          bias_s[:, :, 0:past], bias_s[:, :, past:])

    x1, h2, idx, gates, rank, cnt = _out_router(
        oa_p.reshape(tp, VAL_DIM_A), ob_p.reshape(tp, ATT_DIM_B), xp2,
        oa_s.reshape(ts, VAL_DIM_A), ob_s.reshape(ts, ATT_DIM_B), xs2,
        woa, wob, gffn, wrt, brt)
    ttot = tp + ts
    bm = MOE_BM
    counts = cnt[:, 0].astype(jnp.int32)
    padded = (counts + bm - 1) // bm * bm
    pend = jnp.cumsum(padded)
    pstart = pend - padded
    eids = jnp.arange(N_EXPERTS, dtype=jnp.int32)
    start_of = jnp.sum(jnp.where(idx[None] == eids[:, None, None], pstart[:, None, None], 0), axis=0)
    dest = start_of + rank
    nb = -(-(ttot * TOP_K) // bm) + N_EXPERTS
    nused = (pend[-1] // bm).astype(jnp.int32)[None]
    first = jnp.minimum(jnp.arange(nb, dtype=jnp.int32), nused[0] - 1) * bm
    bexp = jnp.minimum(jnp.sum((pend[None, :] <= first[:, None]).astype(jnp.int32), axis=1),
                       N_EXPERTS - 1)

    xs_rows = _dispatch(pstart + counts, pend, dest, h2, nb * bm, min(COMB_TM, tp, ts))
    yb = _experts(bexp, nused, xs_rows, wgu, bgu, wd, bdn)
    y_p, y_s = _combine(dest, gates.T, x1, p_prompt[0].reshape(tp, PLE_DIM),
                        p_sample[0].reshape(ts, PLE_DIM), yb, gple, wg, wp)

    keep = min(BAND_PAST, tp_len)
    k_p = qkvb_p3[:, tp_len - keep:, ATT_DIM_B:2 * ATT_DIM_B].reshape(1, bp, keep, H_B, HD_B)
    v_p = qkvb_p3[:, tp_len - keep:, 2 * ATT_DIM_B:].reshape(1, bp, keep, H_B, HD_B)
    k_s = qkvb_s3[:, :, ATT_DIM_B:2 * ATT_DIM_B].reshape(1, bs, ts_len, H_B, HD_B)
    v_s = qkvb_s3[:, :, 2 * ATT_DIM_B:].reshape(1, bs, ts_len, H_B, HD_B)
    nconv = CONV_W - 1
    return (y_p.reshape(bp, tp_len, D_MODEL), y_s.reshape(bs, ts_len, D_MODEL),
            sfin_p[None], cnew_p[None, :, SUBLANES - nconv:, :], k_p, v_p,
            sfin_s[None], cnew_s[None, :, SUBLANES - nconv:, :], k_s, v_s)
```

```python
import functools

import jax
import jax.numpy as jnp
import numpy as np
from jax import lax
from jax.experimental import pallas as pl
from jax.experimental.pallas import tpu as pltpu

F32 = jnp.float32
BF16 = jnp.bfloat16

D_MODEL = 1024
CHUNK = 64
H_A = 4
DK_A = 128
DV_A = 128
CONV_W = 4
KEY_DIM_A = H_A * DK_A
VAL_DIM_A = H_A * DV_A
CONV_DIM = 2 * KEY_DIM_A + VAL_DIM_A
H_B = 8
HD_B = 64
ATT_DIM_B = H_B * HD_B
N_BACK = 8
BAND_PAST = N_BACK * CHUNK
REL_MAX = 256
N_EXPERTS = 32
TOP_K = 4
D_FF = 1024
SWIGLU_LIMIT = 7.0
SWIGLU_ALPHA = 1.702
PLE_DIM = 256
RMS_EPS = 1e-6
L2_EPS = 1e-6
NEG_BIG = -1e30

LANES = 128
SUBLANES = 8
VMEM_LIMIT = 56 * 1024 * 1024

PROJ_TM = 512
ATT_G = 2
ATT_PAIR_GROUPS = ((0, 1), (2, 3))
DELTA_NCH = 4
MOE_BM = 512
COMB_TM = 256
COMB_GROUPS = 2


def _mm(a, b):
    return jnp.dot(a.astype(BF16), b.astype(BF16), preferred_element_type=F32)


def _mm_nt(a, b):
    return lax.dot_general(a.astype(BF16), b.astype(BF16), (((1,), (1,)), ((), ())),
                           preferred_element_type=F32)


def _sigmoid(x):
    return 1.0 / (1.0 + jnp.exp(-x))


def _softplus(x):
    return jnp.maximum(x, 0.0) + jnp.log(1.0 + jnp.exp(-jnp.abs(x)))


def _rms(x, g):
    ms = jnp.mean(x * x, axis=-1, keepdims=True)
    return x * lax.rsqrt(ms + RMS_EPS) * g


NSLAB = D_MODEL // LANES


def _store_slabs(ref, val):
    n = val.shape[0]
    for s in range(NSLAB):
        ref[pl.ds(s, n, stride=NSLAB), :] = val[:, s * LANES:(s + 1) * LANES]


def _load_slabs(ref, n):
    return jnp.concatenate([ref[pl.ds(s, n, stride=NSLAB), :] for s in range(NSLAB)], axis=1)


def _slab_rows(ref, r):
    return ref.at[pl.ds(pl.multiple_of(r * NSLAB, NSLAB), NSLAB), :]


def _cparams(sem):
    return pltpu.CompilerParams(dimension_semantics=sem, vmem_limit_bytes=VMEM_LIMIT)


def _proj_kernel(x_ref, gmix_ref, wa_ref, wba_ref, wb_ref, bd_ref, gq_ref, gk_ref,
                 qkvz_ref, ba_ref, qkvb_ref):
    h = _rms(x_ref[...], gmix_ref[...]).astype(BF16)
    qkvz_ref[...] = jnp.dot(h, wa_ref[...], preferred_element_type=F32)
    ba_ref[...] = jnp.dot(h, wba_ref[...], preferred_element_type=F32)
    pb = jnp.dot(h, wb_ref[...], preferred_element_type=F32)
    bd = bd_ref[...]

    def head_norm(q, g):
        ss = jnp.dot((q * q).astype(BF16), bd, preferred_element_type=F32)
        return q * lax.rsqrt(ss * (1.0 / HD_B) + RMS_EPS) * g

    qkvb_ref[:, 0:ATT_DIM_B] = head_norm(pb[:, 0:ATT_DIM_B], gq_ref[...])
    qkvb_ref[:, ATT_DIM_B:2 * ATT_DIM_B] = head_norm(pb[:, ATT_DIM_B:2 * ATT_DIM_B], gk_ref[...])
    qkvb_ref[:, 2 * ATT_DIM_B:] = pb[:, 2 * ATT_DIM_B:]


def _project(x2d, gmix, wa, wba, wb, bd, gq, gk):
    t = x2d.shape[0]
    tm = min(PROJ_TM, t)
    const = lambda i: (0, 0)
    return pl.pallas_call(
        _proj_kernel,
        grid=(t // tm,),
        in_specs=[
            pl.BlockSpec((tm, D_MODEL), lambda i: (i, 0)),
            pl.BlockSpec(gmix.shape, const),
            pl.BlockSpec(wa.shape, const),
            pl.BlockSpec(wba.shape, const),
            pl.BlockSpec(wb.shape, const),
            pl.BlockSpec(bd.shape, const),
            pl.BlockSpec(gq.shape, const),
            pl.BlockSpec(gk.shape, const),
        ],
        out_specs=[
            pl.BlockSpec((tm, wa.shape[1]), lambda i: (i, 0)),
            pl.BlockSpec((tm, LANES), lambda i: (i, 0)),
            pl.BlockSpec((tm, wb.shape[1]), lambda i: (i, 0)),
        ],
        out_shape=[
            jax.ShapeDtypeStruct((t, wa.shape[1]), F32),
            jax.ShapeDtypeStruct((t, LANES), F32),
            jax.ShapeDtypeStruct((t, wb.shape[1]), F32),
        ],
        compiler_params=_cparams(("arbitrary",)),
        name="proj",
    )(x2d, gmix, wa, wba, wb, bd, gq, gk)


def _delta_kernel(qkvz_ref, ba_ref, cbuf_ref, s0_ref, convw_ref, nega_ref, dtb_ref,
                  gon_ref, lbd_ref,
                  o_ref, sfin_ref, cnew_ref, xh_ref, s_ref, *, c, nch):
    step = pl.program_id(1)
    tile = c * nch
    hist = SUBLANES

    @pl.when(step == 0)
    def _():
        xh_ref[0:hist, :] = cbuf_ref[...]
        s_ref[...] = s0_ref[...]

    xh_ref[hist:hist + tile, :] = qkvz_ref[:, 0:CONV_DIM]
    w = convw_ref[...]
    conv = (xh_ref[hist - 3:hist - 3 + tile, :] * w[0:1, :]
            + xh_ref[hist - 2:hist - 2 + tile, :] * w[1:2, :]
            + xh_ref[hist - 1:hist - 1 + tile, :] * w[2:3, :]
            + xh_ref[hist:hist + tile, :] * w[3:4, :])
    conv = conv * _sigmoid(conv)
    last_rows = xh_ref[tile:tile + hist, :]
    cnew_ref[...] = last_rows
    xh_ref[0:hist, :] = last_rows

    ba = ba_ref[...]
    z_all = qkvz_ref[:, CONV_DIM:CONV_DIM + VAL_DIM_A]

    rs = H_A * c
    sk = H_A * DK_A
    row = lax.broadcasted_iota(jnp.int32, (rs, rs), 0)
    col = lax.broadcasted_iota(jnp.int32, (rs, rs), 1)
    same = (row // c) == (col // c)
    causal = jnp.logical_and(same, row >= col)
    strict = jnp.logical_and(same, row > col)
    wrow = lax.broadcasted_iota(jnp.int32, (rs, sk), 0)
    wcol = lax.broadcasted_iota(jnp.int32, (rs, sk), 1)
    head_block = (wrow // c) == (wcol // DK_A)
    lbd = lbd_ref[...]
    nsq = int(np.log2(c))

    def stack(fn):
        return jnp.concatenate([fn(h) for h in range(H_A)], axis=0)

    def spread(m):
        return jnp.where(head_block, jnp.concatenate([m] * H_A, axis=1), 0.0)

    def mm_split3(a_bf16, x):
        hi = x.astype(BF16)
        r1 = x - hi.astype(F32)
        mid = r1.astype(BF16)
        lo = (r1 - mid.astype(F32)).astype(BF16)
        return (jnp.dot(a_bf16, hi, preferred_element_type=F32)
                + jnp.dot(a_bf16, mid, preferred_element_type=F32)
                + jnp.dot(a_bf16, lo, preferred_element_type=F32))

    def l2n(m):
        return m * lax.rsqrt(jnp.sum(m * m, axis=-1, keepdims=True) + L2_EPS)

    def last_row(m, h, shape):
        return jnp.broadcast_to(m[h * c + c - 1:h * c + c, :], shape)

    cs = range(nch)
    q_st = [l2n(stack(lambda h: conv[ci * c:ci * c + c, h * DK_A:(h + 1) * DK_A])) * (DK_A ** -0.5)
            for ci in cs]
    k_st = [l2n(stack(lambda h: conv[ci * c:ci * c + c, KEY_DIM_A + h * DK_A:KEY_DIM_A + (h + 1) * DK_A]))
            for ci in cs]
    v_st = [stack(lambda h: conv[ci * c:ci * c + c, 2 * KEY_DIM_A + h * DV_A:2 * KEY_DIM_A + (h + 1) * DV_A])
            for ci in cs]
    beta = [_sigmoid(stack(lambda h: jnp.broadcast_to(ba[ci * c:ci * c + c, h:h + 1], (c, LANES))))
            for ci in cs]
    g = [nega_ref[...] * _softplus(
        stack(lambda h: jnp.broadcast_to(ba[ci * c:ci * c + c, H_A + h:H_A + h + 1], (c, LANES)))
        + dtb_ref[...]) for ci in cs]
    gc = [mm_split3(lbd, gi) for gi in g]
    kb = [k * b for k, b in zip(k_st, beta)]
    kk = [_mm_nt(a, b) for a, b in zip(kb, k_st)]
    qk = [_mm_nt(a, b) for a, b in zip(q_st, k_st)]
    decay = [jnp.exp(jnp.where(causal,
                               jnp.concatenate([m] * (rs // LANES), axis=1)
                               - jnp.broadcast_to(m.T[0:1, :], (rs, rs)),
                               -jnp.inf)) for m in gc]
    egc = [jnp.exp(m) for m in gc]
    ps = [-jnp.where(strict, a * d, 0.0) for a, d in zip(kk, decay)]
    qk = [a * d for a, d in zip(qk, decay)]
    xs = [jnp.concatenate([v * b, k * e], axis=-1) for v, b, k, e in zip(v_st, beta, kb, egc)]
    for s in range(nsq):
        xs = [x + _mm(p, x) for x, p in zip(xs, ps)]
        if s + 1 < nsq:
            ps = [_mm(p, p) for p in ps]
    qbd = [spread(q * e) for q, e in zip(q_st, egc)]
    kbd = [spread(k * jnp.exp(stack(lambda h: last_row(m, h, (c, LANES))) - m)) for k, m in zip(k_st, gc)]
    egl = [stack(lambda h: last_row(e, h, (DK_A, DV_A))) for e in egc]

    for ci in range(nch):
        r0 = ci * c
        eglast = egl[ci]
        u0 = xs[ci][:, 0:DV_A]
        wbd = spread(xs[ci][:, DV_A:])
        s_all = s_ref[...]
        u = u0 - _mm(wbd, s_all)
        o = _mm(qbd[ci], s_all) + _mm(qk[ci], u)
        s_ref[...] = s_all * eglast + lax.dot_general(
            kbd[ci].astype(BF16), u.astype(BF16), (((0,), (0,)), ((), ())), preferred_element_type=F32)
        on = _rms(o, gon_ref[...])
        for h in range(H_A):
            z = z_all[r0:r0 + c, h * DV_A:(h + 1) * DV_A]
            o_ref[r0:r0 + c, h * DV_A:(h + 1) * DV_A] = on[h * c:(h + 1) * c, :] * (z * _sigmoid(z))

    sfin_ref[...] = s_ref[...]


def _delta_mixer(qkvz, ba, cbuf, s0, convw, a_log, dt_bias, gon):
    b, t, _ = qkvz.shape
    c = min(CHUNK, t)
    nch = min(DELTA_NCH, t // c)
    tile = c * nch
    rs = H_A * c
    sk = H_A * DK_A
    assert rs % LANES == 0
    r = np.arange(rs)
    lbd = jnp.asarray((r[:, None] // c == r[None, :] // c) & (r[:, None] >= r[None, :]), BF16)
    nega = jnp.broadcast_to(jnp.repeat(-jnp.exp(a_log), c)[:, None], (rs, LANES))
    dtb = jnp.broadcast_to(jnp.repeat(dt_bias, c)[:, None], (rs, LANES))
    const2 = lambda i, j: (0, 0)
    kern = functools.partial(_delta_kernel, c=c, nch=nch)
    return pl.pallas_call(
        kern,
        grid=(b, t // tile),
        in_specs=[
            pl.BlockSpec((None, tile, qkvz.shape[2]), lambda i, j: (i, j, 0)),
            pl.BlockSpec((None, tile, LANES), lambda i, j: (i, j, 0)),
            pl.BlockSpec((None, SUBLANES, CONV_DIM), lambda i, j: (i, 0, 0)),
            pl.BlockSpec((None, sk, DV_A), lambda i, j: (i, 0, 0)),
            pl.BlockSpec(convw.shape, const2),
            pl.BlockSpec(nega.shape, const2),
            pl.BlockSpec(dtb.shape, const2),
            pl.BlockSpec(gon.shape, const2),
            pl.BlockSpec(lbd.shape, const2),
        ],
        out_specs=[
            pl.BlockSpec((None, tile, VAL_DIM_A), lambda i, j: (i, j, 0)),
            pl.BlockSpec((None, sk, DV_A), lambda i, j: (i, 0, 0)),
            pl.BlockSpec((None, SUBLANES, CONV_DIM), lambda i, j: (i, 0, 0)),
        ],
        out_shape=[
            jax.ShapeDtypeStruct((b, t, VAL_DIM_A), F32),
            jax.ShapeDtypeStruct((b, sk, DV_A), F32),
            jax.ShapeDtypeStruct((b, SUBLANES, CONV_DIM), F32),
        ],
        scratch_shapes=[
            pltpu.VMEM((SUBLANES + tile, CONV_DIM), F32),
            pltpu.VMEM((sk, DV_A), F32),
        ],
        compiler_params=_cparams(("arbitrary", "arbitrary")),
        name="delta",
    )(qkvz, ba, cbuf, s0, convw, nega, dtb, gon, lbd)


def _pair_queries(qp):
    lane = lax.broadcasted_iota(jnp.int32, qp.shape, 1)
    q_even = jnp.where(lane < HD_B, qp, 0.0)
    q_odd = jnp.where(lane >= HD_B, qp, 0.0)
    return jnp.concatenate([q_even, q_odd], axis=0).astype(BF16)


def _unpair(o, r):
    lane = lax.broadcasted_iota(jnp.int32, (r, LANES), 1)
    return jnp.where(lane < HD_B, o[0:r, :], o[r:2 * r, :])


def _attn_prompt_kernel(q_ref, k_ref, v_ref, bias_ref, o_ref, kwin_ref, vwin_ref, *, g):
    cstep = pl.program_id(1)
    rows = g * CHUNK
    win = (N_BACK + g) * CHUNK

    @pl.when(cstep == 0)
    def _():
        kwin_ref[...] = jnp.zeros(kwin_ref.shape, BF16)
        vwin_ref[...] = jnp.zeros(vwin_ref.shape, BF16)

    for i in range(N_BACK // g):
        kwin_ref[i * rows:(i + 1) * rows, :] = kwin_ref[(i + 1) * rows:(i + 2) * rows, :]
        vwin_ref[i * rows:(i + 1) * rows, :] = vwin_ref[(i + 1) * rows:(i + 2) * rows, :]
    kwin_ref[win - rows:win, :] = k_ref[...].astype(BF16)
    vwin_ref[win - rows:win, :] = v_ref[...].astype(BF16)

    slot_chunk = lax.broadcasted_iota(jnp.int32, (2 * rows, win), 1) // CHUNK
    in_seq = slot_chunk >= N_BACK - cstep * g
    for pairs in ATT_PAIR_GROUPS:
        sls = [slice(p * LANES, (p + 1) * LANES) for p in pairs]
        s = [_mm_nt(_pair_queries(q_ref[:, sl] * (HD_B ** -0.5)), kwin_ref[:, sl]) for sl in sls]
        s = [jnp.where(in_seq, sp + bias_ref[p], NEG_BIG) for p, sp in zip(pairs, s)]
        m = [jnp.max(sp, axis=-1, keepdims=True) for sp in s]
        e = [jnp.exp(sp - mp) for sp, mp in zip(s, m)]
        l = [jnp.sum(ep, axis=-1, keepdims=True) for ep in e]
        o = [jnp.dot(ep.astype(BF16), vwin_ref[:, sl], preferred_element_type=F32) for ep, sl in zip(e, sls)]
        for sl, op, lp in zip(sls, o, l):
            o_ref[:, sl] = _unpair(op / lp, rows)


def _attend_prompt(qkvb, bias):
    b, t, _ = qkvb.shape
    g = ATT_G
    rows = g * CHUNK
    win = (N_BACK + g) * CHUNK
    kern = functools.partial(_attn_prompt_kernel, g=g)
    return pl.pallas_call(
        kern,
        grid=(b, t // rows),
        in_specs=[
            pl.BlockSpec((None, rows, ATT_DIM_B), lambda i, j: (i, j, 0)),
            pl.BlockSpec((None, rows, ATT_DIM_B), lambda i, j: (i, j, 1)),
            pl.BlockSpec((None, rows, ATT_DIM_B), lambda i, j: (i, j, 2)),
            pl.BlockSpec(bias.shape, lambda i, j: (0, 0, 0)),
        ],
        out_specs=pl.BlockSpec((None, rows, ATT_DIM_B), lambda i, j: (i, j, 0)),
        out_shape=jax.ShapeDtypeStruct((b, t, ATT_DIM_B), F32),
        scratch_shapes=[pltpu.VMEM((win, ATT_DIM_B), BF16), pltpu.VMEM((win, ATT_DIM_B), BF16)],
        compiler_params=_cparams(("arbitrary", "arbitrary")),
        name="attn_prompt",
    )(qkvb, qkvb, qkvb, bias)


def _attn_sample_kernel(q_ref, kn_ref, vn_ref, kc_ref, vc_ref, biasc_ref, biasn_ref, o_ref):
    ds = q_ref.shape[0]
    for p in range(H_B // 2):
        sl = slice(p * LANES, (p + 1) * LANES)
        q2 = _pair_queries(q_ref[:, sl] * (HD_B ** -0.5))
        s1 = _mm_nt(q2, kc_ref[:, sl]) + biasc_ref[p]
        s2 = _mm_nt(q2, kn_ref[:, sl]) + biasn_ref[p]
        m = jnp.maximum(jnp.max(s1, axis=-1, keepdims=True), jnp.max(s2, axis=-1, keepdims=True))
        e1 = jnp.exp(s1 - m)
        e2 = jnp.exp(s2 - m)
        l = jnp.sum(e1, axis=-1, keepdims=True) + jnp.sum(e2, axis=-1, keepdims=True)
        o = (_mm(e1, vc_ref[:, sl]) + _mm(e2, vn_ref[:, sl])) / l
        o_ref[:, sl] = _unpair(o, ds)


def _attend_sample(qkvb, cache_k, cache_v, biasc, biasn):
    b, ds, _ = qkvb.shape
    past = cache_k.shape[1]
    return pl.pallas_call(
        _attn_sample_kernel,
        grid=(b,),
        in_specs=[
            pl.BlockSpec((None, ds, ATT_DIM_B), lambda i: (i, 0, 0)),
            pl.BlockSpec((None, ds, ATT_DIM_B), lambda i: (i, 0, 1)),
            pl.BlockSpec((None, ds, ATT_DIM_B), lambda i: (i, 0, 2)),
            pl.BlockSpec((None, past, ATT_DIM_B), lambda i: (i, 0, 0)),
            pl.BlockSpec((None, past, ATT_DIM_B), lambda i: (i, 0, 0)),
            pl.BlockSpec(biasc.shape, lambda i: (0, 0, 0)),
            pl.BlockSpec(biasn.shape, lambda i: (0, 0, 0)),
        ],
        out_specs=pl.BlockSpec((None, ds, ATT_DIM_B), lambda i: (i, 0, 0)),
        out_shape=jax.ShapeDtypeStruct((b, ds, ATT_DIM_B), F32),
        compiler_params=_cparams(("arbitrary",)),
        name="attn_sample",
    )(qkvb, qkvb, qkvb, cache_k, cache_v, biasc, biasn)


def _router_kernel(oap_ref, obp_ref, xp_ref, oas_ref, obs_ref, xs_ref,
                   woa_ref, wob_ref, gffn_ref, wrt_ref, brt_ref, utri_ref,
                   x1_ref, h2_ref, idx_ref, gate_ref, rank_ref, cnt_ref,
                   carry_s, *, ntp):
    i = pl.program_id(0)
    tm = x1_ref.shape[0]

    @pl.when(i == 0)
    def _():
        carry_s[...] = jnp.zeros(carry_s.shape, F32)

    def mix(oa_ref, ob_ref, x_ref):
        x1_ref[...] = x_ref[...] + _mm(oa_ref[...], woa_ref[...]) + _mm(ob_ref[...], wob_ref[...])

    @pl.when(i < ntp)
    def _():
        mix(oap_ref, obp_ref, xp_ref)

    @pl.when(i >= ntp)
    def _():
        mix(oas_ref, obs_ref, xs_ref)

    x1 = x1_ref[...]
    h2 = _rms(x1, gffn_ref[...])
    _store_slabs(h2_ref, h2)
    logits = _mm_nt(wrt_ref[...], h2) + brt_ref[:, 0:1]

    eidx = lax.broadcasted_iota(jnp.int32, (N_EXPERTS, tm), 0).astype(F32)
    vals = logits
    memf = jnp.zeros((N_EXPERTS, tm), F32)
    tops, sels = [], []
    for k in range(TOP_K):
        m = jnp.max(vals, axis=0, keepdims=True)
        ix = jnp.min(jnp.where(vals == m, eidx, float(N_EXPERTS)), axis=0, keepdims=True)
        sel = eidx == ix
        tops.append(m)
        sels.append(sel)
        idx_ref[k:k + 1, :] = ix.astype(jnp.int32)
        vals = jnp.where(sel, -jnp.inf, vals)
        memf = memf + jnp.where(sel, 1.0, 0.0)
    es = [jnp.exp(v - tops[0]) for v in tops]
    denom = es[0] + es[1] + es[2] + es[3]
    for k in range(TOP_K):
        gate_ref[k:k + 1, :] = es[k] / denom

    carry = carry_s[:, 0:1]
    excl = _mm(memf, utri_ref[...]) + carry
    for k in range(TOP_K):
        r = jnp.sum(jnp.where(sels[k], excl, 0.0), axis=0, keepdims=True)
        rank_ref[k:k + 1, :] = r.astype(jnp.int32)
    new_carry = carry + jnp.sum(memf, axis=1, keepdims=True)
    carry_s[...] = jnp.broadcast_to(new_carry, carry_s.shape)
    cnt_ref[...] = jnp.broadcast_to(new_carry, cnt_ref.shape)


def _out_router(oa_p, ob_p, x_p, oa_s, ob_s, x_s, woa, wob, gffn, wrt, brt):
    tp, ts = x_p.shape[0], x_s.shape[0]
    tm = min(PROJ_TM, ts, tp)
    ntp, nts = tp // tm, ts // tm
    ttot = tp + ts
    utri = jnp.triu(jnp.ones((tm, tm), BF16), k=1)
    pmap = lambda i: (jnp.minimum(i, ntp - 1), 0)
    smap = lambda i: (jnp.maximum(i - ntp, 0), 0)
    const = lambda i: (0, 0)
    kern = functools.partial(_router_kernel, ntp=ntp)
    return pl.pallas_call(
        kern,
        grid=(ntp + nts,),
        in_specs=[
            pl.BlockSpec((tm, VAL_DIM_A), pmap),
            pl.BlockSpec((tm, ATT_DIM_B), pmap),
            pl.BlockSpec((tm, D_MODEL), pmap),
            pl.BlockSpec((tm, VAL_DIM_A), smap),
            pl.BlockSpec((tm, ATT_DIM_B), smap),
            pl.BlockSpec((tm, D_MODEL), smap),
            pl.BlockSpec(woa.shape, const),
            pl.BlockSpec(wob.shape, const),
            pl.BlockSpec(gffn.shape, const),
            pl.BlockSpec(wrt.shape, const),
            pl.BlockSpec(brt.shape, const),
            pl.BlockSpec(utri.shape, const),
        ],
        out_specs=[
            pl.BlockSpec((tm, D_MODEL), lambda i: (i, 0)),
            pl.BlockSpec((tm * NSLAB, LANES), lambda i: (i, 0)),
            pl.BlockSpec((TOP_K, tm), lambda i: (0, i)),
            pl.BlockSpec((TOP_K, tm), lambda i: (0, i)),
            pl.BlockSpec((TOP_K, tm), lambda i: (0, i)),
            pl.BlockSpec((N_EXPERTS, LANES), const),
        ],
        out_shape=[
            jax.ShapeDtypeStruct((ttot, D_MODEL), F32),
            jax.ShapeDtypeStruct((ttot * NSLAB, LANES), F32),
            jax.ShapeDtypeStruct((TOP_K, ttot), jnp.int32),
            jax.ShapeDtypeStruct((TOP_K, ttot), F32),
            jax.ShapeDtypeStruct((TOP_K, ttot), jnp.int32),
            jax.ShapeDtypeStruct((N_EXPERTS, LANES), F32),
        ],
        scratch_shapes=[pltpu.VMEM((N_EXPERTS, LANES), F32)],
        compiler_params=_cparams(("arbitrary",)),
        name="out_router",
    )(oa_p, ob_p, x_p, oa_s, ob_s, x_s, woa, wob, gffn, wrt, brt, utri)


def _row_copy(src_ref, s, dst_ref, d, sem):
    return pltpu.make_async_copy(_slab_rows(src_ref, s), _slab_rows(dst_ref, d), sem)


def _dispatch_kernel(padlo_ref, padhi_ref, dest_ref, h2_ref, xs_ref, zrow_ref, sem):
    i = pl.program_id(0)
    tm = h2_ref.shape[0] // NSLAB

    def issue(t, carry):
        for k in range(TOP_K):
            _row_copy(h2_ref, t, xs_ref, dest_ref[k, t], sem).start(priority=k % 2)
        return carry

    lax.fori_loop(0, tm, issue, 0, unroll=8)
    for k in range(TOP_K):
        pltpu.make_async_copy(h2_ref, xs_ref.at[pl.ds(0, tm * NSLAB), :], sem).wait()

    @pl.when(i == pl.num_programs(0) - 1)
    def _():
        zrow_ref[...] = jnp.zeros(zrow_ref.shape, F32)
        for e in range(N_EXPERTS):
            lo = padlo_ref[e]
            hi = padhi_ref[e]

            def zissue(r, carry):
                _row_copy(zrow_ref, 0, xs_ref, r, sem).start()
                return carry

            lax.fori_loop(lo, hi, zissue, 0)

            def zdrain(r, carry):
                _row_copy(zrow_ref, 0, xs_ref, 0, sem).wait()
                return carry

            lax.fori_loop(lo, hi, zdrain, 0)


def _dispatch(padlo, padhi, dest, h2, n_rows, tm_rows):
    ttot = h2.shape[0] // NSLAB
    tm = tm_rows
    grid_spec = pltpu.PrefetchScalarGridSpec(
        num_scalar_prefetch=2,
        grid=(ttot // tm,),
        in_specs=[
            pl.BlockSpec((TOP_K, tm), lambda i, lo, hi: (0, i), memory_space=pltpu.SMEM),
            pl.BlockSpec((tm * NSLAB, LANES), lambda i, lo, hi: (i, 0)),
        ],
        out_specs=pl.BlockSpec(memory_space=pl.ANY),
        scratch_shapes=[pltpu.VMEM((NSLAB, LANES), F32), pltpu.SemaphoreType.DMA(())],
    )
    return pl.pallas_call(
        _dispatch_kernel,
        grid_spec=grid_spec,
        out_shape=jax.ShapeDtypeStruct((n_rows * NSLAB, LANES), F32),
        compiler_params=_cparams(("arbitrary",)),
        name="dispatch",
    )(padlo, padhi, dest, h2)


def _expert_kernel(bexp_ref, nused_ref, x_ref, wgu_ref, bgu_ref, wd_ref, bd_ref, y_ref, wgu_s, wd_s):
    j = pl.program_id(0)
    active = j < nused_ref[0]
    new_expert = jnp.logical_or(j == 0, bexp_ref[j] != bexp_ref[jnp.maximum(j - 1, 0)])

    @pl.when(jnp.logical_and(active, new_expert))
    def _():
        rows = 128
        for r in range(0, D_MODEL, rows):
            wgu_s[r:r + rows, :] = wgu_ref[r:r + rows, :].astype(BF16)
        for r in range(0, D_FF, rows):
            wd_s[r:r + rows, :] = wd_ref[r:r + rows, :].astype(BF16)

    @pl.when(active)
    def _():
        bm = x_ref.shape[0] // NSLAB
        gu = _mm(_load_slabs(x_ref, bm), wgu_s[...]) + bgu_ref[...]
        gate = jnp.minimum(gu[:, 0:D_FF], SWIGLU_LIMIT)
        up = jnp.clip(gu[:, D_FF:], -SWIGLU_LIMIT, SWIGLU_LIMIT)
        hid = (up + 1.0) * (gate * _sigmoid(gate * SWIGLU_ALPHA))
        _store_slabs(y_ref, _mm(hid, wd_s[...]) + bd_ref[...])


def _experts(bexp, nused, xs, wgu, bgu, wd, bd):
    n_rows = xs.shape[0] // NSLAB
    bm = MOE_BM
    nb = n_rows // bm
    row_map = lambda j, be, nu: (jnp.minimum(j, nu[0] - 1), 0)
    w_map = lambda j, be, nu: (be[j], 0, 0)
    grid_spec = pltpu.PrefetchScalarGridSpec(
        num_scalar_prefetch=2,
        grid=(nb,),
        in_specs=[
            pl.BlockSpec((bm * NSLAB, LANES), row_map),
            pl.BlockSpec((None, D_MODEL, 2 * D_FF), w_map),
            pl.BlockSpec((None, 1, 2 * D_FF), w_map),
            pl.BlockSpec((None, D_FF, D_MODEL), w_map),
            pl.BlockSpec((None, 1, D_MODEL), w_map),
        ],
        out_specs=pl.BlockSpec((bm * NSLAB, LANES), row_map),
        scratch_shapes=[pltpu.VMEM((D_MODEL, 2 * D_FF), BF16), pltpu.VMEM((D_FF, D_MODEL), BF16)],
    )
    return pl.pallas_call(
        _expert_kernel,
        grid_spec=grid_spec,
        out_shape=jax.ShapeDtypeStruct((n_rows * NSLAB, LANES), F32),
        compiler_params=_cparams(("arbitrary",)),
        name="experts",
    )(bexp, nused, xs, wgu, bgu, wd, bd)


def _combine_kernel(dest_ref, destn_ref, gate_ref, x1_ref, pp_ref, ps_ref, yb_ref, gple_ref, wg_ref, wp_ref,
                    yp_ref, ys_ref, buf_ref, sems, *, ntp):
    i = pl.program_id(0)
    tm = x1_ref.shape[0]
    per_slot = TOP_K * tm
    slot = lax.rem(i, 2)

    def gather(d_ref, s):
        def issue(t, carry):
            for k in range(TOP_K):
                _row_copy(yb_ref, d_ref[k, t], buf_ref, s * per_slot + k * tm + t,
                          sems.at[s]).start(priority=k % 2)
            return carry

        lax.fori_loop(0, tm, issue, 0, unroll=8)

    @pl.when(i == 0)
    def _():
        gather(dest_ref, 0)

    @pl.when(i + 1 < pl.num_programs(0))
    def _():
        gather(destn_ref, 1 - slot)

    base = pl.multiple_of(slot * per_slot * NSLAB, NSLAB)
    pltpu.make_async_copy(yb_ref.at[pl.ds(0, per_slot * NSLAB), :],
                          buf_ref.at[pl.ds(base, per_slot * NSLAB), :], sems.at[slot]).wait()

    ng = COMB_GROUPS if tm % (COMB_GROUPS * SUBLANES) == 0 else 1
    gr = tm // ng
    groups = range(ng)

    def expert_rows(h, k):
        start = base + (k * tm + h * gr) * NSLAB
        return jnp.concatenate(
            [buf_ref[pl.ds(start + s, gr, stride=NSLAB), :] for s in range(NSLAB)], axis=1)

    gates = [gate_ref[h * gr:(h + 1) * gr, :] for h in groups]
    moe = [expert_rows(h, 0) * gates[h][:, 0:1] for h in groups]
    for k in range(1, TOP_K):
        moe = [moe[h] + expert_rows(h, k) * gates[h][:, k:k + 1] for h in groups]
    x2 = [x1_ref[h * gr:(h + 1) * gr, :] + moe[h] for h in groups]
    hn = [_rms(x2[h], gple_ref[...]) for h in groups]
    gate = [_sigmoid(_mm(hn[h], wg_ref[...])) for h in groups]

    def finish(p_ref, y_ref):
        proj = [_mm(p_ref[h * gr:(h + 1) * gr, :], wp_ref[...]) for h in groups]
        for h in groups:
            y_ref[h * gr:(h + 1) * gr, :] = x2[h] + gate[h] * proj[h]

    @pl.when(i < ntp)
    def _():
        finish(pp_ref, yp_ref)

    @pl.when(i >= ntp)
    def _():
        finish(ps_ref, ys_ref)


def _combine(dest, gates_col, x1, p_p, p_s, yb, gple, wg, wp):
    tp, ts = p_p.shape[0], p_s.shape[0]
    tm = min(COMB_TM, tp, ts)
    ntp, nts = tp // tm, ts // tm
    pmap = lambda i: (jnp.minimum(i, ntp - 1), 0)
    smap = lambda i: (jnp.maximum(i - ntp, 0), 0)
    const = lambda i: (0, 0)
    kern = functools.partial(_combine_kernel, ntp=ntp)
    return pl.pallas_call(
        kern,
        grid=(ntp + nts,),
        in_specs=[
            pl.BlockSpec((TOP_K, tm), lambda i: (0, i), memory_space=pltpu.SMEM),
            pl.BlockSpec((TOP_K, tm), lambda i: (0, jnp.minimum(i + 1, ntp + nts - 1)),
                         memory_space=pltpu.SMEM),
            pl.BlockSpec((tm, TOP_K), lambda i: (i, 0)),
            pl.BlockSpec((tm, D_MODEL), lambda i: (i, 0)),
            pl.BlockSpec((tm, PLE_DIM), pmap),
            pl.BlockSpec((tm, PLE_DIM), smap),
            pl.BlockSpec(memory_space=pl.ANY),
            pl.BlockSpec(gple.shape, const),
            pl.BlockSpec(wg.shape, const),
            pl.BlockSpec(wp.shape, const),
        ],
        out_specs=[
            pl.BlockSpec((tm, D_MODEL), pmap),
            pl.BlockSpec((tm, D_MODEL), smap),
        ],
        out_shape=[
            jax.ShapeDtypeStruct((tp, D_MODEL), F32),
            jax.ShapeDtypeStruct((ts, D_MODEL), F32),
        ],
        scratch_shapes=[pltpu.VMEM((2 * TOP_K * tm * NSLAB, LANES), F32), pltpu.SemaphoreType.DMA((2,))],
        compiler_params=_cparams(("arbitrary",)),
        name="combine",
    )(dest, dest, gates_col, x1, p_p, p_s, yb, gple, wg, wp)


def _bias_tables(rel_bias, n_q, n_k, key_offset, band):
    i = np.arange(n_q)[:, None]
    j = np.arange(n_k)[None, :]
    d_max = n_q - 1 + key_offset
    d_min = key_offset - (n_k - 1)
    n_hi = max(0, d_max - REL_MAX)
    n_lo = max(0, -REL_MAX - d_min)
    mid = rel_bias[:, max(d_min, -REL_MAX) + REL_MAX:min(d_max, REL_MAX) + REL_MAX + 1][:, ::-1]
    e = jnp.concatenate([jnp.broadcast_to(rel_bias[:, 2 * REL_MAX:], (H_B, n_hi)), mid,
                         jnp.broadcast_to(rel_bias[:, 0:1], (H_B, n_lo))], axis=1)
    period = n_q + n_k
    e = jnp.pad(e, ((0, 0), (0, 1)))
    flat = jnp.tile(e, (1, n_q + 1))[:, 0:n_q * (period + 1)]
    tab = flat.reshape(H_B, n_q, period + 1)[:, ::-1, 0:n_k]
    if band:
        qc = i // CHUNK
        sc = j // CHUNK
        ok = (sc >= qc) & (sc <= qc + N_BACK)
        tab = jnp.where(jnp.asarray(ok)[None], tab, NEG_BIG)
    return tab.reshape(H_B // 2, 2 * n_q, n_k)


def kernel(x_prompt, x_sample, state_delta, state_conv, cache_k, cache_v, p_prompt, p_sample, g_mix, w_in, conv_w, a_log, dt_bias, g_onorm, g_qnorm, g_knorm, rel_bias, w_out, g_ffn, w_router, b_router, w_gu, b_gu, w_down, b_down, g_ple, w_ple_gate, w_ple_proj):
    assert w_in.shape[0] == 1, "single-layer kernel"
    bp, tp_len, _ = x_prompt.shape
    bs, ts_len, _ = x_sample.shape
    tp, ts = bp * tp_len, bs * ts_len
    past = cache_k.shape[2]

    w = w_in[0]
    o_z = CONV_DIM + VAL_DIM_A
    o_qb = o_z + 2 * H_A
    wa = w[:, 0:o_z].astype(BF16)
    wba = jnp.pad(w[:, o_z:o_qb], ((0, 0), (0, LANES - 2 * H_A))).astype(BF16)
    wb = w[:, o_qb:].astype(BF16)
    grp = np.arange(ATT_DIM_B) // HD_B
    bd = jnp.asarray(grp[:, None] == grp[None, :], BF16)
    gq = jnp.tile(g_qnorm[0], H_B)[None, :]
    gk = jnp.tile(g_knorm[0], H_B)[None, :]
    gmix = g_mix[0][None, :]
    gon = g_onorm[0][None, :]
    convw = conv_w[0]
    woa = w_out[0, 0:VAL_DIM_A].astype(BF16)
    wob = w_out[0, VAL_DIM_A:].astype(BF16)
    gffn = g_ffn[0][None, :]
    wrt = w_router[0].T.astype(BF16)
    brt = jnp.broadcast_to(b_router[0][:, None], (N_EXPERTS, LANES))
    wgu = w_gu[0]
    bgu = b_gu[0][:, None, :]
    wd = w_down[0]
    bdn = b_down[0][:, None, :]
    gple = g_ple[0][None, :]
    wg = w_ple_gate[0].astype(BF16)
    wp = w_ple_proj[0].astype(BF16)

    xp2 = x_prompt.reshape(tp, D_MODEL)
    xs2 = x_sample.reshape(ts, D_MODEL)

    qkvz_p, ba_p, qkvb_p = _project(xp2, gmix, wa, wba, wb, bd, gq, gk)
    qkvz_s, ba_s, qkvb_s = _project(xs2, gmix, wa, wba, wb, bd, gq, gk)

    hist_pad = ((0, 0), (SUBLANES - (CONV_W - 1), 0), (0, 0))
    cbuf_p = jnp.zeros((bp, SUBLANES, CONV_DIM), F32)
    cbuf_s = jnp.pad(state_conv[0], hist_pad)
    s0_p = jnp.zeros((bp, H_A * DK_A, DV_A), F32)
    s0_s = state_delta[0].reshape(bs, H_A * DK_A, DV_A)
    oa_p, sfin_p, cnew_p = _delta_mixer(qkvz_p.reshape(bp, tp_len, -1), ba_p.reshape(bp, tp_len, LANES),
                                        cbuf_p, s0_p, convw, a_log[0], dt_bias[0], gon)
    oa_s, sfin_s, cnew_s = _delta_mixer(qkvz_s.reshape(bs, ts_len, -1), ba_s.reshape(bs, ts_len, LANES),
                                        cbuf_s, s0_s, convw, a_log[0], dt_bias[0], gon)
    sfin_p = sfin_p.reshape(bp, H_A, DK_A, DV_A)
    sfin_s = sfin_s.reshape(bs, H_A, DK_A, DV_A)

    qkvb_p3 = qkvb_p.reshape(bp, tp_len, 3 * ATT_DIM_B)
    qkvb_s3 = qkvb_s.reshape(bs, ts_len, 3 * ATT_DIM_B)
    bias_p = _bias_tables(rel_bias[0], ATT_G * CHUNK, (N_BACK + ATT_G) * CHUNK, BAND_PAST, True)
    ob_p = _attend_prompt(qkvb_p3, bias_p)
    bias_s = _bias_tables(rel_bias[0], ts_len, past + ts_len, past, False)
    ob_s = _attend_sample(qkvb_s3, cache_k[0].reshape(bs, past, ATT_DIM_B),
                          cache_v[0].reshape(bs, past, ATT_DIM_B),
                          bias_s[:, :, 0:past], bias_s[:, :, past:])

    x1, h2, idx, gates, rank, cnt = _out_router(
        oa_p.reshape(tp, VAL_DIM_A), ob_p.reshape(tp, ATT_DIM_B), xp2,
        oa_s.reshape(ts, VAL_DIM_A), ob_s.reshape(ts, ATT_DIM_B), xs2,
        woa, wob, gffn, wrt, brt)
    ttot = tp + ts
    bm = MOE_BM
    counts = cnt[:, 0].astype(jnp.int32)
    padded = (counts + bm - 1) // bm * bm
    pend = jnp.cumsum(padded)
    pstart = pend - padded
    eids = jnp.arange(N_EXPERTS, dtype=jnp.int32)
    start_of = jnp.sum(jnp.where(idx[None] == eids[:, None, None], pstart[:, None, None], 0), axis=0)
    dest = start_of + rank
    nb = -(-(ttot * TOP_K) // bm) + N_EXPERTS
    nused = (pend[-1] // bm).astype(jnp.int32)[None]
    first = jnp.minimum(jnp.arange(nb, dtype=jnp.int32), nused[0] - 1) * bm
    bexp = jnp.minimum(jnp.sum((pend[None, :] <= first[:, None]).astype(jnp.int32), axis=1),
                       N_EXPERTS - 1)

    xs_rows = _dispatch(pstart + counts, pend, dest, h2, nb * bm, min(COMB_TM, tp, ts))
    yb = _experts(bexp, nused, xs_rows, wgu, bgu, wd, bdn)
    y_p, y_s = _combine(dest, gates.T, x1, p_prompt[0].reshape(tp, PLE_DIM),
                        p_sample[0].reshape(ts, PLE_DIM), yb, gple, wg, wp)

    keep = min(BAND_PAST, tp_len)
    k_p = qkvb_p3[:, tp_len - keep:, ATT_DIM_B:2 * ATT_DIM_B].reshape(1, bp, keep, H_B, HD_B)
    v_p = qkvb_p3[:, tp_len - keep:, 2 * ATT_DIM_B:].reshape(1, bp, keep, H_B, HD_B)
    k_s = qkvb_s3[:, :, ATT_DIM_B:2 * ATT_DIM_B].reshape(1, bs, ts_len, H_B, HD_B)
    v_s = qkvb_s3[:, :, 2 * ATT_DIM_B:].reshape(1, bs, ts_len, H_B, HD_B)
    nconv = CONV_W - 1
    return (y_p.reshape(bp, tp_len, D_MODEL), y_s.reshape(bs, ts_len, D_MODEL),
            sfin_p[None], cnew_p[None, :, SUBLANES - nconv:, :], k_p, v_p,
            sfin_s[None], cnew_s[None, :, SUBLANES - nconv:, :], k_s, v_s)
```

```python
import functools

import jax
import jax.numpy as jnp
import numpy as np
from jax import lax
from jax.experimental import pallas as pl
from jax.experimental.pallas import tpu as pltpu

F32 = jnp.float32
BF16 = jnp.bfloat16

D_MODEL = 1024
CHUNK = 64
H_A = 4
DK_A = 128
DV_A = 128
CONV_W = 4
KEY_DIM_A = H_A * DK_A
VAL_DIM_A = H_A * DV_A
CONV_DIM = 2 * KEY_DIM_A + VAL_DIM_A
H_B = 8
HD_B = 64
ATT_DIM_B = H_B * HD_B
N_BACK = 8
BAND_PAST = N_BACK * CHUNK
REL_MAX = 256
N_EXPERTS = 32
TOP_K = 4
D_FF = 1024
SWIGLU_LIMIT = 7.0
SWIGLU_ALPHA = 1.702
PLE_DIM = 256
RMS_EPS = 1e-6
L2_EPS = 1e-6
NEG_BIG = -1e30

LANES = 128
SUBLANES = 8
VMEM_LIMIT = 56 * 1024 * 1024

PROJ_TM = 512
ATT_G = 2
ATT_PAIR_GROUPS = ((0, 1), (2, 3))
DELTA_NCH = 4
MOE_BM = 512
COMB_TM = 512
COMB_GROUPS = 4


def _mm(a, b):
    return jnp.dot(a.astype(BF16), b.astype(BF16), preferred_element_type=F32)


def _mm_nt(a, b):
    return lax.dot_general(a.astype(BF16), b.astype(BF16), (((1,), (1,)), ((), ())),
                           preferred_element_type=F32)


def _sigmoid(x):
    return 1.0 / (1.0 + jnp.exp(-x))


def _softplus(x):
    return jnp.maximum(x, 0.0) + jnp.log(1.0 + jnp.exp(-jnp.abs(x)))


def _rms(x, g):
    ms = jnp.mean(x * x, axis=-1, keepdims=True)
    return x * lax.rsqrt(ms + RMS_EPS) * g


NSLAB = D_MODEL // LANES


def _store_slabs(ref, val):
    n = val.shape[0]
    for s in range(NSLAB):
        ref[pl.ds(s, n, stride=NSLAB), :] = val[:, s * LANES:(s + 1) * LANES]


def _load_slabs(ref, n):
    return jnp.concatenate([ref[pl.ds(s, n, stride=NSLAB), :] for s in range(NSLAB)], axis=1)


def _slab_rows(ref, r):
    return ref.at[pl.ds(pl.multiple_of(r * NSLAB, NSLAB), NSLAB), :]


def _cparams(sem):
    return pltpu.CompilerParams(dimension_semantics=sem, vmem_limit_bytes=VMEM_LIMIT)


def _proj_kernel(x_ref, gmix_ref, wa_ref, wba_ref, wb_ref, bd_ref, gq_ref, gk_ref,
                 qkvz_ref, ba_ref, qkvb_ref):
    h = _rms(x_ref[...], gmix_ref[...]).astype(BF16)
    qkvz_ref[...] = jnp.dot(h, wa_ref[...], preferred_element_type=F32)
    ba_ref[...] = jnp.dot(h, wba_ref[...], preferred_element_type=F32)
    pb = jnp.dot(h, wb_ref[...], preferred_element_type=F32)
    bd = bd_ref[...]

    def head_norm(q, g):
        ss = jnp.dot((q * q).astype(BF16), bd, preferred_element_type=F32)
        return q * lax.rsqrt(ss * (1.0 / HD_B) + RMS_EPS) * g

    qkvb_ref[:, 0:ATT_DIM_B] = head_norm(pb[:, 0:ATT_DIM_B], gq_ref[...])
    qkvb_ref[:, ATT_DIM_B:2 * ATT_DIM_B] = head_norm(pb[:, ATT_DIM_B:2 * ATT_DIM_B], gk_ref[...])
    qkvb_ref[:, 2 * ATT_DIM_B:] = pb[:, 2 * ATT_DIM_B:]


def _project(x2d, gmix, wa, wba, wb, bd, gq, gk):
    t = x2d.shape[0]
    tm = min(PROJ_TM, t)
    const = lambda i: (0, 0)
    return pl.pallas_call(
        _proj_kernel,
        grid=(t // tm,),
        in_specs=[
            pl.BlockSpec((tm, D_MODEL), lambda i: (i, 0)),
            pl.BlockSpec(gmix.shape, const),
            pl.BlockSpec(wa.shape, const),
            pl.BlockSpec(wba.shape, const),
            pl.BlockSpec(wb.shape, const),
            pl.BlockSpec(bd.shape, const),
            pl.BlockSpec(gq.shape, const),
            pl.BlockSpec(gk.shape, const),
        ],
        out_specs=[
            pl.BlockSpec((tm, wa.shape[1]), lambda i: (i, 0)),
            pl.BlockSpec((tm, LANES), lambda i: (i, 0)),
            pl.BlockSpec((tm, wb.shape[1]), lambda i: (i, 0)),
        ],
        out_shape=[
            jax.ShapeDtypeStruct((t, wa.shape[1]), F32),
            jax.ShapeDtypeStruct((t, LANES), F32),
            jax.ShapeDtypeStruct((t, wb.shape[1]), F32),
        ],
        compiler_params=_cparams(("arbitrary",)),
        name="proj",
    )(x2d, gmix, wa, wba, wb, bd, gq, gk)


def _delta_kernel(qkvz_ref, ba_ref, cbuf_ref, s0_ref, convw_ref, nega_ref, dtb_ref,
                  gon_ref, lbd_ref,
                  o_ref, sfin_ref, cnew_ref, xh_ref, s_ref, *, c, nch):
    step = pl.program_id(1)
    tile = c * nch
    hist = SUBLANES

    @pl.when(step == 0)
    def _():
        xh_ref[0:hist, :] = cbuf_ref[...]
        s_ref[...] = s0_ref[...]

    xh_ref[hist:hist + tile, :] = qkvz_ref[:, 0:CONV_DIM]
    w = convw_ref[...]
    conv = (xh_ref[hist - 3:hist - 3 + tile, :] * w[0:1, :]
            + xh_ref[hist - 2:hist - 2 + tile, :] * w[1:2, :]
            + xh_ref[hist - 1:hist - 1 + tile, :] * w[2:3, :]
            + xh_ref[hist:hist + tile, :] * w[3:4, :])
    conv = conv * _sigmoid(conv)
    last_rows = xh_ref[tile:tile + hist, :]
    cnew_ref[...] = last_rows
    xh_ref[0:hist, :] = last_rows

    ba = ba_ref[...]
    z_all = qkvz_ref[:, CONV_DIM:CONV_DIM + VAL_DIM_A]

    rs = H_A * c
    sk = H_A * DK_A
    row = lax.broadcasted_iota(jnp.int32, (rs, rs), 0)
    col = lax.broadcasted_iota(jnp.int32, (rs, rs), 1)
    same = (row // c) == (col // c)
    causal = jnp.logical_and(same, row >= col)
    strict = jnp.logical_and(same, row > col)
    wrow = lax.broadcasted_iota(jnp.int32, (rs, sk), 0)
    wcol = lax.broadcasted_iota(jnp.int32, (rs, sk), 1)
    head_block = (wrow // c) == (wcol // DK_A)
    lbd = lbd_ref[...]
    nsq = int(np.log2(c))

    def stack(fn):
        return jnp.concatenate([fn(h) for h in range(H_A)], axis=0)

    def spread(m):
        return jnp.where(head_block, jnp.concatenate([m] * H_A, axis=1), 0.0)

    def mm_split3(a_bf16, x):
        hi = x.astype(BF16)
        r1 = x - hi.astype(F32)
        mid = r1.astype(BF16)
        lo = (r1 - mid.astype(F32)).astype(BF16)
        return (jnp.dot(a_bf16, hi, preferred_element_type=F32)
                + jnp.dot(a_bf16, mid, preferred_element_type=F32)
                + jnp.dot(a_bf16, lo, preferred_element_type=F32))

    def l2n(m):
        return m * lax.rsqrt(jnp.sum(m * m, axis=-1, keepdims=True) + L2_EPS)

    def last_row(m, h, shape):
        return jnp.broadcast_to(m[h * c + c - 1:h * c + c, :], shape)

    cs = range(nch)
    q_st = [l2n(stack(lambda h: conv[ci * c:ci * c + c, h * DK_A:(h + 1) * DK_A])) * (DK_A ** -0.5)
            for ci in cs]
    k_st = [l2n(stack(lambda h: conv[ci * c:ci * c + c, KEY_DIM_A + h * DK_A:KEY_DIM_A + (h + 1) * DK_A]))
            for ci in cs]
    v_st = [stack(lambda h: conv[ci * c:ci * c + c, 2 * KEY_DIM_A + h * DV_A:2 * KEY_DIM_A + (h + 1) * DV_A])
            for ci in cs]
    beta = [_sigmoid(stack(lambda h: jnp.broadcast_to(ba[ci * c:ci * c + c, h:h + 1], (c, LANES))))
            for ci in cs]
    g = [nega_ref[...] * _softplus(
        stack(lambda h: jnp.broadcast_to(ba[ci * c:ci * c + c, H_A + h:H_A + h + 1], (c, LANES)))
        + dtb_ref[...]) for ci in cs]
    gc = [mm_split3(lbd, gi) for gi in g]
    kb = [k * b for k, b in zip(k_st, beta)]
    kk = [_mm_nt(a, b) for a, b in zip(kb, k_st)]
    qk = [_mm_nt(a, b) for a, b in zip(q_st, k_st)]
    decay = [jnp.exp(jnp.where(causal,
                               jnp.concatenate([m] * (rs // LANES), axis=1)
                               - jnp.broadcast_to(m.T[0:1, :], (rs, rs)),
                               -jnp.inf)) for m in gc]
    egc = [jnp.exp(m) for m in gc]
    ps = [-jnp.where(strict, a * d, 0.0) for a, d in zip(kk, decay)]
    qk = [a * d for a, d in zip(qk, decay)]
    xs = [jnp.concatenate([v * b, k * e], axis=-1) for v, b, k, e in zip(v_st, beta, kb, egc)]
    for s in range(nsq):
        xs = [x + _mm(p, x) for x, p in zip(xs, ps)]
        if s + 1 < nsq:
            ps = [_mm(p, p) for p in ps]
    qbd = [spread(q * e) for q, e in zip(q_st, egc)]
    kbd = [spread(k * jnp.exp(stack(lambda h: last_row(m, h, (c, LANES))) - m)) for k, m in zip(k_st, gc)]
    egl = [stack(lambda h: last_row(e, h, (DK_A, DV_A))) for e in egc]

    for ci in range(nch):
        r0 = ci * c
        eglast = egl[ci]
        u0 = xs[ci][:, 0:DV_A]
        wbd = spread(xs[ci][:, DV_A:])
        s_all = s_ref[...]
        u = u0 - _mm(wbd, s_all)
        o = _mm(qbd[ci], s_all) + _mm(qk[ci], u)
        s_ref[...] = s_all * eglast + lax.dot_general(
            kbd[ci].astype(BF16), u.astype(BF16), (((0,), (0,)), ((), ())), preferred_element_type=F32)
        on = _rms(o, gon_ref[...])
        for h in range(H_A):
            z = z_all[r0:r0 + c, h * DV_A:(h + 1) * DV_A]
            o_ref[r0:r0 + c, h * DV_A:(h + 1) * DV_A] = on[h * c:(h + 1) * c, :] * (z * _sigmoid(z))

    sfin_ref[...] = s_ref[...]


def _delta_mixer(qkvz, ba, cbuf, s0, convw, a_log, dt_bias, gon):
    b, t, _ = qkvz.shape
    c = min(CHUNK, t)
    nch = min(DELTA_NCH, t // c)
    tile = c * nch
    rs = H_A * c
    sk = H_A * DK_A
    assert rs % LANES == 0
    r = np.arange(rs)
    lbd = jnp.asarray((r[:, None] // c == r[None, :] // c) & (r[:, None] >= r[None, :]), BF16)
    nega = jnp.broadcast_to(jnp.repeat(-jnp.exp(a_log), c)[:, None], (rs, LANES))
    dtb = jnp.broadcast_to(jnp.repeat(dt_bias, c)[:, None], (rs, LANES))
    const2 = lambda i, j: (0, 0)
    kern = functools.partial(_delta_kernel, c=c, nch=nch)
    return pl.pallas_call(
        kern,
        grid=(b, t // tile),
        in_specs=[
            pl.BlockSpec((None, tile, qkvz.shape[2]), lambda i, j: (i, j, 0)),
            pl.BlockSpec((None, tile, LANES), lambda i, j: (i, j, 0)),
            pl.BlockSpec((None, SUBLANES, CONV_DIM), lambda i, j: (i, 0, 0)),
            pl.BlockSpec((None, sk, DV_A), lambda i, j: (i, 0, 0)),
            pl.BlockSpec(convw.shape, const2),
            pl.BlockSpec(nega.shape, const2),
            pl.BlockSpec(dtb.shape, const2),
            pl.BlockSpec(gon.shape, const2),
            pl.BlockSpec(lbd.shape, const2),
        ],
        out_specs=[
            pl.BlockSpec((None, tile, VAL_DIM_A), lambda i, j: (i, j, 0)),
            pl.BlockSpec((None, sk, DV_A), lambda i, j: (i, 0, 0)),
            pl.BlockSpec((None, SUBLANES, CONV_DIM), lambda i, j: (i, 0, 0)),
        ],
        out_shape=[
            jax.ShapeDtypeStruct((b, t, VAL_DIM_A), F32),
            jax.ShapeDtypeStruct((b, sk, DV_A), F32),
            jax.ShapeDtypeStruct((b, SUBLANES, CONV_DIM), F32),
        ],
        scratch_shapes=[
            pltpu.VMEM((SUBLANES + tile, CONV_DIM), F32),
            pltpu.VMEM((sk, DV_A), F32),
        ],
        compiler_params=_cparams(("arbitrary", "arbitrary")),
        name="delta",
    )(qkvz, ba, cbuf, s0, convw, nega, dtb, gon, lbd)


def _pair_queries(qp):
    lane = lax.broadcasted_iota(jnp.int32, qp.shape, 1)
    q_even = jnp.where(lane < HD_B, qp, 0.0)
    q_odd = jnp.where(lane >= HD_B, qp, 0.0)
    return jnp.concatenate([q_even, q_odd], axis=0).astype(BF16)


def _unpair(o, r):
    lane = lax.broadcasted_iota(jnp.int32, (r, LANES), 1)
    return jnp.where(lane < HD_B, o[0:r, :], o[r:2 * r, :])


def _attn_prompt_kernel(q_ref, k_ref, v_ref, bias_ref, o_ref, kwin_ref, vwin_ref, *, g):
    cstep = pl.program_id(1)
    rows = g * CHUNK
    win = (N_BACK + g) * CHUNK

    @pl.when(cstep == 0)
    def _():
        kwin_ref[...] = jnp.zeros(kwin_ref.shape, BF16)
        vwin_ref[...] = jnp.zeros(vwin_ref.shape, BF16)

    for i in range(N_BACK // g):
        kwin_ref[i * rows:(i + 1) * rows, :] = kwin_ref[(i + 1) * rows:(i + 2) * rows, :]
        vwin_ref[i * rows:(i + 1) * rows, :] = vwin_ref[(i + 1) * rows:(i + 2) * rows, :]
    kwin_ref[win - rows:win, :] = k_ref[...].astype(BF16)
    vwin_ref[win - rows:win, :] = v_ref[...].astype(BF16)

    slot_chunk = lax.broadcasted_iota(jnp.int32, (2 * rows, win), 1) // CHUNK
    in_seq = slot_chunk >= N_BACK - cstep * g
    for pairs in ATT_PAIR_GROUPS:
        sls = [slice(p * LANES, (p + 1) * LANES) for p in pairs]
        s = [_mm_nt(_pair_queries(q_ref[:, sl] * (HD_B ** -0.5)), kwin_ref[:, sl]) for sl in sls]
        s = [jnp.where(in_seq, sp + bias_ref[p], NEG_BIG) for p, sp in zip(pairs, s)]
        m = [jnp.max(sp, axis=-1, keepdims=True) for sp in s]
        e = [jnp.exp(sp - mp) for sp, mp in zip(s, m)]
        l = [jnp.sum(ep, axis=-1, keepdims=True) for ep in e]
        o = [jnp.dot(ep.astype(BF16), vwin_ref[:, sl], preferred_element_type=F32) for ep, sl in zip(e, sls)]
        for sl, op, lp in zip(sls, o, l):
            o_ref[:, sl] = _unpair(op / lp, rows)


def _attend_prompt(qkvb, bias):
    b, t, _ = qkvb.shape
    g = ATT_G
    rows = g * CHUNK
    win = (N_BACK + g) * CHUNK
    kern = functools.partial(_attn_prompt_kernel, g=g)
    return pl.pallas_call(
        kern,
        grid=(b, t // rows),
        in_specs=[
            pl.BlockSpec((None, rows, ATT_DIM_B), lambda i, j: (i, j, 0)),
            pl.BlockSpec((None, rows, ATT_DIM_B), lambda i, j: (i, j, 1)),
            pl.BlockSpec((None, rows, ATT_DIM_B), lambda i, j: (i, j, 2)),
            pl.BlockSpec(bias.shape, lambda i, j: (0, 0, 0)),
        ],
        out_specs=pl.BlockSpec((None, rows, ATT_DIM_B), lambda i, j: (i, j, 0)),
        out_shape=jax.ShapeDtypeStruct((b, t, ATT_DIM_B), F32),
        scratch_shapes=[pltpu.VMEM((win, ATT_DIM_B), BF16), pltpu.VMEM((win, ATT_DIM_B), BF16)],
        compiler_params=_cparams(("arbitrary", "arbitrary")),
        name="attn_prompt",
    )(qkvb, qkvb, qkvb, bias)


def _attn_sample_kernel(q_ref, kn_ref, vn_ref, kc_ref, vc_ref, biasc_ref, biasn_ref, o_ref):
    ds = q_ref.shape[0]
    for p in range(H_B // 2):
        sl = slice(p * LANES, (p + 1) * LANES)
        q2 = _pair_queries(q_ref[:, sl] * (HD_B ** -0.5))
        s1 = _mm_nt(q2, kc_ref[:, sl]) + biasc_ref[p]
        s2 = _mm_nt(q2, kn_ref[:, sl]) + biasn_ref[p]
        m = jnp.maximum(jnp.max(s1, axis=-1, keepdims=True), jnp.max(s2, axis=-1, keepdims=True))
        e1 = jnp.exp(s1 - m)
        e2 = jnp.exp(s2 - m)
        l = jnp.sum(e1, axis=-1, keepdims=True) + jnp.sum(e2, axis=-1, keepdims=True)
        o = (_mm(e1, vc_ref[:, sl]) + _mm(e2, vn_ref[:, sl])) / l
        o_ref[:, sl] = _unpair(o, ds)


def _attend_sample(qkvb, cache_k, cache_v, biasc, biasn):
    b, ds, _ = qkvb.shape
    past = cache_k.shape[1]
    return pl.pallas_call(
        _attn_sample_kernel,
        grid=(b,),
        in_specs=[
            pl.BlockSpec((None, ds, ATT_DIM_B), lambda i: (i, 0, 0)),
            pl.BlockSpec((None, ds, ATT_DIM_B), lambda i: (i, 0, 1)),
            pl.BlockSpec((None, ds, ATT_DIM_B), lambda i: (i, 0, 2)),
            pl.BlockSpec((None, past, ATT_DIM_B), lambda i: (i, 0, 0)),
            pl.BlockSpec((None, past, ATT_DIM_B), lambda i: (i, 0, 0)),
            pl.BlockSpec(biasc.shape, lambda i: (0, 0, 0)),
            pl.BlockSpec(biasn.shape, lambda i: (0, 0, 0)),
        ],
        out_specs=pl.BlockSpec((None, ds, ATT_DIM_B), lambda i: (i, 0, 0)),
        out_shape=jax.ShapeDtypeStruct((b, ds, ATT_DIM_B), F32),
        compiler_params=_cparams(("arbitrary",)),
        name="attn_sample",
    )(qkvb, qkvb, qkvb, cache_k, cache_v, biasc, biasn)


def _router_kernel(oap_ref, obp_ref, xp_ref, oas_ref, obs_ref, xs_ref,
                   woa_ref, wob_ref, gffn_ref, wrt_ref, brt_ref, utri_ref,
                   x1_ref, h2_ref, idx_ref, gate_ref, rank_ref, cnt_ref,
                   carry_s, *, ntp):
    i = pl.program_id(0)
    tm = x1_ref.shape[0]

    @pl.when(i == 0)
    def _():
        carry_s[...] = jnp.zeros(carry_s.shape, F32)

    def mix(oa_ref, ob_ref, x_ref):
        x1_ref[...] = x_ref[...] + _mm(oa_ref[...], woa_ref[...]) + _mm(ob_ref[...], wob_ref[...])

    @pl.when(i < ntp)
    def _():
        mix(oap_ref, obp_ref, xp_ref)

    @pl.when(i >= ntp)
    def _():
        mix(oas_ref, obs_ref, xs_ref)

    x1 = x1_ref[...]
    h2 = _rms(x1, gffn_ref[...])
    _store_slabs(h2_ref, h2)
    logits = _mm_nt(wrt_ref[...], h2) + brt_ref[:, 0:1]

    eidx = lax.broadcasted_iota(jnp.int32, (N_EXPERTS, tm), 0).astype(F32)
    vals = logits
    memf = jnp.zeros((N_EXPERTS, tm), F32)
    tops, sels = [], []
    for k in range(TOP_K):
        m = jnp.max(vals, axis=0, keepdims=True)
        ix = jnp.min(jnp.where(vals == m, eidx, float(N_EXPERTS)), axis=0, keepdims=True)
        sel = eidx == ix
        tops.append(m)
        sels.append(sel)
        idx_ref[k:k + 1, :] = ix.astype(jnp.int32)
        vals = jnp.where(sel, -jnp.inf, vals)
        memf = memf + jnp.where(sel, 1.0, 0.0)
    es = [jnp.exp(v - tops[0]) for v in tops]
    denom = es[0] + es[1] + es[2] + es[3]
    for k in range(TOP_K):
        gate_ref[k:k + 1, :] = es[k] / denom

    carry = carry_s[:, 0:1]
    excl = _mm(memf, utri_ref[...]) + carry
    for k in range(TOP_K):
        r = jnp.sum(jnp.where(sels[k], excl, 0.0), axis=0, keepdims=True)
        rank_ref[k:k + 1, :] = r.astype(jnp.int32)
    new_carry = carry + jnp.sum(memf, axis=1, keepdims=True)
    carry_s[...] = jnp.broadcast_to(new_carry, carry_s.shape)
    cnt_ref[...] = jnp.broadcast_to(new_carry, cnt_ref.shape)


def _out_router(oa_p, ob_p, x_p, oa_s, ob_s, x_s, woa, wob, gffn, wrt, brt):
    tp, ts = x_p.shape[0], x_s.shape[0]
    tm = min(PROJ_TM, ts, tp)
    ntp, nts = tp // tm, ts // tm
    ttot = tp + ts
    utri = jnp.triu(jnp.ones((tm, tm), BF16), k=1)
    pmap = lambda i: (jnp.minimum(i, ntp - 1), 0)
    smap = lambda i: (jnp.maximum(i - ntp, 0), 0)
    const = lambda i: (0, 0)
    kern = functools.partial(_router_kernel, ntp=ntp)
    return pl.pallas_call(
        kern,
        grid=(ntp + nts,),
        in_specs=[
            pl.BlockSpec((tm, VAL_DIM_A), pmap),
            pl.BlockSpec((tm, ATT_DIM_B), pmap),
            pl.BlockSpec((tm, D_MODEL), pmap),
            pl.BlockSpec((tm, VAL_DIM_A), smap),
            pl.BlockSpec((tm, ATT_DIM_B), smap),
            pl.BlockSpec((tm, D_MODEL), smap),
            pl.BlockSpec(woa.shape, const),
            pl.BlockSpec(wob.shape, const),
            pl.BlockSpec(gffn.shape, const),
            pl.BlockSpec(wrt.shape, const),
            pl.BlockSpec(brt.shape, const),
            pl.BlockSpec(utri.shape, const),
        ],
        out_specs=[
            pl.BlockSpec((tm, D_MODEL), lambda i: (i, 0)),
            pl.BlockSpec((tm * NSLAB, LANES), lambda i: (i, 0)),
            pl.BlockSpec((TOP_K, tm), lambda i: (0, i)),
            pl.BlockSpec((TOP_K, tm), lambda i: (0, i)),
            pl.BlockSpec((TOP_K, tm), lambda i: (0, i)),
            pl.BlockSpec((N_EXPERTS, LANES), const),
        ],
        out_shape=[
            jax.ShapeDtypeStruct((ttot, D_MODEL), F32),
            jax.ShapeDtypeStruct((ttot * NSLAB, LANES), F32),
            jax.ShapeDtypeStruct((TOP_K, ttot), jnp.int32),
            jax.ShapeDtypeStruct((TOP_K, ttot), F32),
            jax.ShapeDtypeStruct((TOP_K, ttot), jnp.int32),
            jax.ShapeDtypeStruct((N_EXPERTS, LANES), F32),
        ],
        scratch_shapes=[pltpu.VMEM((N_EXPERTS, LANES), F32)],
        compiler_params=_cparams(("arbitrary",)),
        name="out_router",
    )(oa_p, ob_p, x_p, oa_s, ob_s, x_s, woa, wob, gffn, wrt, brt, utri)


def _row_copy(src_ref, s, dst_ref, d, sem):
    return pltpu.make_async_copy(_slab_rows(src_ref, s), _slab_rows(dst_ref, d), sem)


def _dispatch_kernel(padlo_ref, padhi_ref, dest_ref, h2_ref, xs_ref, zrow_ref, sem):
    i = pl.program_id(0)
    tm = h2_ref.shape[0] // NSLAB

    def issue(t, carry):
        for k in range(TOP_K):
            _row_copy(h2_ref, t, xs_ref, dest_ref[k, t], sem).start(priority=k % 2)
        return carry

    lax.fori_loop(0, tm, issue, 0, unroll=8)
    for k in range(TOP_K):
        pltpu.make_async_copy(h2_ref, xs_ref.at[pl.ds(0, tm * NSLAB), :], sem).wait()

    @pl.when(i == pl.num_programs(0) - 1)
    def _():
        zrow_ref[...] = jnp.zeros(zrow_ref.shape, F32)
        for e in range(N_EXPERTS):
            lo = padlo_ref[e]
            hi = padhi_ref[e]

            def zissue(r, carry):
                _row_copy(zrow_ref, 0, xs_ref, r, sem).start()
                return carry

            lax.fori_loop(lo, hi, zissue, 0)

            def zdrain(r, carry):
                _row_copy(zrow_ref, 0, xs_ref, 0, sem).wait()
                return carry

            lax.fori_loop(lo, hi, zdrain, 0)


def _dispatch(padlo, padhi, dest, h2, n_rows, tm_rows):
    ttot = h2.shape[0] // NSLAB
    tm = tm_rows
    grid_spec = pltpu.PrefetchScalarGridSpec(
        num_scalar_prefetch=2,
        grid=(ttot // tm,),
        in_specs=[
            pl.BlockSpec((TOP_K, tm), lambda i, lo, hi: (0, i), memory_space=pltpu.SMEM),
            pl.BlockSpec((tm * NSLAB, LANES), lambda i, lo, hi: (i, 0)),
        ],
        out_specs=pl.BlockSpec(memory_space=pl.ANY),
        scratch_shapes=[pltpu.VMEM((NSLAB, LANES), F32), pltpu.SemaphoreType.DMA(())],
    )
    return pl.pallas_call(
        _dispatch_kernel,
        grid_spec=grid_spec,
        out_shape=jax.ShapeDtypeStruct((n_rows * NSLAB, LANES), F32),
        compiler_params=_cparams(("arbitrary",)),
        name="dispatch",
    )(padlo, padhi, dest, h2)


def _expert_kernel(bexp_ref, nused_ref, x_ref, wgu_ref, bgu_ref, wd_ref, bd_ref, y_ref, wgu_s, wd_s):
    j = pl.program_id(0)
    active = j < nused_ref[0]
    new_expert = jnp.logical_or(j == 0, bexp_ref[j] != bexp_ref[jnp.maximum(j - 1, 0)])

    @pl.when(jnp.logical_and(active, new_expert))
    def _():
        rows = 128
        for r in range(0, D_MODEL, rows):
            wgu_s[r:r + rows, :] = wgu_ref[r:r + rows, :].astype(BF16)
        for r in range(0, D_FF, rows):
            wd_s[r:r + rows, :] = wd_ref[r:r + rows, :].astype(BF16)

    @pl.when(active)
    def _():
        bm = x_ref.shape[0] // NSLAB
        gu = _mm(_load_slabs(x_ref, bm), wgu_s[...]) + bgu_ref[...]
        gate = jnp.minimum(gu[:, 0:D_FF], SWIGLU_LIMIT)
        up = jnp.clip(gu[:, D_FF:], -SWIGLU_LIMIT, SWIGLU_LIMIT)
        hid = (up + 1.0) * (gate * _sigmoid(gate * SWIGLU_ALPHA))
        _store_slabs(y_ref, _mm(hid, wd_s[...]) + bd_ref[...])


def _experts(bexp, nused, xs, wgu, bgu, wd, bd):
    n_rows = xs.shape[0] // NSLAB
    bm = MOE_BM
    nb = n_rows // bm
    row_map = lambda j, be, nu: (jnp.minimum(j, nu[0] - 1), 0)
    w_map = lambda j, be, nu: (be[j], 0, 0)
    grid_spec = pltpu.PrefetchScalarGridSpec(
        num_scalar_prefetch=2,
        grid=(nb,),
        in_specs=[
            pl.BlockSpec((bm * NSLAB, LANES), row_map),
            pl.BlockSpec((None, D_MODEL, 2 * D_FF), w_map),
            pl.BlockSpec((None, 1, 2 * D_FF), w_map),
            pl.BlockSpec((None, D_FF, D_MODEL), w_map),
            pl.BlockSpec((None, 1, D_MODEL), w_map),
        ],
        out_specs=pl.BlockSpec((bm * NSLAB, LANES), row_map),
        scratch_shapes=[pltpu.VMEM((D_MODEL, 2 * D_FF), BF16), pltpu.VMEM((D_FF, D_MODEL), BF16)],
    )
    return pl.pallas_call(
        _expert_kernel,
        grid_spec=grid_spec,
        out_shape=jax.ShapeDtypeStruct((n_rows * NSLAB, LANES), F32),
        compiler_params=_cparams(("arbitrary",)),
        name="experts",
    )(bexp, nused, xs, wgu, bgu, wd, bd)


def _combine_kernel(dest_ref, destn_ref, gate_ref, x1_ref, pp_ref, ps_ref, yb_ref, gple_ref, wg_ref, wp_ref,
                    yp_ref, ys_ref, buf_ref, sems, *, ntp):
    i = pl.program_id(0)
    tm = x1_ref.shape[0]
    per_slot = TOP_K * tm
    slot = lax.rem(i, 2)

    def gather(d_ref, s):
        def issue(t, carry):
            for k in range(TOP_K):
                _row_copy(yb_ref, d_ref[k, t], buf_ref, s * per_slot + k * tm + t,
                          sems.at[s]).start(priority=k % 2)
            return carry

        lax.fori_loop(0, tm, issue, 0, unroll=8)

    @pl.when(i == 0)
    def _():
        gather(dest_ref, 0)

    @pl.when(i + 1 < pl.num_programs(0))
    def _():
        gather(destn_ref, 1 - slot)

    base = pl.multiple_of(slot * per_slot * NSLAB, NSLAB)
    pltpu.make_async_copy(yb_ref.at[pl.ds(0, per_slot * NSLAB), :],
                          buf_ref.at[pl.ds(base, per_slot * NSLAB), :], sems.at[slot]).wait()

    ng = COMB_GROUPS if tm % (COMB_GROUPS * SUBLANES) == 0 else 1
    gr = tm // ng
    groups = range(ng)

    def expert_rows(h, k):
        start = base + (k * tm + h * gr) * NSLAB
        return jnp.concatenate(
            [buf_ref[pl.ds(start + s, gr, stride=NSLAB), :] for s in range(NSLAB)], axis=1)

    gates = [gate_ref[h * gr:(h + 1) * gr, :] for h in groups]
    moe = [expert_rows(h, 0) * gates[h][:, 0:1] for h in groups]
    for k in range(1, TOP_K):
        moe = [moe[h] + expert_rows(h, k) * gates[h][:, k:k + 1] for h in groups]
    x2 = [x1_ref[h * gr:(h + 1) * gr, :] + moe[h] for h in groups]
    hn = [_rms(x2[h], gple_ref[...]) for h in groups]
    gate = [_sigmoid(_mm(hn[h], wg_ref[...])) for h in groups]

    def finish(p_ref, y_ref):
        proj = [_mm(p_ref[h * gr:(h + 1) * gr, :], wp_ref[...]) for h in groups]
        for h in groups:
            y_ref[h * gr:(h + 1) * gr, :] = x2[h] + gate[h] * proj[h]

    @pl.when(i < ntp)
    def _():
        finish(pp_ref, yp_ref)

    @pl.when(i >= ntp)
    def _():
        finish(ps_ref, ys_ref)


def _combine(dest, gates_col, x1, p_p, p_s, yb, gple, wg, wp):
    tp, ts = p_p.shape[0], p_s.shape[0]
    tm = min(COMB_TM, tp, ts)
    ntp, nts = tp // tm, ts // tm
    pmap = lambda i: (jnp.minimum(i, ntp - 1), 0)
    smap = lambda i: (jnp.maximum(i - ntp, 0), 0)
    const = lambda i: (0, 0)
    kern = functools.partial(_combine_kernel, ntp=ntp)
    return pl.pallas_call(
        kern,
        grid=(ntp + nts,),
        in_specs=[
            pl.BlockSpec((TOP_K, tm), lambda i: (0, i), memory_space=pltpu.SMEM),
            pl.BlockSpec((TOP_K, tm), lambda i: (0, jnp.minimum(i + 1, ntp + nts - 1)),
                         memory_space=pltpu.SMEM),
            pl.BlockSpec((tm, TOP_K), lambda i: (i, 0)),
            pl.BlockSpec((tm, D_MODEL), lambda i: (i, 0)),
            pl.BlockSpec((tm, PLE_DIM), pmap),
            pl.BlockSpec((tm, PLE_DIM), smap),
            pl.BlockSpec(memory_space=pl.ANY),
            pl.BlockSpec(gple.shape, const),
            pl.BlockSpec(wg.shape, const),
            pl.BlockSpec(wp.shape, const),
        ],
        out_specs=[
            pl.BlockSpec((tm, D_MODEL), pmap),
            pl.BlockSpec((tm, D_MODEL), smap),
        ],
        out_shape=[
            jax.ShapeDtypeStruct((tp, D_MODEL), F32),
            jax.ShapeDtypeStruct((ts, D_MODEL), F32),
        ],
        scratch_shapes=[pltpu.VMEM((2 * TOP_K * tm * NSLAB, LANES), F32), pltpu.SemaphoreType.DMA((2,))],
        compiler_params=_cparams(("arbitrary",)),
        name="combine",
    )(dest, dest, gates_col, x1, p_p, p_s, yb, gple, wg, wp)


def _bias_tables(rel_bias, n_q, n_k, key_offset, band):
    i = np.arange(n_q)[:, None]
    j = np.arange(n_k)[None, :]
    d_max = n_q - 1 + key_offset
    d_min = key_offset - (n_k - 1)
    n_hi = max(0, d_max - REL_MAX)
    n_lo = max(0, -REL_MAX - d_min)
    mid = rel_bias[:, max(d_min, -REL_MAX) + REL_MAX:min(d_max, REL_MAX) + REL_MAX + 1][:, ::-1]
    e = jnp.concatenate([jnp.broadcast_to(rel_bias[:, 2 * REL_MAX:], (H_B, n_hi)), mid,
                         jnp.broadcast_to(rel_bias[:, 0:1], (H_B, n_lo))], axis=1)
    period = n_q + n_k
    e = jnp.pad(e, ((0, 0), (0, 1)))
    flat = jnp.tile(e, (1, n_q + 1))[:, 0:n_q * (period + 1)]
    tab = flat.reshape(H_B, n_q, period + 1)[:, ::-1, 0:n_k]
    if band:
        qc = i // CHUNK
        sc = j // CHUNK
        ok = (sc >= qc) & (sc <= qc + N_BACK)
        tab = jnp.where(jnp.asarray(ok)[None], tab, NEG_BIG)
    return tab.reshape(H_B // 2, 2 * n_q, n_k)


def kernel(x_prompt, x_sample, state_delta, state_conv, cache_k, cache_v, p_prompt, p_sample, g_mix, w_in, conv_w, a_log, dt_bias, g_onorm, g_qnorm, g_knorm, rel_bias, w_out, g_ffn, w_router, b_router, w_gu, b_gu, w_down, b_down, g_ple, w_ple_gate, w_ple_proj):
    assert w_in.shape[0] == 1, "single-layer kernel"
    bp, tp_len, _ = x_prompt.shape
    bs, ts_len, _ = x_sample.shape
    tp, ts = bp * tp_len, bs * ts_len
    past = cache_k.shape[2]

    w = w_in[0]
    o_z = CONV_DIM + VAL_DIM_A
    o_qb = o_z + 2 * H_A
    wa = w[:, 0:o_z].astype(BF16)
    wba = jnp.pad(w[:, o_z:o_qb], ((0, 0), (0, LANES - 2 * H_A))).astype(BF16)
    wb = w[:, o_qb:].astype(BF16)
    grp = np.arange(ATT_DIM_B) // HD_B
    bd = jnp.asarray(grp[:, None] == grp[None, :], BF16)
    gq = jnp.tile(g_qnorm[0], H_B)[None, :]
    gk = jnp.tile(g_knorm[0], H_B)[None, :]
    gmix = g_mix[0][None, :]
    gon = g_onorm[0][None, :]
    convw = conv_w[0]
    woa = w_out[0, 0:VAL_DIM_A].astype(BF16)
    wob = w_out[0, VAL_DIM_A:].astype(BF16)
    gffn = g_ffn[0][None, :]
    wrt = w_router[0].T.astype(BF16)
    brt = jnp.broadcast_to(b_router[0][:, None], (N_EXPERTS, LANES))
    wgu = w_gu[0]
    bgu = b_gu[0][:, None, :]
    wd = w_down[0]
    bdn = b_down[0][:, None, :]
    gple = g_ple[0][None, :]
    wg = w_ple_gate[0].astype(BF16)
    wp = w_ple_proj[0].astype(BF16)

    xp2 = x_prompt.reshape(tp, D_MODEL)
    xs2 = x_sample.reshape(ts, D_MODEL)

    qkvz_p, ba_p, qkvb_p = _project(xp2, gmix, wa, wba, wb, bd, gq, gk)
    qkvz_s, ba_s, qkvb_s = _project(xs2, gmix, wa, wba, wb, bd, gq, gk)

    hist_pad = ((0, 0), (SUBLANES - (CONV_W - 1), 0), (0, 0))
    cbuf_p = jnp.zeros((bp, SUBLANES, CONV_DIM), F32)
    cbuf_s = jnp.pad(state_conv[0], hist_pad)
    s0_p = jnp.zeros((bp, H_A * DK_A, DV_A), F32)
    s0_s = state_delta[0].reshape(bs, H_A * DK_A, DV_A)
    oa_p, sfin_p, cnew_p = _delta_mixer(qkvz_p.reshape(bp, tp_len, -1), ba_p.reshape(bp, tp_len, LANES),
                                        cbuf_p, s0_p, convw, a_log[0], dt_bias[0], gon)
    oa_s, sfin_s, cnew_s = _delta_mixer(qkvz_s.reshape(bs, ts_len, -1), ba_s.reshape(bs, ts_len, LANES),
                                        cbuf_s, s0_s, convw, a_log[0], dt_bias[0], gon)
    sfin_p = sfin_p.reshape(bp, H_A, DK_A, DV_A)
    sfin_s = sfin_s.reshape(bs, H_A, DK_A, DV_A)

    qkvb_p3 = qkvb_p.reshape(bp, tp_len, 3 * ATT_DIM_B)
    qkvb_s3 = qkvb_s.reshape(bs, ts_len, 3 * ATT_DIM_B)
    bias_p = _bias_tables(rel_bias[0], ATT_G * CHUNK, (N_BACK + ATT_G) * CHUNK, BAND_PAST, True)
    ob_p = _attend_prompt(qkvb_p3, bias_p)
    bias_s = _bias_tables(rel_bias[0], ts_len, past + ts_len, past, False)
    ob_s = _attend_sample(qkvb_s3, cache_k[0].reshape(bs, past, ATT_DIM_B),
                          cache_v[0].reshape(bs, past, ATT_DIM_B),
                          bias_s[:, :, 0:past], bias_s[:, :, past:])

    x1, h2, idx, gates, rank, cnt = _out_router(
        oa_p.reshape(tp, VAL_DIM_A), ob_p.reshape(tp, ATT_DIM_B), xp2,
        oa_s.reshape(ts, VAL_DIM_A), ob_s.reshape(ts, ATT_DIM_B), xs2,
        woa, wob, gffn, wrt, brt)
    ttot = tp + ts
    bm = MOE_BM
    counts = cnt[:, 0].astype(jnp.int32)
    padded = (counts + bm - 1) // bm * bm
    pend = jnp.cumsum(padded)
    pstart = pend - padded
    eids = jnp.arange(N_EXPERTS, dtype=jnp.int32)
    start_of = jnp.sum(jnp.where(idx[None] == eids[:, None, None], pstart[:, None, None], 0), axis=0)
    dest = start_of + rank
    nb = -(-(ttot * TOP_K) // bm) + N_EXPERTS
    nused = (pend[-1] // bm).astype(jnp.int32)[None]
    first = jnp.minimum(jnp.arange(nb, dtype=jnp.int32), nused[0] - 1) * bm
    bexp = jnp.minimum(jnp.sum((pend[None, :] <= first[:, None]).astype(jnp.int32), axis=1),
                       N_EXPERTS - 1)

    xs_rows = _dispatch(pstart + counts, pend, dest, h2, nb * bm, min(COMB_TM, tp, ts))
    yb = _experts(bexp, nused, xs_rows, wgu, bgu, wd, bdn)
    y_p, y_s = _combine(dest, gates.T, x1, p_prompt[0].reshape(tp, PLE_DIM),
                        p_sample[0].reshape(ts, PLE_DIM), yb, gple, wg, wp)

    keep = min(BAND_PAST, tp_len)
    k_p = qkvb_p3[:, tp_len - keep:, ATT_DIM_B:2 * ATT_DIM_B].reshape(1, bp, keep, H_B, HD_B)
    v_p = qkvb_p3[:, tp_len - keep:, 2 * ATT_DIM_B:].reshape(1, bp, keep, H_B, HD_B)
    k_s = qkvb_s3[:, :, ATT_DIM_B:2 * ATT_DIM_B].reshape(1, bs, ts_len, H_B, HD_B)
    v_s = qkvb_s3[:, :, 2 * ATT_DIM_B:].reshape(1, bs, ts_len, H_B, HD_B)
    nconv = CONV_W - 1
    return (y_p.reshape(bp, tp_len, D_MODEL), y_s.reshape(bs, ts_len, D_MODEL),
            sfin_p[None], cnew_p[None, :, SUBLANES - nconv:, :], k_p, v_p,
            sfin_s[None], cnew_s[None, :, SUBLANES - nconv:, :], k_s, v_s)
```

```python
import functools

import jax
import jax.numpy as jnp
import numpy as np
from jax import lax
from jax.experimental import pallas as pl
from jax.experimental.pallas import tpu as pltpu

F32 = jnp.float32
BF16 = jnp.bfloat16

D_MODEL = 1024
CHUNK = 64
H_A = 4
DK_A = 128
DV_A = 128
CONV_W = 4
KEY_DIM_A = H_A * DK_A
VAL_DIM_A = H_A * DV_A
CONV_DIM = 2 * KEY_DIM_A + VAL_DIM_A
H_B = 8
HD_B = 64
ATT_DIM_B = H_B * HD_B
N_BACK = 8
BAND_PAST = N_BACK * CHUNK
REL_MAX = 256
N_EXPERTS = 32
TOP_K = 4
D_FF = 1024
SWIGLU_LIMIT = 7.0
SWIGLU_ALPHA = 1.702
PLE_DIM = 256
RMS_EPS = 1e-6
L2_EPS = 1e-6
NEG_BIG = -1e30

LANES = 128
SUBLANES = 8
VMEM_LIMIT = 56 * 1024 * 1024

PROJ_TM = 512
ATT_G = 2
ATT_PAIR_GROUPS = ((0, 1), (2, 3))
DELTA_NCH = 4
SOLVE_SPLIT_STAGES = 2
MOE_BM = 512
COMB_TM = 512
COMB_GROUPS = 4


def _mm(a, b):
    return jnp.dot(a.astype(BF16), b.astype(BF16), preferred_element_type=F32)


def _mm_nt(a, b):
    return lax.dot_general(a.astype(BF16), b.astype(BF16), (((1,), (1,)), ((), ())),
                           preferred_element_type=F32)


def _sigmoid(x):
    return 1.0 / (1.0 + jnp.exp(-x))


def _softplus(x):
    return jnp.maximum(x, 0.0) + jnp.log(1.0 + jnp.exp(-jnp.abs(x)))


def _rms(x, g):
    ms = jnp.mean(x * x, axis=-1, keepdims=True)
    return x * lax.rsqrt(ms + RMS_EPS) * g


NSLAB = D_MODEL // LANES


def _store_slabs(ref, val):
    n = val.shape[0]
    for s in range(NSLAB):
        ref[pl.ds(s, n, stride=NSLAB), :] = val[:, s * LANES:(s + 1) * LANES]


def _load_slabs(ref, n):
    return jnp.concatenate([ref[pl.ds(s, n, stride=NSLAB), :] for s in range(NSLAB)], axis=1)


def _slab_rows(ref, r):
    return ref.at[pl.ds(pl.multiple_of(r * NSLAB, NSLAB), NSLAB), :]


def _cparams(sem):
    return pltpu.CompilerParams(dimension_semantics=sem, vmem_limit_bytes=VMEM_LIMIT)


def _proj_kernel(x_ref, gmix_ref, wa_ref, wba_ref, wb_ref, bd_ref, gq_ref, gk_ref,
                 qkvz_ref, ba_ref, qkvb_ref):
    h = _rms(x_ref[...], gmix_ref[...]).astype(BF16)
    qkvz_ref[...] = jnp.dot(h, wa_ref[...], preferred_element_type=F32)
    ba_ref[...] = jnp.dot(h, wba_ref[...], preferred_element_type=F32)
    pb = jnp.dot(h, wb_ref[...], preferred_element_type=F32)
    bd = bd_ref[...]

    def head_norm(q, g):
        ss = jnp.dot((q * q).astype(BF16), bd, preferred_element_type=F32)
        return q * lax.rsqrt(ss * (1.0 / HD_B) + RMS_EPS) * g

    qkvb_ref[:, 0:ATT_DIM_B] = head_norm(pb[:, 0:ATT_DIM_B], gq_ref[...])
    qkvb_ref[:, ATT_DIM_B:2 * ATT_DIM_B] = head_norm(pb[:, ATT_DIM_B:2 * ATT_DIM_B], gk_ref[...])
    qkvb_ref[:, 2 * ATT_DIM_B:] = pb[:, 2 * ATT_DIM_B:]


def _project(x2d, gmix, wa, wba, wb, bd, gq, gk):
    t = x2d.shape[0]
    tm = min(PROJ_TM, t)
    const = lambda i: (0, 0)
    return pl.pallas_call(
        _proj_kernel,
        grid=(t // tm,),
        in_specs=[
            pl.BlockSpec((tm, D_MODEL), lambda i: (i, 0)),
            pl.BlockSpec(gmix.shape, const),
            pl.BlockSpec(wa.shape, const),
            pl.BlockSpec(wba.shape, const),
            pl.BlockSpec(wb.shape, const),
            pl.BlockSpec(bd.shape, const),
            pl.BlockSpec(gq.shape, const),
            pl.BlockSpec(gk.shape, const),
        ],
        out_specs=[
            pl.BlockSpec((tm, wa.shape[1]), lambda i: (i, 0)),
            pl.BlockSpec((tm, LANES), lambda i: (i, 0)),
            pl.BlockSpec((tm, wb.shape[1]), lambda i: (i, 0)),
        ],
        out_shape=[
            jax.ShapeDtypeStruct((t, wa.shape[1]), F32),
            jax.ShapeDtypeStruct((t, LANES), F32),
            jax.ShapeDtypeStruct((t, wb.shape[1]), F32),
        ],
        compiler_params=_cparams(("arbitrary",)),
        name="proj",
    )(x2d, gmix, wa, wba, wb, bd, gq, gk)


def _delta_kernel(qkvz_ref, ba_ref, cbuf_ref, s0_ref, convw_ref, nega_ref, dtb_ref,
                  gon_ref, lbd_ref,
                  o_ref, sfin_ref, cnew_ref, xh_ref, s_ref, *, c, nch):
    step = pl.program_id(1)
    tile = c * nch
    hist = SUBLANES

    @pl.when(step == 0)
    def _():
        xh_ref[0:hist, :] = cbuf_ref[...]
        s_ref[...] = s0_ref[...]

    xh_ref[hist:hist + tile, :] = qkvz_ref[:, 0:CONV_DIM]
    w = convw_ref[...]
    conv = (xh_ref[hist - 3:hist - 3 + tile, :] * w[0:1, :]
            + xh_ref[hist - 2:hist - 2 + tile, :] * w[1:2, :]
            + xh_ref[hist - 1:hist - 1 + tile, :] * w[2:3, :]
            + xh_ref[hist:hist + tile, :] * w[3:4, :])
    conv = conv * _sigmoid(conv)
    last_rows = xh_ref[tile:tile + hist, :]
    cnew_ref[...] = last_rows
    xh_ref[0:hist, :] = last_rows

    ba = ba_ref[...]
    z_all = qkvz_ref[:, CONV_DIM:CONV_DIM + VAL_DIM_A]

    rs = H_A * c
    sk = H_A * DK_A
    row = lax.broadcasted_iota(jnp.int32, (rs, rs), 0)
    col = lax.broadcasted_iota(jnp.int32, (rs, rs), 1)
    same = (row // c) == (col // c)
    causal = jnp.logical_and(same, row >= col)
    strict = jnp.logical_and(same, row > col)
    wrow = lax.broadcasted_iota(jnp.int32, (rs, sk), 0)
    wcol = lax.broadcasted_iota(jnp.int32, (rs, sk), 1)
    head_block = (wrow // c) == (wcol // DK_A)
    lbd = lbd_ref[...]
    nsq = int(np.log2(c))

    def stack(fn):
        return jnp.concatenate([fn(h) for h in range(H_A)], axis=0)

    def spread(m):
        return jnp.where(head_block, jnp.concatenate([m] * H_A, axis=1), 0.0)

    def mm_split3(a_bf16, x):
        hi = x.astype(BF16)
        r1 = x - hi.astype(F32)
        mid = r1.astype(BF16)
        lo = (r1 - mid.astype(F32)).astype(BF16)
        return (jnp.dot(a_bf16, hi, preferred_element_type=F32)
                + jnp.dot(a_bf16, mid, preferred_element_type=F32)
                + jnp.dot(a_bf16, lo, preferred_element_type=F32))

    def l2n(m):
        return m * lax.rsqrt(jnp.sum(m * m, axis=-1, keepdims=True) + L2_EPS)

    def last_row(m, h, shape):
        return jnp.broadcast_to(m[h * c + c - 1:h * c + c, :], shape)

    cs = range(nch)
    q_st = [l2n(stack(lambda h: conv[ci * c:ci * c + c, h * DK_A:(h + 1) * DK_A])) * (DK_A ** -0.5)
            for ci in cs]
    k_st = [l2n(stack(lambda h: conv[ci * c:ci * c + c, KEY_DIM_A + h * DK_A:KEY_DIM_A + (h + 1) * DK_A]))
            for ci in cs]
    v_st = [stack(lambda h: conv[ci * c:ci * c + c, 2 * KEY_DIM_A + h * DV_A:2 * KEY_DIM_A + (h + 1) * DV_A])
            for ci in cs]
    beta = [_sigmoid(stack(lambda h: jnp.broadcast_to(ba[ci * c:ci * c + c, h:h + 1], (c, LANES))))
            for ci in cs]
    g = [nega_ref[...] * _softplus(
        stack(lambda h: jnp.broadcast_to(ba[ci * c:ci * c + c, H_A + h:H_A + h + 1], (c, LANES)))
        + dtb_ref[...]) for ci in cs]
    gc = [mm_split3(lbd, gi) for gi in g]
    kb = [k * b for k, b in zip(k_st, beta)]
    kk = [_mm_nt(a, b) for a, b in zip(kb, k_st)]
    qk = [_mm_nt(a, b) for a, b in zip(q_st, k_st)]
    decay = [jnp.exp(jnp.where(causal,
                               jnp.concatenate([m] * (rs // LANES), axis=1)
                               - jnp.broadcast_to(m.T[0:1, :], (rs, rs)),
                               -jnp.inf)) for m in gc]
    egc = [jnp.exp(m) for m in gc]
    ps = [-jnp.where(strict, a * d, 0.0) for a, d in zip(kk, decay)]
    qk = [a * d for a, d in zip(qk, decay)]
    xs = [jnp.concatenate([v * b, k * e], axis=-1) for v, b, k, e in zip(v_st, beta, kb, egc)]
    def mm_split_rhs(a, b):
        bh = b.astype(BF16)
        bl = (b - bh.astype(F32)).astype(BF16)
        ah = a.astype(BF16)
        return (jnp.dot(ah, bh, preferred_element_type=F32)
                + jnp.dot(ah, bl, preferred_element_type=F32))

    for s in range(nsq):
        apply = mm_split_rhs if s < SOLVE_SPLIT_STAGES else _mm
        xs = [x + apply(p, x) for x, p in zip(xs, ps)]
        if s + 1 < nsq:
            ps = [_mm(p, p) for p in ps]
    qbd = [spread(q * e) for q, e in zip(q_st, egc)]
    kbd = [spread(k * jnp.exp(stack(lambda h: last_row(m, h, (c, LANES))) - m)) for k, m in zip(k_st, gc)]
    egl = [stack(lambda h: last_row(e, h, (DK_A, DV_A))) for e in egc]

    for ci in range(nch):
        r0 = ci * c
        eglast = egl[ci]
        u0 = xs[ci][:, 0:DV_A]
        wbd = spread(xs[ci][:, DV_A:])
        s_all = s_ref[...]
        u = u0 - _mm(wbd, s_all)
        o = _mm(qbd[ci], s_all) + _mm(qk[ci], u)
        s_ref[...] = s_all * eglast + lax.dot_general(
            kbd[ci].astype(BF16), u.astype(BF16), (((0,), (0,)), ((), ())), preferred_element_type=F32)
        on = _rms(o, gon_ref[...])
        for h in range(H_A):
            z = z_all[r0:r0 + c, h * DV_A:(h + 1) * DV_A]
            o_ref[r0:r0 + c, h * DV_A:(h + 1) * DV_A] = on[h * c:(h + 1) * c, :] * (z * _sigmoid(z))

    sfin_ref[...] = s_ref[...]


def _delta_mixer(qkvz, ba, cbuf, s0, convw, a_log, dt_bias, gon):
    b, t, _ = qkvz.shape
    c = min(CHUNK, t)
    nch = min(DELTA_NCH, t // c)
    tile = c * nch
    rs = H_A * c
    sk = H_A * DK_A
    assert rs % LANES == 0
    r = np.arange(rs)
    lbd = jnp.asarray((r[:, None] // c == r[None, :] // c) & (r[:, None] >= r[None, :]), BF16)
    nega = jnp.broadcast_to(jnp.repeat(-jnp.exp(a_log), c)[:, None], (rs, LANES))
    dtb = jnp.broadcast_to(jnp.repeat(dt_bias, c)[:, None], (rs, LANES))
    const2 = lambda i, j: (0, 0)
    kern = functools.partial(_delta_kernel, c=c, nch=nch)
    return pl.pallas_call(
        kern,
        grid=(b, t // tile),
        in_specs=[
            pl.BlockSpec((None, tile, qkvz.shape[2]), lambda i, j: (i, j, 0)),
            pl.BlockSpec((None, tile, LANES), lambda i, j: (i, j, 0)),
            pl.BlockSpec((None, SUBLANES, CONV_DIM), lambda i, j: (i, 0, 0)),
            pl.BlockSpec((None, sk, DV_A), lambda i, j: (i, 0, 0)),
            pl.BlockSpec(convw.shape, const2),
            pl.BlockSpec(nega.shape, const2),
            pl.BlockSpec(dtb.shape, const2),
            pl.BlockSpec(gon.shape, const2),
            pl.BlockSpec(lbd.shape, const2),
        ],
        out_specs=[
            pl.BlockSpec((None, tile, VAL_DIM_A), lambda i, j: (i, j, 0)),
            pl.BlockSpec((None, sk, DV_A), lambda i, j: (i, 0, 0)),
            pl.BlockSpec((None, SUBLANES, CONV_DIM), lambda i, j: (i, 0, 0)),
        ],
        out_shape=[
            jax.ShapeDtypeStruct((b, t, VAL_DIM_A), F32),
            jax.ShapeDtypeStruct((b, sk, DV_A), F32),
            jax.ShapeDtypeStruct((b, SUBLANES, CONV_DIM), F32),
        ],
        scratch_shapes=[
            pltpu.VMEM((SUBLANES + tile, CONV_DIM), F32),
            pltpu.VMEM((sk, DV_A), F32),
        ],
        compiler_params=_cparams(("arbitrary", "arbitrary")),
        name="delta",
    )(qkvz, ba, cbuf, s0, convw, nega, dtb, gon, lbd)


def _pair_queries(qp):
    lane = lax.broadcasted_iota(jnp.int32, qp.shape, 1)
    q_even = jnp.where(lane < HD_B, qp, 0.0)
    q_odd = jnp.where(lane >= HD_B, qp, 0.0)
    return jnp.concatenate([q_even, q_odd], axis=0).astype(BF16)


def _unpair(o, r):
    lane = lax.broadcasted_iota(jnp.int32, (r, LANES), 1)
    return jnp.where(lane < HD_B, o[0:r, :], o[r:2 * r, :])


def _attn_prompt_kernel(q_ref, k_ref, v_ref, bias_ref, o_ref, kwin_ref, vwin_ref, *, g):
    cstep = pl.program_id(1)
    rows = g * CHUNK
    win = (N_BACK + g) * CHUNK

    @pl.when(cstep == 0)
    def _():
        kwin_ref[...] = jnp.zeros(kwin_ref.shape, BF16)
        vwin_ref[...] = jnp.zeros(vwin_ref.shape, BF16)

    for i in range(N_BACK // g):
        kwin_ref[i * rows:(i + 1) * rows, :] = kwin_ref[(i + 1) * rows:(i + 2) * rows, :]
        vwin_ref[i * rows:(i + 1) * rows, :] = vwin_ref[(i + 1) * rows:(i + 2) * rows, :]
    kwin_ref[win - rows:win, :] = k_ref[...].astype(BF16)
    vwin_ref[win - rows:win, :] = v_ref[...].astype(BF16)

    slot_chunk = lax.broadcasted_iota(jnp.int32, (2 * rows, win), 1) // CHUNK
    in_seq = slot_chunk >= N_BACK - cstep * g
    for pairs in ATT_PAIR_GROUPS:
        sls = [slice(p * LANES, (p + 1) * LANES) for p in pairs]
        s = [_mm_nt(_pair_queries(q_ref[:, sl] * (HD_B ** -0.5)), kwin_ref[:, sl]) for sl in sls]
        s = [jnp.where(in_seq, sp + bias_ref[p], NEG_BIG) for p, sp in zip(pairs, s)]
        m = [jnp.max(sp, axis=-1, keepdims=True) for sp in s]
        e = [jnp.exp(sp - mp) for sp, mp in zip(s, m)]
        l = [jnp.sum(ep, axis=-1, keepdims=True) for ep in e]
        o = [jnp.dot(ep.astype(BF16), vwin_ref[:, sl], preferred_element_type=F32) for ep, sl in zip(e, sls)]
        for sl, op, lp in zip(sls, o, l):
            o_ref[:, sl] = _unpair(op / lp, rows)


def _attend_prompt(qkvb, bias):
    b, t, _ = qkvb.shape
    g = ATT_G
    rows = g * CHUNK
    win = (N_BACK + g) * CHUNK
    kern = functools.partial(_attn_prompt_kernel, g=g)
    return pl.pallas_call(
        kern,
        grid=(b, t // rows),
        in_specs=[
            pl.BlockSpec((None, rows, ATT_DIM_B), lambda i, j: (i, j, 0)),
            pl.BlockSpec((None, rows, ATT_DIM_B), lambda i, j: (i, j, 1)),
            pl.BlockSpec((None, rows, ATT_DIM_B), lambda i, j: (i, j, 2)),
            pl.BlockSpec(bias.shape, lambda i, j: (0, 0, 0)),
        ],
        out_specs=pl.BlockSpec((None, rows, ATT_DIM_B), lambda i, j: (i, j, 0)),
        out_shape=jax.ShapeDtypeStruct((b, t, ATT_DIM_B), F32),
        scratch_shapes=[pltpu.VMEM((win, ATT_DIM_B), BF16), pltpu.VMEM((win, ATT_DIM_B), BF16)],
        compiler_params=_cparams(("arbitrary", "arbitrary")),
        name="attn_prompt",
    )(qkvb, qkvb, qkvb, bias)


def _attn_sample_kernel(q_ref, kn_ref, vn_ref, kc_ref, vc_ref, biasc_ref, biasn_ref, o_ref):
    ds = q_ref.shape[0]
    for p in range(H_B // 2):
        sl = slice(p * LANES, (p + 1) * LANES)
        q2 = _pair_queries(q_ref[:, sl] * (HD_B ** -0.5))
        s1 = _mm_nt(q2, kc_ref[:, sl]) + biasc_ref[p]
        s2 = _mm_nt(q2, kn_ref[:, sl]) + biasn_ref[p]
        m = jnp.maximum(jnp.max(s1, axis=-1, keepdims=True), jnp.max(s2, axis=-1, keepdims=True))
        e1 = jnp.exp(s1 - m)
        e2 = jnp.exp(s2 - m)
        l = jnp.sum(e1, axis=-1, keepdims=True) + jnp.sum(e2, axis=-1, keepdims=True)
        o = (_mm(e1, vc_ref[:, sl]) + _mm(e2, vn_ref[:, sl])) / l
        o_ref[:, sl] = _unpair(o, ds)


def _attend_sample(qkvb, cache_k, cache_v, biasc, biasn):
    b, ds, _ = qkvb.shape
    past = cache_k.shape[1]
    return pl.pallas_call(
        _attn_sample_kernel,
        grid=(b,),
        in_specs=[
            pl.BlockSpec((None, ds, ATT_DIM_B), lambda i: (i, 0, 0)),
            pl.BlockSpec((None, ds, ATT_DIM_B), lambda i: (i, 0, 1)),
            pl.BlockSpec((None, ds, ATT_DIM_B), lambda i: (i, 0, 2)),
            pl.BlockSpec((None, past, ATT_DIM_B), lambda i: (i, 0, 0)),
            pl.BlockSpec((None, past, ATT_DIM_B), lambda i: (i, 0, 0)),
            pl.BlockSpec(biasc.shape, lambda i: (0, 0, 0)),
            pl.BlockSpec(biasn.shape, lambda i: (0, 0, 0)),
        ],
        out_specs=pl.BlockSpec((None, ds, ATT_DIM_B), lambda i: (i, 0, 0)),
        out_shape=jax.ShapeDtypeStruct((b, ds, ATT_DIM_B), F32),
        compiler_params=_cparams(("arbitrary",)),
        name="attn_sample",
    )(qkvb, qkvb, qkvb, cache_k, cache_v, biasc, biasn)


def _router_kernel(oap_ref, obp_ref, xp_ref, oas_ref, obs_ref, xs_ref,
                   woa_ref, wob_ref, gffn_ref, wrt_ref, brt_ref, utri_ref,
                   x1_ref, h2_ref, idx_ref, gate_ref, rank_ref, cnt_ref,
                   carry_s, *, ntp):
    i = pl.program_id(0)
    tm = x1_ref.shape[0]

    @pl.when(i == 0)
    def _():
        carry_s[...] = jnp.zeros(carry_s.shape, F32)

    def mix(oa_ref, ob_ref, x_ref):
        x1_ref[...] = x_ref[...] + _mm(oa_ref[...], woa_ref[...]) + _mm(ob_ref[...], wob_ref[...])

    @pl.when(i < ntp)
    def _():
        mix(oap_ref, obp_ref, xp_ref)

    @pl.when(i >= ntp)
    def _():
        mix(oas_ref, obs_ref, xs_ref)

    x1 = x1_ref[...]
    h2 = _rms(x1, gffn_ref[...])
    _store_slabs(h2_ref, h2)
    logits = _mm_nt(wrt_ref[...], h2) + brt_ref[:, 0:1]

    eidx = lax.broadcasted_iota(jnp.int32, (N_EXPERTS, tm), 0).astype(F32)
    vals = logits
    memf = jnp.zeros((N_EXPERTS, tm), F32)
    tops, sels = [], []
    for k in range(TOP_K):
        m = jnp.max(vals, axis=0, keepdims=True)
        ix = jnp.min(jnp.where(vals == m, eidx, float(N_EXPERTS)), axis=0, keepdims=True)
        sel = eidx == ix
        tops.append(m)
        sels.append(sel)
        idx_ref[k:k + 1, :] = ix.astype(jnp.int32)
        vals = jnp.where(sel, -jnp.inf, vals)
        memf = memf + jnp.where(sel, 1.0, 0.0)
    es = [jnp.exp(v - tops[0]) for v in tops]
    denom = es[0] + es[1] + es[2] + es[3]
    for k in range(TOP_K):
        gate_ref[k:k + 1, :] = es[k] / denom

    carry = carry_s[:, 0:1]
    excl = _mm(memf, utri_ref[...]) + carry
    for k in range(TOP_K):
        r = jnp.sum(jnp.where(sels[k], excl, 0.0), axis=0, keepdims=True)
        rank_ref[k:k + 1, :] = r.astype(jnp.int32)
    new_carry = carry + jnp.sum(memf, axis=1, keepdims=True)
    carry_s[...] = jnp.broadcast_to(new_carry, carry_s.shape)
    cnt_ref[...] = jnp.broadcast_to(new_carry, cnt_ref.shape)


def _out_router(oa_p, ob_p, x_p, oa_s, ob_s, x_s, woa, wob, gffn, wrt, brt):
    tp, ts = x_p.shape[0], x_s.shape[0]
    tm = min(PROJ_TM, ts, tp)
    ntp, nts = tp // tm, ts // tm
    ttot = tp + ts
    utri = jnp.triu(jnp.ones((tm, tm), BF16), k=1)
    pmap = lambda i: (jnp.minimum(i, ntp - 1), 0)
    smap = lambda i: (jnp.maximum(i - ntp, 0), 0)
    const = lambda i: (0, 0)
    kern = functools.partial(_router_kernel, ntp=ntp)
    return pl.pallas_call(
        kern,
        grid=(ntp + nts,),
        in_specs=[
            pl.BlockSpec((tm, VAL_DIM_A), pmap),
            pl.BlockSpec((tm, ATT_DIM_B), pmap),
            pl.BlockSpec((tm, D_MODEL), pmap),
            pl.BlockSpec((tm, VAL_DIM_A), smap),
            pl.BlockSpec((tm, ATT_DIM_B), smap),
            pl.BlockSpec((tm, D_MODEL), smap),
            pl.BlockSpec(woa.shape, const),
            pl.BlockSpec(wob.shape, const),
            pl.BlockSpec(gffn.shape, const),
            pl.BlockSpec(wrt.shape, const),
            pl.BlockSpec(brt.shape, const),
            pl.BlockSpec(utri.shape, const),
        ],
        out_specs=[
            pl.BlockSpec((tm, D_MODEL), lambda i: (i, 0)),
            pl.BlockSpec((tm * NSLAB, LANES), lambda i: (i, 0)),
            pl.BlockSpec((TOP_K, tm), lambda i: (0, i)),
            pl.BlockSpec((TOP_K, tm), lambda i: (0, i)),
            pl.BlockSpec((TOP_K, tm), lambda i: (0, i)),
            pl.BlockSpec((N_EXPERTS, LANES), const),
        ],
        out_shape=[
            jax.ShapeDtypeStruct((ttot, D_MODEL), F32),
            jax.ShapeDtypeStruct((ttot * NSLAB, LANES), F32),
            jax.ShapeDtypeStruct((TOP_K, ttot), jnp.int32),
            jax.ShapeDtypeStruct((TOP_K, ttot), F32),
            jax.ShapeDtypeStruct((TOP_K, ttot), jnp.int32),
            jax.ShapeDtypeStruct((N_EXPERTS, LANES), F32),
        ],
        scratch_shapes=[pltpu.VMEM((N_EXPERTS, LANES), F32)],
        compiler_params=_cparams(("arbitrary",)),
        name="out_router",
    )(oa_p, ob_p, x_p, oa_s, ob_s, x_s, woa, wob, gffn, wrt, brt, utri)


def _row_copy(src_ref, s, dst_ref, d, sem):
    return pltpu.make_async_copy(_slab_rows(src_ref, s), _slab_rows(dst_ref, d), sem)


def _dispatch_kernel(padlo_ref, padhi_ref, dest_ref, h2_ref, xs_ref, zrow_ref, sem):
    i = pl.program_id(0)
    tm = h2_ref.shape[0] // NSLAB

    def issue(t, carry):
        for k in range(TOP_K):
            _row_copy(h2_ref, t, xs_ref, dest_ref[k, t], sem).start(priority=k % 2)
        return carry

    lax.fori_loop(0, tm, issue, 0, unroll=8)
    for k in range(TOP_K):
        pltpu.make_async_copy(h2_ref, xs_ref.at[pl.ds(0, tm * NSLAB), :], sem).wait()

    @pl.when(i == pl.num_programs(0) - 1)
    def _():
        zrow_ref[...] = jnp.zeros(zrow_ref.shape, F32)
        for e in range(N_EXPERTS):
            lo = padlo_ref[e]
            hi = padhi_ref[e]

            def zissue(r, carry):
                _row_copy(zrow_ref, 0, xs_ref, r, sem).start()
                return carry

            lax.fori_loop(lo, hi, zissue, 0)

            def zdrain(r, carry):
                _row_copy(zrow_ref, 0, xs_ref, 0, sem).wait()
                return carry

            lax.fori_loop(lo, hi, zdrain, 0)


def _dispatch(padlo, padhi, dest, h2, n_rows, tm_rows):
    ttot = h2.shape[0] // NSLAB
    tm = tm_rows
    grid_spec = pltpu.PrefetchScalarGridSpec(
        num_scalar_prefetch=2,
        grid=(ttot // tm,),
        in_specs=[
            pl.BlockSpec((TOP_K, tm), lambda i, lo, hi: (0, i), memory_space=pltpu.SMEM),
            pl.BlockSpec((tm * NSLAB, LANES), lambda i, lo, hi: (i, 0)),
        ],
        out_specs=pl.BlockSpec(memory_space=pl.ANY),
        scratch_shapes=[pltpu.VMEM((NSLAB, LANES), F32), pltpu.SemaphoreType.DMA(())],
    )
    return pl.pallas_call(
        _dispatch_kernel,
        grid_spec=grid_spec,
        out_shape=jax.ShapeDtypeStruct((n_rows * NSLAB, LANES), F32),
        compiler_params=_cparams(("arbitrary",)),
        name="dispatch",
    )(padlo, padhi, dest, h2)


def _expert_kernel(bexp_ref, nused_ref, x_ref, wgu_ref, bgu_ref, wd_ref, bd_ref, y_ref, wgu_s, wd_s):
    j = pl.program_id(0)
    active = j < nused_ref[0]
    new_expert = jnp.logical_or(j == 0, bexp_ref[j] != bexp_ref[jnp.maximum(j - 1, 0)])

    @pl.when(jnp.logical_and(active, new_expert))
    def _():
        rows = 128
        for r in range(0, D_MODEL, rows):
            wgu_s[r:r + rows, :] = wgu_ref[r:r + rows, :].astype(BF16)
        for r in range(0, D_FF, rows):
            wd_s[r:r + rows, :] = wd_ref[r:r + rows, :].astype(BF16)

    @pl.when(active)
    def _():
        bm = x_ref.shape[0] // NSLAB
        gu = _mm(_load_slabs(x_ref, bm), wgu_s[...]) + bgu_ref[...]
        gate = jnp.minimum(gu[:, 0:D_FF], SWIGLU_LIMIT)
        up = jnp.clip(gu[:, D_FF:], -SWIGLU_LIMIT, SWIGLU_LIMIT)
        hid = (up + 1.0) * (gate * _sigmoid(gate * SWIGLU_ALPHA))
        _store_slabs(y_ref, _mm(hid, wd_s[...]) + bd_ref[...])


def _experts(bexp, nused, xs, wgu, bgu, wd, bd):
    n_rows = xs.shape[0] // NSLAB
    bm = MOE_BM
    nb = n_rows // bm
    row_map = lambda j, be, nu: (jnp.minimum(j, nu[0] - 1), 0)
    w_map = lambda j, be, nu: (be[j], 0, 0)
    grid_spec = pltpu.PrefetchScalarGridSpec(
        num_scalar_prefetch=2,
        grid=(nb,),
        in_specs=[
            pl.BlockSpec((bm * NSLAB, LANES), row_map),
            pl.BlockSpec((None, D_MODEL, 2 * D_FF), w_map),
            pl.BlockSpec((None, 1, 2 * D_FF), w_map),
            pl.BlockSpec((None, D_FF, D_MODEL), w_map),
            pl.BlockSpec((None, 1, D_MODEL), w_map),
        ],
        out_specs=pl.BlockSpec((bm * NSLAB, LANES), row_map),
        scratch_shapes=[pltpu.VMEM((D_MODEL, 2 * D_FF), BF16), pltpu.VMEM((D_FF, D_MODEL), BF16)],
    )
    return pl.pallas_call(
        _expert_kernel,
        grid_spec=grid_spec,
        out_shape=jax.ShapeDtypeStruct((n_rows * NSLAB, LANES), F32),
        compiler_params=_cparams(("arbitrary",)),
        name="experts",
    )(bexp, nused, xs, wgu, bgu, wd, bd)


def _combine_kernel(dest_ref, destn_ref, gate_ref, x1_ref, pp_ref, ps_ref, yb_ref, gple_ref, wg_ref, wp_ref,
                    yp_ref, ys_ref, buf_ref, sems, *, ntp):
    i = pl.program_id(0)
    tm = x1_ref.shape[0]
    per_slot = TOP_K * tm
    slot = lax.rem(i, 2)

    def gather(d_ref, s):
        def issue(t, carry):
            for k in range(TOP_K):
                _row_copy(yb_ref, d_ref[k, t], buf_ref, s * per_slot + k * tm + t,
                          sems.at[s]).start(priority=k % 2)
            return carry

        lax.fori_loop(0, tm, issue, 0, unroll=8)

    @pl.when(i == 0)
    def _():
        gather(dest_ref, 0)

    @pl.when(i + 1 < pl.num_programs(0))
    def _():
        gather(destn_ref, 1 - slot)

    base = pl.multiple_of(slot * per_slot * NSLAB, NSLAB)
    pltpu.make_async_copy(yb_ref.at[pl.ds(0, per_slot * NSLAB), :],
                          buf_ref.at[pl.ds(base, per_slot * NSLAB), :], sems.at[slot]).wait()

    ng = COMB_GROUPS if tm % (COMB_GROUPS * SUBLANES) == 0 else 1
    gr = tm // ng
    groups = range(ng)

    def expert_rows(h, k):
        start = base + (k * tm + h * gr) * NSLAB
        return jnp.concatenate(
            [buf_ref[pl.ds(start + s, gr, stride=NSLAB), :] for s in range(NSLAB)], axis=1)

    gates = [gate_ref[h * gr:(h + 1) * gr, :] for h in groups]
    moe = [expert_rows(h, 0) * gates[h][:, 0:1] for h in groups]
    for k in range(1, TOP_K):
        moe = [moe[h] + expert_rows(h, k) * gates[h][:, k:k + 1] for h in groups]
    x2 = [x1_ref[h * gr:(h + 1) * gr, :] + moe[h] for h in groups]
    hn = [_rms(x2[h], gple_ref[...]) for h in groups]
    gate = [_sigmoid(_mm(hn[h], wg_ref[...])) for h in groups]

    def finish(p_ref, y_ref):
        proj = [_mm(p_ref[h * gr:(h + 1) * gr, :], wp_ref[...]) for h in groups]
        for h in groups:
            y_ref[h * gr:(h + 1) * gr, :] = x2[h] + gate[h] * proj[h]

    @pl.when(i < ntp)
    def _():
        finish(pp_ref, yp_ref)

    @pl.when(i >= ntp)
    def _():
        finish(ps_ref, ys_ref)


def _combine(dest, gates_col, x1, p_p, p_s, yb, gple, wg, wp):
    tp, ts = p_p.shape[0], p_s.shape[0]
    tm = min(COMB_TM, tp, ts)
    ntp, nts = tp // tm, ts // tm
    pmap = lambda i: (jnp.minimum(i, ntp - 1), 0)
    smap = lambda i: (jnp.maximum(i - ntp, 0), 0)
    const = lambda i: (0, 0)
    kern = functools.partial(_combine_kernel, ntp=ntp)
    return pl.pallas_call(
        kern,
        grid=(ntp + nts,),
        in_specs=[
            pl.BlockSpec((TOP_K, tm), lambda i: (0, i), memory_space=pltpu.SMEM),
            pl.BlockSpec((TOP_K, tm), lambda i: (0, jnp.minimum(i + 1, ntp + nts - 1)),
                         memory_space=pltpu.SMEM),
            pl.BlockSpec((tm, TOP_K), lambda i: (i, 0)),
            pl.BlockSpec((tm, D_MODEL), lambda i: (i, 0)),
            pl.BlockSpec((tm, PLE_DIM), pmap),
            pl.BlockSpec((tm, PLE_DIM), smap),
            pl.BlockSpec(memory_space=pl.ANY),
            pl.BlockSpec(gple.shape, const),
            pl.BlockSpec(wg.shape, const),
            pl.BlockSpec(wp.shape, const),
        ],
        out_specs=[
            pl.BlockSpec((tm, D_MODEL), pmap),
            pl.BlockSpec((tm, D_MODEL), smap),
        ],
        out_shape=[
            jax.ShapeDtypeStruct((tp, D_MODEL), F32),
            jax.ShapeDtypeStruct((ts, D_MODEL), F32),
        ],
        scratch_shapes=[pltpu.VMEM((2 * TOP_K * tm * NSLAB, LANES), F32), pltpu.SemaphoreType.DMA((2,))],
        compiler_params=_cparams(("arbitrary",)),
        name="combine",
    )(dest, dest, gates_col, x1, p_p, p_s, yb, gple, wg, wp)


def _bias_tables(rel_bias, n_q, n_k, key_offset, band):
    i = np.arange(n_q)[:, None]
    j = np.arange(n_k)[None, :]
    d_max = n_q - 1 + key_offset
    d_min = key_offset - (n_k - 1)
    n_hi = max(0, d_max - REL_MAX)
    n_lo = max(0, -REL_MAX - d_min)
    mid = rel_bias[:, max(d_min, -REL_MAX) + REL_MAX:min(d_max, REL_MAX) + REL_MAX + 1][:, ::-1]
    e = jnp.concatenate([jnp.broadcast_to(rel_bias[:, 2 * REL_MAX:], (H_B, n_hi)), mid,
                         jnp.broadcast_to(rel_bias[:, 0:1], (H_B, n_lo))], axis=1)
    period = n_q + n_k
    e = jnp.pad(e, ((0, 0), (0, 1)))
    flat = jnp.tile(e, (1, n_q + 1))[:, 0:n_q * (period + 1)]
    tab = flat.reshape(H_B, n_q, period + 1)[:, ::-1, 0:n_k]
    if band:
        qc = i // CHUNK
        sc = j // CHUNK
        ok = (sc >= qc) & (sc <= qc + N_BACK)
        tab = jnp.where(jnp.asarray(ok)[None], tab, NEG_BIG)
    return tab.reshape(H_B // 2, 2 * n_q, n_k)


def kernel(x_prompt, x_sample, state_delta, state_conv, cache_k, cache_v, p_prompt, p_sample, g_mix, w_in, conv_w, a_log, dt_bias, g_onorm, g_qnorm, g_knorm, rel_bias, w_out, g_ffn, w_router, b_router, w_gu, b_gu, w_down, b_down, g_ple, w_ple_gate, w_ple_proj):
    assert w_in.shape[0] == 1, "single-layer kernel"
    bp, tp_len, _ = x_prompt.shape
    bs, ts_len, _ = x_sample.shape
    tp, ts = bp * tp_len, bs * ts_len
    past = cache_k.shape[2]

    w = w_in[0]
    o_z = CONV_DIM + VAL_DIM_A
    o_qb = o_z + 2 * H_A
    wa = w[:, 0:o_z].astype(BF16)
    wba = jnp.pad(w[:, o_z:o_qb], ((0, 0), (0, LANES - 2 * H_A))).astype(BF16)
    wb = w[:, o_qb:].astype(BF16)
    grp = np.arange(ATT_DIM_B) // HD_B
    bd = jnp.asarray(grp[:, None] == grp[None, :], BF16)
    gq = jnp.tile(g_qnorm[0], H_B)[None, :]
    gk = jnp.tile(g_knorm[0], H_B)[None, :]
    gmix = g_mix[0][None, :]
    gon = g_onorm[0][None, :]
    convw = conv_w[0]
    woa = w_out[0, 0:VAL_DIM_A].astype(BF16)
    wob = w_out[0, VAL_DIM_A:].astype(BF16)
    gffn = g_ffn[0][None, :]
    wrt = w_router[0].T.astype(BF16)
    brt = jnp.broadcast_to(b_router[0][:, None], (N_EXPERTS, LANES))
    wgu = w_gu[0]
    bgu = b_gu[0][:, None, :]
    wd = w_down[0]
    bdn = b_down[0][:, None, :]
    gple = g_ple[0][None, :]
    wg = w_ple_gate[0].astype(BF16)
    wp = w_ple_proj[0].astype(BF16)

    xp2 = x_prompt.reshape(tp, D_MODEL)
    xs2 = x_sample.reshape(ts, D_MODEL)

    qkvz_p, ba_p, qkvb_p = _project(xp2, gmix, wa, wba, wb, bd, gq, gk)
    qkvz_s, ba_s, qkvb_s = _project(xs2, gmix, wa, wba, wb, bd, gq, gk)

    hist_pad = ((0, 0), (SUBLANES - (CONV_W - 1), 0), (0, 0))
    cbuf_p = jnp.zeros((bp, SUBLANES, CONV_DIM), F32)
    cbuf_s = jnp.pad(state_conv[0], hist_pad)
    s0_p = jnp.zeros((bp, H_A * DK_A, DV_A), F32)
    s0_s = state_delta[0].reshape(bs, H_A * DK_A, DV_A)
    oa_p, sfin_p, cnew_p = _delta_mixer(qkvz_p.reshape(bp, tp_len, -1), ba_p.reshape(bp, tp_len, LANES),
                                        cbuf_p, s0_p, convw, a_log[0], dt_bias[0], gon)
    oa_s, sfin_s, cnew_s = _delta_mixer(qkvz_s.reshape(bs, ts_len, -1), ba_s.reshape(bs, ts_len, LANES),
                                        cbuf_s, s0_s, convw, a_log[0], dt_bias[0], gon)
    sfin_p = sfin_p.reshape(bp, H_A, DK_A, DV_A)
    sfin_s = sfin_s.reshape(bs, H_A, DK_A, DV_A)

    qkvb_p3 = qkvb_p.reshape(bp, tp_len, 3 * ATT_DIM_B)
    qkvb_s3 = qkvb_s.reshape(bs, ts_len, 3 * ATT_DIM_B)
    bias_p = _bias_tables(rel_bias[0], ATT_G * CHUNK, (N_BACK + ATT_G) * CHUNK, BAND_PAST, True)
    ob_p = _attend_prompt(qkvb_p3, bias_p)
    bias_s = _bias_tables(rel_bias[0], ts_len, past + ts_len, past, False)
    ob_s = _attend_sample(qkvb_s3, cache_k[0].reshape(bs, past, ATT_DIM_B),
                          cache_v[0].reshape(bs, past, ATT_DIM_B),
                          bias_s[:, :, 0:past], bias_s[:, :, past:])

    x1, h2, idx, gates, rank, cnt = _out_router(
        oa_p.reshape(tp, VAL_DIM_A), ob_p.reshape(tp, ATT_DIM_B), xp2,
        oa_s.reshape(ts, VAL_DIM_A), ob_s.reshape(ts, ATT_DIM_B), xs2,
        woa, wob, gffn, wrt, brt)
    ttot = tp + ts
    bm = MOE_BM
    counts = cnt[:, 0].astype(jnp.int32)
    padded = (counts + bm - 1) // bm * bm
    pend = jnp.cumsum(padded)
    pstart = pend - padded
    eids = jnp.arange(N_EXPERTS, dtype=jnp.int32)
    start_of = jnp.sum(jnp.where(idx[None] == eids[:, None, None], pstart[:, None, None], 0), axis=0)
    dest = start_of + rank
    nb = -(-(ttot * TOP_K) // bm) + N_EXPERTS
    nused = (pend[-1] // bm).astype(jnp.int32)[None]
    first = jnp.minimum(jnp.arange(nb, dtype=jnp.int32), nused[0] - 1) * bm
    bexp = jnp.minimum(jnp.sum((pend[None, :] <= first[:, None]).astype(jnp.int32), axis=1),
                       N_EXPERTS - 1)

    xs_rows = _dispatch(pstart + counts, pend, dest, h2, nb * bm, min(COMB_TM, tp, ts))
    yb = _experts(bexp, nused, xs_rows, wgu, bgu, wd, bdn)
    y_p, y_s = _combine(dest, gates.T, x1, p_prompt[0].reshape(tp, PLE_DIM),
                        p_sample[0].reshape(ts, PLE_DIM), yb, gple, wg, wp)

    keep = min(BAND_PAST, tp_len)
    k_p = qkvb_p3[:, tp_len - keep:, ATT_DIM_B:2 * ATT_DIM_B].reshape(1, bp, keep, H_B, HD_B)
    v_p = qkvb_p3[:, tp_len - keep:, 2 * ATT_DIM_B:].reshape(1, bp, keep, H_B, HD_B)
    k_s = qkvb_s3[:, :, ATT_DIM_B:2 * ATT_DIM_B].reshape(1, bs, ts_len, H_B, HD_B)
    v_s = qkvb_s3[:, :, 2 * ATT_DIM_B:].reshape(1, bs, ts_len, H_B, HD_B)
    nconv = CONV_W - 1
    return (y_p.reshape(bp, tp_len, D_MODEL), y_s.reshape(bs, ts_len, D_MODEL),
            sfin_p[None], cnew_p[None, :, SUBLANES - nconv:, :], k_p, v_p,
            sfin_s[None], cnew_s[None, :, SUBLANES - nconv:, :], k_s, v_s)
```

```python
import functools

import jax
import jax.numpy as jnp
import numpy as np
from jax import lax
from jax.experimental import pallas as pl
from jax.experimental.pallas import tpu as pltpu

F32 = jnp.float32
BF16 = jnp.bfloat16

D_MODEL = 1024
CHUNK = 64
H_A = 4
DK_A = 128
DV_A = 128
CONV_W = 4
KEY_DIM_A = H_A * DK_A
VAL_DIM_A = H_A * DV_A
CONV_DIM = 2 * KEY_DIM_A + VAL_DIM_A
H_B = 8
HD_B = 64
ATT_DIM_B = H_B * HD_B
N_BACK = 8
BAND_PAST = N_BACK * CHUNK
REL_MAX = 256
N_EXPERTS = 32
TOP_K = 4
D_FF = 1024
SWIGLU_LIMIT = 7.0
SWIGLU_ALPHA = 1.702
PLE_DIM = 256
RMS_EPS = 1e-6
L2_EPS = 1e-6
NEG_BIG = -1e30

LANES = 128
SUBLANES = 8
VMEM_LIMIT = 56 * 1024 * 1024

PROJ_TM = 512
ATT_G = 2
ATT_PAIR_GROUPS = ((0, 1), (2, 3))
ATT_SAMPLE_BATCH = 4
DELTA_NCH = 4
SOLVE_SPLIT_STAGES = 2
MOE_BM = 512
COMB_TM = 512
COMB_GROUPS = 4


def _mm(a, b):
    return jnp.dot(a.astype(BF16), b.astype(BF16), preferred_element_type=F32)


def _mm_nt(a, b):
    return lax.dot_general(a.astype(BF16), b.astype(BF16), (((1,), (1,)), ((), ())),
                           preferred_element_type=F32)


def _sigmoid(x):
    return 1.0 / (1.0 + jnp.exp(-x))


def _softplus(x):
    return jnp.maximum(x, 0.0) + jnp.log(1.0 + jnp.exp(-jnp.abs(x)))


def _rms(x, g):
    ms = jnp.mean(x * x, axis=-1, keepdims=True)
    return x * lax.rsqrt(ms + RMS_EPS) * g


NSLAB = D_MODEL // LANES


def _store_slabs(ref, val):
    n = val.shape[0]
    for s in range(NSLAB):
        ref[pl.ds(s, n, stride=NSLAB), :] = val[:, s * LANES:(s + 1) * LANES]


def _load_slabs(ref, n):
    return jnp.concatenate([ref[pl.ds(s, n, stride=NSLAB), :] for s in range(NSLAB)], axis=1)


def _slab_rows(ref, r):
    return ref.at[pl.ds(pl.multiple_of(r * NSLAB, NSLAB), NSLAB), :]


def _cparams(sem):
    return pltpu.CompilerParams(dimension_semantics=sem, vmem_limit_bytes=VMEM_LIMIT)


def _proj_kernel(x_ref, gmix_ref, wa_ref, wba_ref, wb_ref, bd_ref, gq_ref, gk_ref,
                 qkvz_ref, ba_ref, qkvb_ref):
    h = _rms(x_ref[...], gmix_ref[...]).astype(BF16)
    qkvz_ref[...] = jnp.dot(h, wa_ref[...], preferred_element_type=F32)
    ba_ref[...] = jnp.dot(h, wba_ref[...], preferred_element_type=F32)
    pb = jnp.dot(h, wb_ref[...], preferred_element_type=F32)
    bd = bd_ref[...]

    def head_norm(q, g):
        ss = jnp.dot((q * q).astype(BF16), bd, preferred_element_type=F32)
        return q * lax.rsqrt(ss * (1.0 / HD_B) + RMS_EPS) * g

    qkvb_ref[:, 0:ATT_DIM_B] = head_norm(pb[:, 0:ATT_DIM_B], gq_ref[...])
    qkvb_ref[:, ATT_DIM_B:2 * ATT_DIM_B] = head_norm(pb[:, ATT_DIM_B:2 * ATT_DIM_B], gk_ref[...])
    qkvb_ref[:, 2 * ATT_DIM_B:] = pb[:, 2 * ATT_DIM_B:]


def _project(x2d, gmix, wa, wba, wb, bd, gq, gk):
    t = x2d.shape[0]
    tm = min(PROJ_TM, t)
    const = lambda i: (0, 0)
    return pl.pallas_call(
        _proj_kernel,
        grid=(t // tm,),
        in_specs=[
            pl.BlockSpec((tm, D_MODEL), lambda i: (i, 0)),
            pl.BlockSpec(gmix.shape, const),
            pl.BlockSpec(wa.shape, const),
            pl.BlockSpec(wba.shape, const),
            pl.BlockSpec(wb.shape, const),
            pl.BlockSpec(bd.shape, const),
            pl.BlockSpec(gq.shape, const),
            pl.BlockSpec(gk.shape, const),
        ],
        out_specs=[
            pl.BlockSpec((tm, wa.shape[1]), lambda i: (i, 0)),
            pl.BlockSpec((tm, LANES), lambda i: (i, 0)),
            pl.BlockSpec((tm, wb.shape[1]), lambda i: (i, 0)),
        ],
        out_shape=[
            jax.ShapeDtypeStruct((t, wa.shape[1]), F32),
            jax.ShapeDtypeStruct((t, LANES), F32),
            jax.ShapeDtypeStruct((t, wb.shape[1]), F32),
        ],
        compiler_params=_cparams(("arbitrary",)),
        name="proj",
    )(x2d, gmix, wa, wba, wb, bd, gq, gk)


def _delta_kernel(qkvz_ref, ba_ref, cbuf_ref, s0_ref, convw_ref, nega_ref, dtb_ref,
                  gon_ref, lbd_ref,
                  o_ref, sfin_ref, cnew_ref, xh_ref, s_ref, *, c, nch):
    step = pl.program_id(1)
    tile = c * nch
    hist = SUBLANES

    @pl.when(step == 0)
    def _():
        xh_ref[0:hist, :] = cbuf_ref[...]
        s_ref[...] = s0_ref[...]

    xh_ref[hist:hist + tile, :] = qkvz_ref[:, 0:CONV_DIM]
    w = convw_ref[...]
    conv = (xh_ref[hist - 3:hist - 3 + tile, :] * w[0:1, :]
            + xh_ref[hist - 2:hist - 2 + tile, :] * w[1:2, :]
            + xh_ref[hist - 1:hist - 1 + tile, :] * w[2:3, :]
            + xh_ref[hist:hist + tile, :] * w[3:4, :])
    conv = conv * _sigmoid(conv)
    last_rows = xh_ref[tile:tile + hist, :]
    cnew_ref[...] = last_rows
    xh_ref[0:hist, :] = last_rows

    ba = ba_ref[...]
    z_all = qkvz_ref[:, CONV_DIM:CONV_DIM + VAL_DIM_A]

    rs = H_A * c
    sk = H_A * DK_A
    row = lax.broadcasted_iota(jnp.int32, (rs, rs), 0)
    col = lax.broadcasted_iota(jnp.int32, (rs, rs), 1)
    same = (row // c) == (col // c)
    causal = jnp.logical_and(same, row >= col)
    strict = jnp.logical_and(same, row > col)
    wrow = lax.broadcasted_iota(jnp.int32, (rs, sk), 0)
    wcol = lax.broadcasted_iota(jnp.int32, (rs, sk), 1)
    head_block = (wrow // c) == (wcol // DK_A)
    lbd = lbd_ref[...]
    nsq = int(np.log2(c))

    def stack(fn):
        return jnp.concatenate([fn(h) for h in range(H_A)], axis=0)

    def spread(m):
        return jnp.where(head_block, jnp.concatenate([m] * H_A, axis=1), 0.0)

    def mm_split3(a_bf16, x):
        hi = x.astype(BF16)
        r1 = x - hi.astype(F32)
        mid = r1.astype(BF16)
        lo = (r1 - mid.astype(F32)).astype(BF16)
        return (jnp.dot(a_bf16, hi, preferred_element_type=F32)
                + jnp.dot(a_bf16, mid, preferred_element_type=F32)
                + jnp.dot(a_bf16, lo, preferred_element_type=F32))

    def l2n(m):
        return m * lax.rsqrt(jnp.sum(m * m, axis=-1, keepdims=True) + L2_EPS)

    def last_row(m, h, shape):
        return jnp.broadcast_to(m[h * c + c - 1:h * c + c, :], shape)

    cs = range(nch)
    q_st = [l2n(stack(lambda h: conv[ci * c:ci * c + c, h * DK_A:(h + 1) * DK_A])) * (DK_A ** -0.5)
            for ci in cs]
    k_st = [l2n(stack(lambda h: conv[ci * c:ci * c + c, KEY_DIM_A + h * DK_A:KEY_DIM_A + (h + 1) * DK_A]))
            for ci in cs]
    v_st = [stack(lambda h: conv[ci * c:ci * c + c, 2 * KEY_DIM_A + h * DV_A:2 * KEY_DIM_A + (h + 1) * DV_A])
            for ci in cs]
    beta = [_sigmoid(stack(lambda h: jnp.broadcast_to(ba[ci * c:ci * c + c, h:h + 1], (c, LANES))))
            for ci in cs]
    g = [nega_ref[...] * _softplus(
        stack(lambda h: jnp.broadcast_to(ba[ci * c:ci * c + c, H_A + h:H_A + h + 1], (c, LANES)))
        + dtb_ref[...]) for ci in cs]
    gc = [mm_split3(lbd, gi) for gi in g]
    kb = [k * b for k, b in zip(k_st, beta)]
    kk = [_mm_nt(a, b) for a, b in zip(kb, k_st)]
    qk = [_mm_nt(a, b) for a, b in zip(q_st, k_st)]
    decay = [jnp.exp(jnp.where(causal,
                               jnp.concatenate([m] * (rs // LANES), axis=1)
                               - jnp.broadcast_to(m.T[0:1, :], (rs, rs)),
                               -jnp.inf)) for m in gc]
    egc = [jnp.exp(m) for m in gc]
    ps = [-jnp.where(strict, a * d, 0.0) for a, d in zip(kk, decay)]
    qk = [a * d for a, d in zip(qk, decay)]
    xs = [jnp.concatenate([v * b, k * e], axis=-1) for v, b, k, e in zip(v_st, beta, kb, egc)]
    def mm_split_rhs(a, b):
        bh = b.astype(BF16)
        bl = (b - bh.astype(F32)).astype(BF16)
        ah = a.astype(BF16)
        return (jnp.dot(ah, bh, preferred_element_type=F32)
                + jnp.dot(ah, bl, preferred_element_type=F32))

    for s in range(nsq):
        apply = mm_split_rhs if s < SOLVE_SPLIT_STAGES else _mm
        xs = [x + apply(p, x) for x, p in zip(xs, ps)]
        if s + 1 < nsq:
            ps = [_mm(p, p) for p in ps]
    qbd = [spread(q * e) for q, e in zip(q_st, egc)]
    kbd = [spread(k * jnp.exp(stack(lambda h: last_row(m, h, (c, LANES))) - m)) for k, m in zip(k_st, gc)]
    egl = [stack(lambda h: last_row(e, h, (DK_A, DV_A))) for e in egc]

    for ci in range(nch):
        r0 = ci * c
        eglast = egl[ci]
        u0 = xs[ci][:, 0:DV_A]
        wbd = spread(xs[ci][:, DV_A:])
        s_all = s_ref[...]
        u = u0 - _mm(wbd, s_all)
        o = _mm(qbd[ci], s_all) + _mm(qk[ci], u)
        s_ref[...] = s_all * eglast + lax.dot_general(
            kbd[ci].astype(BF16), u.astype(BF16), (((0,), (0,)), ((), ())), preferred_element_type=F32)
        on = _rms(o, gon_ref[...])
        for h in range(H_A):
            z = z_all[r0:r0 + c, h * DV_A:(h + 1) * DV_A]
            o_ref[r0:r0 + c, h * DV_A:(h + 1) * DV_A] = on[h * c:(h + 1) * c, :] * (z * _sigmoid(z))

    sfin_ref[...] = s_ref[...]


def _delta_mixer(qkvz, ba, cbuf, s0, convw, a_log, dt_bias, gon):
    b, t, _ = qkvz.shape
    c = min(CHUNK, t)
    nch = min(DELTA_NCH, t // c)
    tile = c * nch
    rs = H_A * c
    sk = H_A * DK_A
    assert rs % LANES == 0
    r = np.arange(rs)
    lbd = jnp.asarray((r[:, None] // c == r[None, :] // c) & (r[:, None] >= r[None, :]), BF16)
    nega = jnp.broadcast_to(jnp.repeat(-jnp.exp(a_log), c)[:, None], (rs, LANES))
    dtb = jnp.broadcast_to(jnp.repeat(dt_bias, c)[:, None], (rs, LANES))
    const2 = lambda i, j: (0, 0)
    kern = functools.partial(_delta_kernel, c=c, nch=nch)
    return pl.pallas_call(
        kern,
        grid=(b, t // tile),
        in_specs=[
            pl.BlockSpec((None, tile, qkvz.shape[2]), lambda i, j: (i, j, 0)),
            pl.BlockSpec((None, tile, LANES), lambda i, j: (i, j, 0)),
            pl.BlockSpec((None, SUBLANES, CONV_DIM), lambda i, j: (i, 0, 0)),
            pl.BlockSpec((None, sk, DV_A), lambda i, j: (i, 0, 0)),
            pl.BlockSpec(convw.shape, const2),
            pl.BlockSpec(nega.shape, const2),
            pl.BlockSpec(dtb.shape, const2),
            pl.BlockSpec(gon.shape, const2),
            pl.BlockSpec(lbd.shape, const2),
        ],
        out_specs=[
            pl.BlockSpec((None, tile, VAL_DIM_A), lambda i, j: (i, j, 0)),
            pl.BlockSpec((None, sk, DV_A), lambda i, j: (i, 0, 0)),
            pl.BlockSpec((None, SUBLANES, CONV_DIM), lambda i, j: (i, 0, 0)),
        ],
        out_shape=[
            jax.ShapeDtypeStruct((b, t, VAL_DIM_A), F32),
            jax.ShapeDtypeStruct((b, sk, DV_A), F32),
            jax.ShapeDtypeStruct((b, SUBLANES, CONV_DIM), F32),
        ],
        scratch_shapes=[
            pltpu.VMEM((SUBLANES + tile, CONV_DIM), F32),
            pltpu.VMEM((sk, DV_A), F32),
        ],
        compiler_params=_cparams(("arbitrary", "arbitrary")),
        name="delta",
    )(qkvz, ba, cbuf, s0, convw, nega, dtb, gon, lbd)


def _pair_queries(qp):
    lane = lax.broadcasted_iota(jnp.int32, qp.shape, 1)
    q_even = jnp.where(lane < HD_B, qp, 0.0)
    q_odd = jnp.where(lane >= HD_B, qp, 0.0)
    return jnp.concatenate([q_even, q_odd], axis=0).astype(BF16)


def _unpair(o, r):
    lane = lax.broadcasted_iota(jnp.int32, (r, LANES), 1)
    return jnp.where(lane < HD_B, o[0:r, :], o[r:2 * r, :])


def _attn_prompt_kernel(q_ref, k_ref, v_ref, bias_ref, o_ref, kwin_ref, vwin_ref, *, g):
    cstep = pl.program_id(1)
    rows = g * CHUNK
    win = (N_BACK + g) * CHUNK

    @pl.when(cstep == 0)
    def _():
        kwin_ref[...] = jnp.zeros(kwin_ref.shape, BF16)
        vwin_ref[...] = jnp.zeros(vwin_ref.shape, BF16)

    for i in range(N_BACK // g):
        kwin_ref[i * rows:(i + 1) * rows, :] = kwin_ref[(i + 1) * rows:(i + 2) * rows, :]
        vwin_ref[i * rows:(i + 1) * rows, :] = vwin_ref[(i + 1) * rows:(i + 2) * rows, :]
    kwin_ref[win - rows:win, :] = k_ref[...].astype(BF16)
    vwin_ref[win - rows:win, :] = v_ref[...].astype(BF16)

    slot_chunk = lax.broadcasted_iota(jnp.int32, (2 * rows, win), 1) // CHUNK
    in_seq = slot_chunk >= N_BACK - cstep * g
    for pairs in ATT_PAIR_GROUPS:
        sls = [slice(p * LANES, (p + 1) * LANES) for p in pairs]
        s = [_mm_nt(_pair_queries(q_ref[:, sl] * (HD_B ** -0.5)), kwin_ref[:, sl]) for sl in sls]
        s = [jnp.where(in_seq, sp + bias_ref[p], NEG_BIG) for p, sp in zip(pairs, s)]
        m = [jnp.max(sp, axis=-1, keepdims=True) for sp in s]
        e = [jnp.exp(sp - mp) for sp, mp in zip(s, m)]
        l = [jnp.sum(ep, axis=-1, keepdims=True) for ep in e]
        o = [jnp.dot(ep.astype(BF16), vwin_ref[:, sl], preferred_element_type=F32) for ep, sl in zip(e, sls)]
        for sl, op, lp in zip(sls, o, l):
            o_ref[:, sl] = _unpair(op / lp, rows)


def _attend_prompt(qkvb, bias):
    b, t, _ = qkvb.shape
    g = ATT_G
    rows = g * CHUNK
    win = (N_BACK + g) * CHUNK
    kern = functools.partial(_attn_prompt_kernel, g=g)
    return pl.pallas_call(
        kern,
        grid=(b, t // rows),
        in_specs=[
            pl.BlockSpec((None, rows, ATT_DIM_B), lambda i, j: (i, j, 0)),
            pl.BlockSpec((None, rows, ATT_DIM_B), lambda i, j: (i, j, 1)),
            pl.BlockSpec((None, rows, ATT_DIM_B), lambda i, j: (i, j, 2)),
            pl.BlockSpec(bias.shape, lambda i, j: (0, 0, 0)),
        ],
        out_specs=pl.BlockSpec((None, rows, ATT_DIM_B), lambda i, j: (i, j, 0)),
        out_shape=jax.ShapeDtypeStruct((b, t, ATT_DIM_B), F32),
        scratch_shapes=[pltpu.VMEM((win, ATT_DIM_B), BF16), pltpu.VMEM((win, ATT_DIM_B), BF16)],
        compiler_params=_cparams(("arbitrary", "arbitrary")),
        name="attn_prompt",
    )(qkvb, qkvb, qkvb, bias)


def _attn_sample_kernel(q_ref, kn_ref, vn_ref, kc_ref, vc_ref, biasc_ref, biasn_ref, o_ref):
    nb, ds = q_ref.shape[0], q_ref.shape[1]
    bs = range(nb)
    for p in range(H_B // 2):
        sl = slice(p * LANES, (p + 1) * LANES)
        q2 = [_pair_queries(q_ref[b, :, sl] * (HD_B ** -0.5)) for b in bs]
        s1 = [_mm_nt(q2[b], kc_ref[b, :, sl]) + biasc_ref[p] for b in bs]
        s2 = [_mm_nt(q2[b], kn_ref[b, :, sl]) + biasn_ref[p] for b in bs]
        m = [jnp.maximum(jnp.max(s1[b], axis=-1, keepdims=True), jnp.max(s2[b], axis=-1, keepdims=True))
             for b in bs]
        e1 = [jnp.exp(s1[b] - m[b]) for b in bs]
        e2 = [jnp.exp(s2[b] - m[b]) for b in bs]
        l = [jnp.sum(e1[b], axis=-1, keepdims=True) + jnp.sum(e2[b], axis=-1, keepdims=True) for b in bs]
        o = [(_mm(e1[b], vc_ref[b, :, sl]) + _mm(e2[b], vn_ref[b, :, sl])) / l[b] for b in bs]
        for b in bs:
            o_ref[b, :, sl] = _unpair(o[b], ds)


def _attend_sample(qkvb, cache_k, cache_v, biasc, biasn):
    b, ds, _ = qkvb.shape
    past = cache_k.shape[1]
    g = max(n for n in (ATT_SAMPLE_BATCH, 2, 1) if b % n == 0)
    return pl.pallas_call(
        _attn_sample_kernel,
        grid=(b // g,),
        in_specs=[
            pl.BlockSpec((g, ds, ATT_DIM_B), lambda i: (i, 0, 0)),
            pl.BlockSpec((g, ds, ATT_DIM_B), lambda i: (i, 0, 1)),
            pl.BlockSpec((g, ds, ATT_DIM_B), lambda i: (i, 0, 2)),
            pl.BlockSpec((g, past, ATT_DIM_B), lambda i: (i, 0, 0)),
            pl.BlockSpec((g, past, ATT_DIM_B), lambda i: (i, 0, 0)),
            pl.BlockSpec(biasc.shape, lambda i: (0, 0, 0)),
            pl.BlockSpec(biasn.shape, lambda i: (0, 0, 0)),
        ],
        out_specs=pl.BlockSpec((g, ds, ATT_DIM_B), lambda i: (i, 0, 0)),
        out_shape=jax.ShapeDtypeStruct((b, ds, ATT_DIM_B), F32),
        compiler_params=_cparams(("arbitrary",)),
        name="attn_sample",
    )(qkvb, qkvb, qkvb, cache_k, cache_v, biasc, biasn)


def _router_kernel(oap_ref, obp_ref, xp_ref, oas_ref, obs_ref, xs_ref,
                   woa_ref, wob_ref, gffn_ref, wrt_ref, brt_ref, utri_ref,
                   x1_ref, h2_ref, idx_ref, gate_ref, rank_ref, cnt_ref,
                   carry_s, *, ntp):
    i = pl.program_id(0)
    tm = x1_ref.shape[0]

    @pl.when(i == 0)
    def _():
        carry_s[...] = jnp.zeros(carry_s.shape, F32)

    def mix(oa_ref, ob_ref, x_ref):
        x1_ref[...] = x_ref[...] + _mm(oa_ref[...], woa_ref[...]) + _mm(ob_ref[...], wob_ref[...])

    @pl.when(i < ntp)
    def _():
        mix(oap_ref, obp_ref, xp_ref)

    @pl.when(i >= ntp)
    def _():
        mix(oas_ref, obs_ref, xs_ref)

    x1 = x1_ref[...]
    h2 = _rms(x1, gffn_ref[...])
    _store_slabs(h2_ref, h2)
    logits = _mm_nt(wrt_ref[...], h2) + brt_ref[:, 0:1]

    eidx = lax.broadcasted_iota(jnp.int32, (N_EXPERTS, tm), 0).astype(F32)
    vals = logits
    memf = jnp.zeros((N_EXPERTS, tm), F32)
    tops, sels = [], []
    for k in range(TOP_K):
        m = jnp.max(vals, axis=0, keepdims=True)
        ix = jnp.min(jnp.where(vals == m, eidx, float(N_EXPERTS)), axis=0, keepdims=True)
        sel = eidx == ix
        tops.append(m)
        sels.append(sel)
        idx_ref[k:k + 1, :] = ix.astype(jnp.int32)
        vals = jnp.where(sel, -jnp.inf, vals)
        memf = memf + jnp.where(sel, 1.0, 0.0)
    es = [jnp.exp(v - tops[0]) for v in tops]
    denom = es[0] + es[1] + es[2] + es[3]
    for k in range(TOP_K):
        gate_ref[k:k + 1, :] = es[k] / denom

    carry = carry_s[:, 0:1]
    excl = _mm(memf, utri_ref[...]) + carry
    for k in range(TOP_K):
        r = jnp.sum(jnp.where(sels[k], excl, 0.0), axis=0, keepdims=True)
        rank_ref[k:k + 1, :] = r.astype(jnp.int32)
    new_carry = carry + jnp.sum(memf, axis=1, keepdims=True)
    carry_s[...] = jnp.broadcast_to(new_carry, carry_s.shape)
    cnt_ref[...] = jnp.broadcast_to(new_carry, cnt_ref.shape)


def _out_router(oa_p, ob_p, x_p, oa_s, ob_s, x_s, woa, wob, gffn, wrt, brt):
    tp, ts = x_p.shape[0], x_s.shape[0]
    tm = min(PROJ_TM, ts, tp)
    ntp, nts = tp // tm, ts // tm
    ttot = tp + ts
    utri = jnp.triu(jnp.ones((tm, tm), BF16), k=1)
    pmap = lambda i: (jnp.minimum(i, ntp - 1), 0)
    smap = lambda i: (jnp.maximum(i - ntp, 0), 0)
    const = lambda i: (0, 0)
    kern = functools.partial(_router_kernel, ntp=ntp)
    return pl.pallas_call(
        kern,
        grid=(ntp + nts,),
        in_specs=[
            pl.BlockSpec((tm, VAL_DIM_A), pmap),
            pl.BlockSpec((tm, ATT_DIM_B), pmap),
            pl.BlockSpec((tm, D_MODEL), pmap),
            pl.BlockSpec((tm, VAL_DIM_A), smap),
            pl.BlockSpec((tm, ATT_DIM_B), smap),
            pl.BlockSpec((tm, D_MODEL), smap),
            pl.BlockSpec(woa.shape, const),
            pl.BlockSpec(wob.shape, const),
            pl.BlockSpec(gffn.shape, const),
            pl.BlockSpec(wrt.shape, const),
            pl.BlockSpec(brt.shape, const),
            pl.BlockSpec(utri.shape, const),
        ],
        out_specs=[
            pl.BlockSpec((tm, D_MODEL), lambda i: (i, 0)),
            pl.BlockSpec((tm * NSLAB, LANES), lambda i: (i, 0)),
            pl.BlockSpec((TOP_K, tm), lambda i: (0, i)),
            pl.BlockSpec((TOP_K, tm), lambda i: (0, i)),
            pl.BlockSpec((TOP_K, tm), lambda i: (0, i)),
            pl.BlockSpec((N_EXPERTS, LANES), const),
        ],
        out_shape=[
            jax.ShapeDtypeStruct((ttot, D_MODEL), F32),
            jax.ShapeDtypeStruct((ttot * NSLAB, LANES), F32),
            jax.ShapeDtypeStruct((TOP_K, ttot), jnp.int32),
            jax.ShapeDtypeStruct((TOP_K, ttot), F32),
            jax.ShapeDtypeStruct((TOP_K, ttot), jnp.int32),
            jax.ShapeDtypeStruct((N_EXPERTS, LANES), F32),
        ],
        scratch_shapes=[pltpu.VMEM((N_EXPERTS, LANES), F32)],
        compiler_params=_cparams(("arbitrary",)),
        name="out_router",
    )(oa_p, ob_p, x_p, oa_s, ob_s, x_s, woa, wob, gffn, wrt, brt, utri)


def _row_copy(src_ref, s, dst_ref, d, sem):
    return pltpu.make_async_copy(_slab_rows(src_ref, s), _slab_rows(dst_ref, d), sem)


def _dispatch_kernel(padlo_ref, padhi_ref, dest_ref, h2_ref, xs_ref, zrow_ref, sem):
    i = pl.program_id(0)
    tm = h2_ref.shape[0] // NSLAB

    def issue(t, carry):
        for k in range(TOP_K):
            _row_copy(h2_ref, t, xs_ref, dest_ref[k, t], sem).start(priority=k % 2)
        return carry

    lax.fori_loop(0, tm, issue, 0, unroll=8)
    for k in range(TOP_K):
        pltpu.make_async_copy(h2_ref, xs_ref.at[pl.ds(0, tm * NSLAB), :], sem).wait()

    @pl.when(i == pl.num_programs(0) - 1)
    def _():
        zrow_ref[...] = jnp.zeros(zrow_ref.shape, F32)
        for e in range(N_EXPERTS):
            lo = padlo_ref[e]
            hi = padhi_ref[e]

            def zissue(r, carry):
                _row_copy(zrow_ref, 0, xs_ref, r, sem).start()
                return carry

            lax.fori_loop(lo, hi, zissue, 0)

            def zdrain(r, carry):
                _row_copy(zrow_ref, 0, xs_ref, 0, sem).wait()
                return carry

            lax.fori_loop(lo, hi, zdrain, 0)


def _dispatch(padlo, padhi, dest, h2, n_rows, tm_rows):
    ttot = h2.shape[0] // NSLAB
    tm = tm_rows
    grid_spec = pltpu.PrefetchScalarGridSpec(
        num_scalar_prefetch=2,
        grid=(ttot // tm,),
        in_specs=[
            pl.BlockSpec((TOP_K, tm), lambda i, lo, hi: (0, i), memory_space=pltpu.SMEM),
            pl.BlockSpec((tm * NSLAB, LANES), lambda i, lo, hi: (i, 0)),
        ],
        out_specs=pl.BlockSpec(memory_space=pl.ANY),
        scratch_shapes=[pltpu.VMEM((NSLAB, LANES), F32), pltpu.SemaphoreType.DMA(())],
    )
    return pl.pallas_call(
        _dispatch_kernel,
        grid_spec=grid_spec,
        out_shape=jax.ShapeDtypeStruct((n_rows * NSLAB, LANES), F32),
        compiler_params=_cparams(("arbitrary",)),
        name="dispatch",
    )(padlo, padhi, dest, h2)


def _expert_kernel(bexp_ref, nused_ref, x_ref, wgu_ref, bgu_ref, wd_ref, bd_ref, y_ref, wgu_s, wd_s):
    j = pl.program_id(0)
    active = j < nused_ref[0]
    new_expert = jnp.logical_or(j == 0, bexp_ref[j] != bexp_ref[jnp.maximum(j - 1, 0)])

    @pl.when(jnp.logical_and(active, new_expert))
    def _():
        rows = 128
        for r in range(0, D_MODEL, rows):
            wgu_s[r:r + rows, :] = wgu_ref[r:r + rows, :].astype(BF16)
        for r in range(0, D_FF, rows):
            wd_s[r:r + rows, :] = wd_ref[r:r + rows, :].astype(BF16)

    @pl.when(active)
    def _():
        bm = x_ref.shape[0] // NSLAB
        gu = _mm(_load_slabs(x_ref, bm), wgu_s[...]) + bgu_ref[...]
        gate = jnp.minimum(gu[:, 0:D_FF], SWIGLU_LIMIT)
        up = jnp.clip(gu[:, D_FF:], -SWIGLU_LIMIT, SWIGLU_LIMIT)
        hid = (up + 1.0) * (gate * _sigmoid(gate * SWIGLU_ALPHA))
        _store_slabs(y_ref, _mm(hid, wd_s[...]) + bd_ref[...])


def _experts(bexp, nused, xs, wgu, bgu, wd, bd):
    n_rows = xs.shape[0] // NSLAB
    bm = MOE_BM
    nb = n_rows // bm
    row_map = lambda j, be, nu: (jnp.minimum(j, nu[0] - 1), 0)
    w_map = lambda j, be, nu: (be[j], 0, 0)
    grid_spec = pltpu.PrefetchScalarGridSpec(
        num_scalar_prefetch=2,
        grid=(nb,),
        in_specs=[
            pl.BlockSpec((bm * NSLAB, LANES), row_map),
            pl.BlockSpec((None, D_MODEL, 2 * D_FF), w_map),
            pl.BlockSpec((None, 1, 2 * D_FF), w_map),
            pl.BlockSpec((None, D_FF, D_MODEL), w_map),
            pl.BlockSpec((None, 1, D_MODEL), w_map),
        ],
        out_specs=pl.BlockSpec((bm * NSLAB, LANES), row_map),
        scratch_shapes=[pltpu.VMEM((D_MODEL, 2 * D_FF), BF16), pltpu.VMEM((D_FF, D_MODEL), BF16)],
    )
    return pl.pallas_call(
        _expert_kernel,
        grid_spec=grid_spec,
        out_shape=jax.ShapeDtypeStruct((n_rows * NSLAB, LANES), F32),
        compiler_params=_cparams(("arbitrary",)),
        name="experts",
    )(bexp, nused, xs, wgu, bgu, wd, bd)


def _combine_kernel(dest_ref, destn_ref, gate_ref, x1_ref, pp_ref, ps_ref, yb_ref, gple_ref, wg_ref, wp_ref,
                    yp_ref, ys_ref, buf_ref, sems, *, ntp):
    i = pl.program_id(0)
    tm = x1_ref.shape[0]
    per_slot = TOP_K * tm
    slot = lax.rem(i, 2)

    def gather(d_ref, s):
        def issue(t, carry):
            for k in range(TOP_K):
                _row_copy(yb_ref, d_ref[k, t], buf_ref, s * per_slot + k * tm + t,
                          sems.at[s]).start(priority=k % 2)
            return carry

        lax.fori_loop(0, tm, issue, 0, unroll=8)

    @pl.when(i == 0)
    def _():
        gather(dest_ref, 0)

    @pl.when(i + 1 < pl.num_programs(0))
    def _():
        gather(destn_ref, 1 - slot)

    base = pl.multiple_of(slot * per_slot * NSLAB, NSLAB)
    pltpu.make_async_copy(yb_ref.at[pl.ds(0, per_slot * NSLAB), :],
                          buf_ref.at[pl.ds(base, per_slot * NSLAB), :], sems.at[slot]).wait()

    ng = COMB_GROUPS if tm % (COMB_GROUPS * SUBLANES) == 0 else 1
    gr = tm // ng
    groups = range(ng)

    def expert_rows(h, k):
        start = base + (k * tm + h * gr) * NSLAB
        return jnp.concatenate(
            [buf_ref[pl.ds(start + s, gr, stride=NSLAB), :] for s in range(NSLAB)], axis=1)

    gates = [gate_ref[h * gr:(h + 1) * gr, :] for h in groups]
    moe = [expert_rows(h, 0) * gates[h][:, 0:1] for h in groups]
    for k in range(1, TOP_K):
        moe = [moe[h] + expert_rows(h, k) * gates[h][:, k:k + 1] for h in groups]
    x2 = [x1_ref[h * gr:(h + 1) * gr, :] + moe[h] for h in groups]
    hn = [_rms(x2[h], gple_ref[...]) for h in groups]
    gate = [_sigmoid(_mm(hn[h], wg_ref[...])) for h in groups]

    def finish(p_ref, y_ref):
        proj = [_mm(p_ref[h * gr:(h + 1) * gr, :], wp_ref[...]) for h in groups]
        for h in groups:
            y_ref[h * gr:(h + 1) * gr, :] = x2[h] + gate[h] * proj[h]

    @pl.when(i < ntp)
    def _():
        finish(pp_ref, yp_ref)

    @pl.when(i >= ntp)
    def _():
        finish(ps_ref, ys_ref)


def _combine(dest, gates_col, x1, p_p, p_s, yb, gple, wg, wp):
    tp, ts = p_p.shape[0], p_s.shape[0]
    tm = min(COMB_TM, tp, ts)
    ntp, nts = tp // tm, ts // tm
    pmap = lambda i: (jnp.minimum(i, ntp - 1), 0)
    smap = lambda i: (jnp.maximum(i - ntp, 0), 0)
    const = lambda i: (0, 0)
    kern = functools.partial(_combine_kernel, ntp=ntp)
    return pl.pallas_call(
        kern,
        grid=(ntp + nts,),
        in_specs=[
            pl.BlockSpec((TOP_K, tm), lambda i: (0, i), memory_space=pltpu.SMEM),
            pl.BlockSpec((TOP_K, tm), lambda i: (0, jnp.minimum(i + 1, ntp + nts - 1)),
                         memory_space=pltpu.SMEM),
            pl.BlockSpec((tm, TOP_K), lambda i: (i, 0)),
            pl.BlockSpec((tm, D_MODEL), lambda i: (i, 0)),
            pl.BlockSpec((tm, PLE_DIM), pmap),
            pl.BlockSpec((tm, PLE_DIM), smap),
            pl.BlockSpec(memory_space=pl.ANY),
            pl.BlockSpec(gple.shape, const),
            pl.BlockSpec(wg.shape, const),
            pl.BlockSpec(wp.shape, const),
        ],
        out_specs=[
            pl.BlockSpec((tm, D_MODEL), pmap),
            pl.BlockSpec((tm, D_MODEL), smap),
        ],
        out_shape=[
            jax.ShapeDtypeStruct((tp, D_MODEL), F32),
            jax.ShapeDtypeStruct((ts, D_MODEL), F32),
        ],
        scratch_shapes=[pltpu.VMEM((2 * TOP_K * tm * NSLAB, LANES), F32), pltpu.SemaphoreType.DMA((2,))],
        compiler_params=_cparams(("arbitrary",)),
        name="combine",
    )(dest, dest, gates_col, x1, p_p, p_s, yb, gple, wg, wp)


def _bias_tables(rel_bias, n_q, n_k, key_offset, band):
    i = np.arange(n_q)[:, None]
    j = np.arange(n_k)[None, :]
    d_max = n_q - 1 + key_offset
    d_min = key_offset - (n_k - 1)
    n_hi = max(0, d_max - REL_MAX)
    n_lo = max(0, -REL_MAX - d_min)
    mid = rel_bias[:, max(d_min, -REL_MAX) + REL_MAX:min(d_max, REL_MAX) + REL_MAX + 1][:, ::-1]
    e = jnp.concatenate([jnp.broadcast_to(rel_bias[:, 2 * REL_MAX:], (H_B, n_hi)), mid,
                         jnp.broadcast_to(rel_bias[:, 0:1], (H_B, n_lo))], axis=1)
    period = n_q + n_k
    e = jnp.pad(e, ((0, 0), (0, 1)))
    flat = jnp.tile(e, (1, n_q + 1))[:, 0:n_q * (period + 1)]
    tab = flat.reshape(H_B, n_q, period + 1)[:, ::-1, 0:n_k]
    if band:
        qc = i // CHUNK
        sc = j // CHUNK
        ok = (sc >= qc) & (sc <= qc + N_BACK)
        tab = jnp.where(jnp.asarray(ok)[None], tab, NEG_BIG)
    return tab.reshape(H_B // 2, 2 * n_q, n_k)


def kernel(x_prompt, x_sample, state_delta, state_conv, cache_k, cache_v, p_prompt, p_sample, g_mix, w_in, conv_w, a_log, dt_bias, g_onorm, g_qnorm, g_knorm, rel_bias, w_out, g_ffn, w_router, b_router, w_gu, b_gu, w_down, b_down, g_ple, w_ple_gate, w_ple_proj):
    assert w_in.shape[0] == 1, "single-layer kernel"
    bp, tp_len, _ = x_prompt.shape
    bs, ts_len, _ = x_sample.shape
    tp, ts = bp * tp_len, bs * ts_len
    past = cache_k.shape[2]

    w = w_in[0]
    o_z = CONV_DIM + VAL_DIM_A
    o_qb = o_z + 2 * H_A
    wa = w[:, 0:o_z].astype(BF16)
    wba = jnp.pad(w[:, o_z:o_qb], ((0, 0), (0, LANES - 2 * H_A))).astype(BF16)
    wb = w[:, o_qb:].astype(BF16)
    grp = np.arange(ATT_DIM_B) // HD_B
    bd = jnp.asarray(grp[:, None] == grp[None, :], BF16)
    gq = jnp.tile(g_qnorm[0], H_B)[None, :]
    gk = jnp.tile(g_knorm[0], H_B)[None, :]
    gmix = g_mix[0][None, :]
    gon = g_onorm[0][None, :]
    convw = conv_w[0]
    woa = w_out[0, 0:VAL_DIM_A].astype(BF16)
    wob = w_out[0, VAL_DIM_A:].astype(BF16)
    gffn = g_ffn[0][None, :]
    wrt = w_router[0].T.astype(BF16)
    brt = jnp.broadcast_to(b_router[0][:, None], (N_EXPERTS, LANES))
    wgu = w_gu[0]
    bgu = b_gu[0][:, None, :]
    wd = w_down[0]
    bdn = b_down[0][:, None, :]
    gple = g_ple[0][None, :]
    wg = w_ple_gate[0].astype(BF16)
    wp = w_ple_proj[0].astype(BF16)

    xp2 = x_prompt.reshape(tp, D_MODEL)
    xs2 = x_sample.reshape(ts, D_MODEL)

    qkvz_p, ba_p, qkvb_p = _project(xp2, gmix, wa, wba, wb, bd, gq, gk)
    qkvz_s, ba_s, qkvb_s = _project(xs2, gmix, wa, wba, wb, bd, gq, gk)

    hist_pad = ((0, 0), (SUBLANES - (CONV_W - 1), 0), (0, 0))
    cbuf_p = jnp.zeros((bp, SUBLANES, CONV_DIM), F32)
    cbuf_s = jnp.pad(state_conv[0], hist_pad)
    s0_p = jnp.zeros((bp, H_A * DK_A, DV_A), F32)
    s0_s = state_delta[0].reshape(bs, H_A * DK_A, DV_A)
    oa_p, sfin_p, cnew_p = _delta_mixer(qkvz_p.reshape(bp, tp_len, -1), ba_p.reshape(bp, tp_len, LANES),
                                        cbuf_p, s0_p, convw, a_log[0], dt_bias[0], gon)
    oa_s, sfin_s, cnew_s = _delta_mixer(qkvz_s.reshape(bs, ts_len, -1), ba_s.reshape(bs, ts_len, LANES),
                                        cbuf_s, s0_s, convw, a_log[0], dt_bias[0], gon)
    sfin_p = sfin_p.reshape(bp, H_A, DK_A, DV_A)
    sfin_s = sfin_s.reshape(bs, H_A, DK_A, DV_A)

    qkvb_p3 = qkvb_p.reshape(bp, tp_len, 3 * ATT_DIM_B)
    qkvb_s3 = qkvb_s.reshape(bs, ts_len, 3 * ATT_DIM_B)
    bias_p = _bias_tables(rel_bias[0], ATT_G * CHUNK, (N_BACK + ATT_G) * CHUNK, BAND_PAST, True)
    ob_p = _attend_prompt(qkvb_p3, bias_p)
    bias_s = _bias_tables(rel_bias[0], ts_len, past + ts_len, past, False)
    ob_s = _attend_sample(qkvb_s3, cache_k[0].reshape(bs, past, ATT_DIM_B),
                          cache_v[0].reshape(bs, past, ATT_DIM_B),
                          bias_s[:, :, 0:past], bias_s[:, :, past:])

    x1, h2, idx, gates, rank, cnt = _out_router(
        oa_p.reshape(tp, VAL_DIM_A), ob_p.reshape(tp, ATT_DIM_B), xp2,
        oa_s.reshape(ts, VAL_DIM_A), ob_s.reshape(ts, ATT_DIM_B), xs2,
        woa, wob, gffn, wrt, brt)
    ttot = tp + ts
    bm = MOE_BM
    counts = cnt[:, 0].astype(jnp.int32)
    padded = (counts + bm - 1) // bm * bm
    pend = jnp.cumsum(padded)
    pstart = pend - padded
    eids = jnp.arange(N_EXPERTS, dtype=jnp.int32)
    start_of = jnp.sum(jnp.where(idx[None] == eids[:, None, None], pstart[:, None, None], 0), axis=0)
    dest = start_of + rank
    nb = -(-(ttot * TOP_K) // bm) + N_EXPERTS
    nused = (pend[-1] // bm).astype(jnp.int32)[None]
    first = jnp.minimum(jnp.arange(nb, dtype=jnp.int32), nused[0] - 1) * bm
    bexp = jnp.minimum(jnp.sum((pend[None, :] <= first[:, None]).astype(jnp.int32), axis=1),
                       N_EXPERTS - 1)

    xs_rows = _dispatch(pstart + counts, pend, dest, h2, nb * bm, min(COMB_TM, tp, ts))
    yb = _experts(bexp, nused, xs_rows, wgu, bgu, wd, bdn)
    y_p, y_s = _combine(dest, gates.T, x1, p_prompt[0].reshape(tp, PLE_DIM),
                        p_sample[0].reshape(ts, PLE_DIM), yb, gple, wg, wp)

    keep = min(BAND_PAST, tp_len)
    k_p = qkvb_p3[:, tp_len - keep:, ATT_DIM_B:2 * ATT_DIM_B].reshape(1, bp, keep, H_B, HD_B)
    v_p = qkvb_p3[:, tp_len - keep:, 2 * ATT_DIM_B:].reshape(1, bp, keep, H_B, HD_B)
    k_s = qkvb_s3[:, :, ATT_DIM_B:2 * ATT_DIM_B].reshape(1, bs, ts_len, H_B, HD_B)
    v_s = qkvb_s3[:, :, 2 * ATT_DIM_B:].reshape(1, bs, ts_len, H_B, HD_B)
    nconv = CONV_W - 1
    return (y_p.reshape(bp, tp_len, D_MODEL), y_s.reshape(bs, ts_len, D_MODEL),
            sfin_p[None], cnew_p[None, :, SUBLANES - nconv:, :], k_p, v_p,
            sfin_s[None], cnew_s[None, :, SUBLANES - nconv:, :], k_s, v_s)
```
